```python
import math, functools
import jax, jax.numpy as jnp
from jax import lax
import numpy as np

D_MODEL = 1024
BATCH = 32
SEQ = 256
DEPTH = 2
DEC_BATCH = 2
DEC_SEQ = 4096
PAST_LEN = 256

GRID_W = 64
N_DIR = 2
MIX_WIDTH = D_MODEL
SSM_CH = D_MODEL // 4
SSM_GROUP_CH = 16
SSM_GROUPS = SSM_CH // SSM_GROUP_CH
SSM_STATE = 64
DN_HEADS = 4
DN_HEAD_DIM = D_MODEL // 8
DN_WIDTH = DN_HEADS * DN_HEAD_DIM
DN_CONV = 3
DN_CHUNK = 64
CM_HEADS = 4
CM_WIDTH = D_MODEL // 4
CM_HEAD_DIM = CM_WIDTH // CM_HEADS
CM_CHUNK = 128
ROWS_PER_CHUNK = CM_CHUNK // GRID_W
IN_COLS = SSM_CH + 4 * DN_WIDTH + 2 * N_DIR * DN_HEADS + 2 * CM_WIDTH
N_EXPERTS = 64
TOP_K = 6
N_ROUTE_GROUPS = 8
TOPK_ROUTE_GROUPS = 4
EXPERT_FF = 128
SHARED_FF = 256
ROUTED_SCALE = 2.5
MOE_BLOCK = 128
N_MOD = 6
EPS = 1e-6

kernel_name = 'hybrid_diffusion_prefix_trunk_step'


def rmsnorm(x, g):
    xf = x.astype(jnp.float32)
    y = xf * lax.rsqrt(jnp.mean(xf * xf, axis=-1, keepdims=True) + EPS)
    return y.astype(x.dtype) * g


def l2norm(x):
    return x * lax.rsqrt(jnp.sum(x * x, axis=-1, keepdims=True) + EPS)


def cmul(ar, ai, br, bi):
    return ar * br - ai * bi, ar * bi + ai * br


def s5_direction(u, lam_re, lam_im, log_dt, b_re, b_im, c_re, c_im, h0_re, h0_im, reverse):
    if reverse:
        u = jnp.flip(u, axis=1)
    dt = jnp.exp(log_dt)[:, None]
    mag = jnp.exp(lam_re * dt)
    ar, ai = mag * jnp.cos(lam_im * dt), mag * jnp.sin(lam_im * dt)
    inv = 1.0 / (lam_re * lam_re + lam_im * lam_im)
    fr, fi = cmul(ar - 1.0, ai, lam_re * inv, -lam_im * inv)
    bbr, bbi = cmul(fr[..., None], fi[..., None], b_re, b_im)
    xr = jnp.einsum('blgh,gph->blgp', u, bbr)
    xi = jnp.einsum('blgh,gph->blgp', u, bbi)
    h0r, h0i = cmul(ar, ai, h0_re, h0_im)
    xr = xr.at[:, 0].add(h0r)
    xi = xi.at[:, 0].add(h0i)
    a_r = jnp.broadcast_to(ar, xr.shape)
    a_i = jnp.broadcast_to(ai, xi.shape)

    def combine(e1, e2):
        a1r, a1i, b1r, b1i = e1
        a2r, a2i, b2r, b2i = e2
        pr, pi = cmul(a2r, a2i, a1r, a1i)
        qr, qi = cmul(a2r, a2i, b1r, b1i)
        return pr, pi, qr + b2r, qi + b2i

    _, _, sr, si = lax.associative_scan(combine, (a_r, a_i, xr, xi), axis=1)
    y = jnp.einsum('blgp,ghp->blgh', sr, c_re) - jnp.einsum('blgp,ghp->blgh', si, c_im)
    if reverse:
        y = jnp.flip(y, axis=1)
    return y, sr[:, -1], si[:, -1]


def ssm_mixer(u, h0, p):
    bsz, seq, _ = u.shape
    f32 = jnp.float32
    uf = u.astype(f32).reshape(bsz, seq, SSM_GROUPS, SSM_GROUP_CH)
    h0 = h0.astype(f32)
    y = p['ssm_d'].astype(f32).reshape(SSM_GROUPS, SSM_GROUP_CH) * uf
    finals = []
    for d in range(N_DIR):
        prm = [p[k][d].astype(f32) for k in ('ssm_lam_re', 'ssm_lam_im', 'ssm_log_dt', 'ssm_b_re',
                                            'ssm_b_im', 'ssm_c_re', 'ssm_c_im')]
        yd, fr, fi = s5_direction(uf, *prm, h0[:, d, 0], h0[:, d, 1], reverse=(d == 1))
        y = y + yd
        finals.append(jnp.stack([fr, fi], axis=1))
    y = jax.nn.gelu(y.reshape(bsz, seq, SSM_CH)).astype(u.dtype)
    y = y * jax.nn.sigmoid(y @ p['ssm_w_glu'] + p['ssm_b_glu'])
    return y, jnp.stack(finals, axis=1)


def short_conv(x, w):
    ch = x.shape[-1]
    k = w.shape[0]
    return lax.conv_general_dilated(x, w[:, None, :].astype(x.dtype), window_strides=(1,),
                                    padding=[(k // 2, k // 2)],
                                    dimension_numbers=('NWC', 'WIO', 'NWC'),
                                    feature_group_count=ch)


def gated_delta_chunked(q, k, v, g, beta, s0):
    bsz, seq, nh, _ = q.shape
    dv = v.shape[-1]
    n = seq // DN_CHUNK

    def chunks(t):
        t = t.reshape((bsz, n, DN_CHUNK, nh) + t.shape[3:])
        return jnp.moveaxis(t, 3, 1)

    q, k, v, g, beta = (chunks(t) for t in (q, k, v, g, beta))
    gc = jnp.cumsum(g, axis=-1)
    kb = k * beta[..., None]
    vb = v * beta[..., None]
    idx = jnp.arange(DN_CHUNK)
    lower = idx[:, None] >= idx[None, :]
    strict = idx[:, None] > idx[None, :]
    decay = jnp.exp(jnp.where(lower, gc[..., :, None] - gc[..., None, :], -jnp.inf))
    lmat = jnp.where(strict, jnp.einsum('bhnik,bhnjk->bhnij', kb, k) * decay, 0.0)
    solve = functools.partial(lax.linalg.triangular_solve, left_side=True, lower=True,
                              unit_diagonal=True)
    w_cum = solve(lmat, kb * jnp.exp(gc)[..., None])
    u_val = solve(lmat, vb)
    qk = jnp.einsum('bhnik,bhnjk->bhnij', q, k) * decay
    qg = q * jnp.exp(gc)[..., None]
    kt = k * jnp.exp(gc[..., -1:] - gc)[..., None]
    glast = jnp.exp(gc[..., -1])

    def step(s, xs):
        w_i, u_i, qg_i, qk_i, kt_i, gl_i = xs
        v_new = u_i - jnp.einsum('bhck,bhkv->bhcv', w_i, s)
        o_i = jnp.einsum('bhck,bhkv->bhcv', qg_i, s) + jnp.einsum('bhcj,bhjv->bhcv', qk_i, v_new)
        s = s * gl_i[..., None, None] + jnp.einsum('bhck,bhcv->bhkv', kt_i, v_new)
        return s, o_i

    xs = tuple(jnp.moveaxis(t, 2, 0) for t in (w_cum, u_val, qg, qk, kt, glast))
    s_fin, o = lax.scan(step, s0, xs)
    o = jnp.transpose(o, (1, 0, 3, 2, 4)).reshape(bsz, seq, nh, dv)
    return o, s_fin


def deltanet_mixer(qkv, z, a, b, s0, p):
    bsz, seq, _ = qkv.shape
    f32 = jnp.float32
    qkv = jax.nn.silu(short_conv(qkv, p['dn_conv'])).astype(f32)
    q, k, v = (t.reshape(bsz, seq, DN_HEADS, DN_HEAD_DIM) for t in jnp.split(qkv, 3, axis=-1))
    q = l2norm(q) * (DN_HEAD_DIM ** -0.5)
    k = l2norm(k)
    a = a.astype(f32).reshape(bsz, seq, N_DIR, DN_HEADS)
    b = b.astype(f32).reshape(bsz, seq, N_DIR, DN_HEADS)
    s0 = s0.astype(f32)
    o = jnp.zeros_like(v)
    finals = []
    for d in range(N_DIR):
        g = -jnp.exp(p['dn_a_log'][d].astype(f32)) * jax.nn.softplus(a[:, :, d] + p['dn_dt_bias'][d].astype(f32))
        beta = jax.nn.sigmoid(b[:, :, d])
        ins = (q, k, v, g, beta)
        if d == 1:
            ins = tuple(jnp.flip(t, axis=1) for t in ins)
        od, sd = gated_delta_chunked(*ins, s0[:, d])
        if d == 1:
            od = jnp.flip(od, axis=1)
        o = o + od
        finals.append(sd)
    zf = z.astype(f32).reshape(bsz, seq, DN_HEADS, DN_HEAD_DIM)
    o = rmsnorm(o, p['dn_norm'].astype(f32)) * jax.nn.silu(zf)
    return o.reshape(bsz, seq, DN_WIDTH).astype(z.dtype), jnp.stack(finals, axis=1)


def chunk_mlp_mixer(u, v, n_chunks, p):
    bsz, seq, _ = u.shape
    u = jax.nn.gelu(u)
    v = jax.nn.gelu(v).reshape(bsz, n_chunks, CM_CHUNK, CM_HEADS, CM_HEAD_DIM)
    v = rmsnorm(v, p['cm_norm'].reshape(CM_HEADS, CM_HEAD_DIM))
    vs = jnp.einsum('hij,bnjhc->bnihc', p['cm_w_s'], v) + p['cm_b_s'].T[:, :, None]
    return u * vs.reshape(bsz, seq, CM_WIDTH)


def moe(h, p):
    bsz, seq, dm = h.shape
    t = h.reshape(bsz * seq, dm)
    n_tok = t.shape[0]
    scores = jax.nn.sigmoid((t @ p['moe_router']).astype(jnp.float32))
    choice = scores + p['moe_router_bias'].astype(jnp.float32)
    per_group = N_EXPERTS // N_ROUTE_GROUPS
    grp = lax.top_k(choice.reshape(n_tok, N_ROUTE_GROUPS, per_group), 2)[0].sum(-1)
    _, gidx = lax.top_k(grp, TOPK_ROUTE_GROUPS)
    gmask = jnp.repeat(jax.nn.one_hot(gidx, N_ROUTE_GROUPS).sum(-2), per_group, axis=-1)
    _, eidx = lax.top_k(jnp.where(gmask > 0, choice, -jnp.inf), TOP_K)
    wts = jnp.take_along_axis(scores, eidx, axis=-1)
    wts = wts / (wts.sum(-1, keepdims=True) + 1e-20) * ROUTED_SCALE
    gates = jnp.sum(jax.nn.one_hot(eidx, N_EXPERTS) * wts[..., None], axis=-2).astype(h.dtype)
    w_gate, w_up, w_down = p['moe_w_gate'], p['moe_w_up'], p['moe_w_down']

    def block(args):
        xb, gb = args
        hg = jnp.einsum('td,edf->tef', xb, w_gate)
        hu = jnp.einsum('td,edf->tef', xb, w_up)
        return jnp.einsum('tef,efd->td', jax.nn.silu(hg) * hu * gb[..., None], w_down)

    routed = lax.map(block, (t.reshape(-1, MOE_BLOCK, dm), gates.reshape(-1, MOE_BLOCK, N_EXPERTS)))
    shared = (jax.nn.silu(t @ p['moe_ws_gate']) * (t @ p['moe_ws_up'])) @ p['moe_ws_down']
    return (routed.reshape(n_tok, dm) + shared).reshape(bsz, seq, dm)


def trunk_layer(x, mod, ssm_h0, dn_s0, n_chunks, p):
    shift1, scale1, gate1, shift2, scale2, gate2 = jnp.split(mod[:, None, :], N_MOD, axis=-1)
    h = rmsnorm(x, p['norm1']) * (1.0 + scale1) + shift1
    proj = h @ p['w_in']
    cuts = [SSM_CH, SSM_CH + 3 * DN_WIDTH, SSM_CH + 4 * DN_WIDTH,
            SSM_CH + 4 * DN_WIDTH + N_DIR * DN_HEADS, SSM_CH + 4 * DN_WIDTH + 2 * N_DIR * DN_HEADS,
            SSM_CH + 4 * DN_WIDTH + 2 * N_DIR * DN_HEADS + CM_WIDTH]
    s_u, qkv, z, a, b, c_u, c_v = jnp.split(proj, cuts, axis=-1)
    y_a, ssm_fin = ssm_mixer(s_u, ssm_h0, p)
    y_b, dn_fin = deltanet_mixer(qkv, z, a, b, dn_s0, p)
    y_c = chunk_mlp_mixer(c_u, c_v, n_chunks, p)
    x = x + gate1 * (jnp.concatenate([y_a, y_b, y_c], axis=-1) @ p['w_out'])
    h = rmsnorm(x, p['norm2']) * (1.0 + scale2) + shift2
    x = x + gate2 * moe(h, p)
    return x, ssm_fin, dn_fin


def setup_inputs(seed: int = 0) -> dict:
    key = jax.random.key(seed)
    ks = iter(jax.random.split(key, 48))
    f32 = jnp.float32

    def nrm(shape, scale):
        return jax.random.normal(next(ks), shape, f32) * scale

    G, P, Hg = SSM_GROUPS, SSM_STATE, SSM_GROUP_CH
    n_idx = jnp.arange(P, dtype=f32)
    x_prompt = nrm((BATCH, SEQ, D_MODEL), 1.0)
    x_sample = nrm((DEC_BATCH, DEC_SEQ, D_MODEL), 1.0)
    state_ssm = nrm((DEC_BATCH, DEPTH, N_DIR, 2, G, P), 0.1)
    state_delta = nrm((DEC_BATCH, DEPTH, N_DIR, DN_HEADS, DN_HEAD_DIM, DN_HEAD_DIM), 0.1)
    c = nrm((DEC_BATCH, D_MODEL), 1.0)
    c_ctx = nrm((D_MODEL,), 1.0)
    w_ada = nrm((DEPTH, D_MODEL, N_MOD * D_MODEL), 0.5 * D_MODEL ** -0.5)
    b_ada = nrm((DEPTH, N_MOD * D_MODEL), 0.02)
    norm1 = 1.0 + nrm((DEPTH, D_MODEL), 0.02)
    norm2 = 1.0 + nrm((DEPTH, D_MODEL), 0.02)
    w_in = nrm((DEPTH, D_MODEL, IN_COLS), D_MODEL ** -0.5)
    w_out = nrm((DEPTH, MIX_WIDTH, D_MODEL), MIX_WIDTH ** -0.5)
    ssm_lam_re = -0.5 + nrm((DEPTH, N_DIR, G, P), 0.01)
    ssm_lam_im = math.pi * n_idx + nrm((DEPTH, N_DIR, G, P), 0.01)
    ssm_log_dt = jax.random.uniform(next(ks), (DEPTH, N_DIR, G), f32, math.log(1e-3), math.log(1e-1))
    ssm_b_re = nrm((DEPTH, N_DIR, G, P, Hg), (2 * Hg) ** -0.5)
    ssm_b_im = nrm((DEPTH, N_DIR, G, P, Hg), (2 * Hg) ** -0.5)
    ssm_c_re = nrm((DEPTH, N_DIR, G, Hg, P), P ** -0.5)
    ssm_c_im = nrm((DEPTH, N_DIR, G, Hg, P), P ** -0.5)
    ssm_d = nrm((DEPTH, SSM_CH), 1.0)
    ssm_w_glu = nrm((DEPTH, SSM_CH, SSM_CH), SSM_CH ** -0.5)
    ssm_b_glu = nrm((DEPTH, SSM_CH), 0.02)
    dn_conv = nrm((DEPTH, DN_CONV, 3 * DN_WIDTH), DN_CONV ** -0.5)
    dn_a_log = jnp.log(jax.random.uniform(next(ks), (DEPTH, N_DIR, DN_HEADS), f32, 1.0, 16.0))
    dt0 = jnp.exp(jax.random.uniform(next(ks), (DEPTH, N_DIR, DN_HEADS), f32, math.log(1e-3), math.log(1e-1)))
    dn_dt_bias = dt0 + jnp.log(-jnp.expm1(-dt0))
    dn_norm = 1.0 + nrm((DEPTH, DN_HEAD_DIM), 0.02)
    cm_norm = 1.0 + nrm((DEPTH, CM_WIDTH), 0.02)
    cm_w_s = nrm((DEPTH, CM_HEADS, CM_CHUNK, CM_CHUNK), CM_CHUNK ** -0.5)
    cm_b_s = 1.0 + nrm((DEPTH, CM_HEADS, CM_CHUNK), 0.02)
    moe_router = nrm((DEPTH, D_MODEL, N_EXPERTS), D_MODEL ** -0.5)
    moe_router_bias = nrm((DEPTH, N_EXPERTS), 0.01)
    moe_w_gate = nrm((DEPTH, N_EXPERTS, D_MODEL, EXPERT_FF), D_MODEL ** -0.5)
    moe_w_up = nrm((DEPTH, N_EXPERTS, D_MODEL, EXPERT_FF), D_MODEL ** -0.5)
    moe_w_down = nrm((DEPTH, N_EXPERTS, EXPERT_FF, D_MODEL), EXPERT_FF ** -0.5)
    moe_ws_gate = nrm((DEPTH, D_MODEL, SHARED_FF), D_MODEL ** -0.5)
    moe_ws_up = nrm((DEPTH, D_MODEL, SHARED_FF), D_MODEL ** -0.5)
    moe_ws_down = nrm((DEPTH, SHARED_FF, D_MODEL), SHARED_FF ** -0.5)
    norm_f = 1.0 + nrm((D_MODEL,), 0.02)
    return {'x_prompt': x_prompt, 'x_sample': x_sample, 'state_ssm': state_ssm,
            'state_delta': state_delta, 'c': c, 'c_ctx': c_ctx, 'w_ada': w_ada, 'b_ada': b_ada,
            'norm1': norm1, 'norm2': norm2, 'w_in': w_in, 'w_out': w_out,
            'ssm_lam_re': ssm_lam_re, 'ssm_lam_im': ssm_lam_im, 'ssm_log_dt': ssm_log_dt,
            'ssm_b_re': ssm_b_re, 'ssm_b_im': ssm_b_im, 'ssm_c_re': ssm_c_re, 'ssm_c_im': ssm_c_im,
            'ssm_d': ssm_d, 'ssm_w_glu': ssm_w_glu, 'ssm_b_glu': ssm_b_glu, 'dn_conv': dn_conv,
            'dn_a_log': dn_a_log, 'dn_dt_bias': dn_dt_bias, 'dn_norm': dn_norm, 'cm_norm': cm_norm,
            'cm_w_s': cm_w_s, 'cm_b_s': cm_b_s, 'moe_router': moe_router,
            'moe_router_bias': moe_router_bias, 'moe_w_gate': moe_w_gate, 'moe_w_up': moe_w_up,
            'moe_w_down': moe_w_down, 'moe_ws_gate': moe_ws_gate, 'moe_ws_up': moe_ws_up,
            'moe_ws_down': moe_ws_down, 'norm_f': norm_f}


def reference(x_prompt, x_sample, state_ssm, state_delta, c, c_ctx, w_ada, b_ada, norm1, norm2,
              w_in, w_out, ssm_lam_re, ssm_lam_im, ssm_log_dt, ssm_b_re, ssm_b_im, ssm_c_re,
              ssm_c_im, ssm_d, ssm_w_glu, ssm_b_glu, dn_conv, dn_a_log, dn_dt_bias, dn_norm,
              cm_norm, cm_w_s, cm_b_s, moe_router, moe_router_bias, moe_w_gate, moe_w_up,
              moe_w_down, moe_ws_gate, moe_ws_up, moe_ws_down, norm_f):
    f32 = jnp.float32

    def layer_params(l):
        return {'norm1': norm1[l], 'norm2': norm2[l], 'w_in': w_in[l], 'w_out': w_out[l],
                'ssm_lam_re': ssm_lam_re[l], 'ssm_lam_im': ssm_lam_im[l], 'ssm_log_dt': ssm_log_dt[l],
                'ssm_b_re': ssm_b_re[l], 'ssm_b_im': ssm_b_im[l], 'ssm_c_re': ssm_c_re[l],
                'ssm_c_im': ssm_c_im[l], 'ssm_d': ssm_d[l], 'ssm_w_glu': ssm_w_glu[l],
                'ssm_b_glu': ssm_b_glu[l], 'dn_conv': dn_conv[l], 'dn_a_log': dn_a_log[l],
                'dn_dt_bias': dn_dt_bias[l], 'dn_norm': dn_norm[l], 'cm_norm': cm_norm[l],
                'cm_w_s': cm_w_s[l], 'cm_b_s': cm_b_s[l], 'moe_router': moe_router[l],
                'moe_router_bias': moe_router_bias[l], 'moe_w_gate': moe_w_gate[l],
                'moe_w_up': moe_w_up[l], 'moe_w_down': moe_w_down[l], 'moe_ws_gate': moe_ws_gate[l],
                'moe_ws_up': moe_ws_up[l], 'moe_ws_down': moe_ws_down[l]}

    bsz = x_prompt.shape[0]
    ctx_chunks = x_prompt.shape[1] // CM_CHUNK
    ssm_zero = jnp.zeros((bsz, N_DIR, 2, SSM_GROUPS, SSM_STATE), f32)
    dn_zero = jnp.zeros((bsz, N_DIR, DN_HEADS, DN_HEAD_DIM, DN_HEAD_DIM), f32)
    xp = x_prompt
    ssm_states, dn_states = [], []
    for l in range(DEPTH):
        mod_ctx = jax.nn.silu(c_ctx)[None, :] @ w_ada[l] + b_ada[l]
        xp, s_ssm, s_dn = trunk_layer(xp, mod_ctx, ssm_zero, dn_zero, ctx_chunks, layer_params(l))
        ssm_states.append(s_ssm)
        dn_states.append(s_dn)
    y_prompt = rmsnorm(xp, norm_f)

    rows = x_sample.shape[1] // GRID_W
    lat_chunks = rows // ROWS_PER_CHUNK
    xs = x_sample
    for l in range(DEPTH):
        mod_lat = jax.nn.silu(c) @ w_ada[l] + b_ada[l]
        xs, _, _ = trunk_layer(xs, mod_lat, state_ssm[:, l], state_delta[:, l], lat_chunks, layer_params(l))
    y_sample = rmsnorm(xs, norm_f)

    new_state_ssm = jnp.stack(ssm_states, axis=1).astype(x_prompt.dtype)
    new_state_delta = jnp.stack(dn_states, axis=1).astype(x_prompt.dtype)
    return (y_prompt, y_sample, new_state_ssm, new_state_delta)
```

```python
import functools
import math

import jax
import jax.numpy as jnp
from jax import lax
from jax.experimental import pallas as pl
from jax.experimental.pallas import tpu as pltpu

F32 = jnp.float32
BF16 = jnp.bfloat16
HIGHEST = lax.Precision.HIGHEST

D_MODEL = 1024
N_MOD = 6
EPS = 1e-6
SSM_CH = 256
SSM_GROUPS = 16
SSM_GROUP_CH = 16
SSM_STATE = 64
S5_T = 16
S5_W = S5_T * SSM_GROUP_CH
DN_HEADS = 4
DN_HEAD_DIM = 128
DN_WIDTH = 512
DN_CHUNK = 64
CM_HEADS = 4
CM_WIDTH = 256
CM_HEAD_DIM = 64
CM_CHUNK = 128
N_EXPERTS = 64
TOP_K = 6
N_ROUTE_GROUPS = 8
TOPK_ROUTE_GROUPS = 4
EXPERT_FF = 128
SHARED_FF = 256
ROUTED_SCALE = 2.5
MAIN_COLS = SSM_CH + 3 * DN_WIDTH + DN_WIDTH + 2 * CM_WIDTH
VMEM_LIMIT = 56 * 1024 * 1024


def _cparams(sem, vmem=None):
    return pltpu.CompilerParams(dimension_semantics=sem, vmem_limit_bytes=vmem)


def _dot(a, b):
    return jnp.dot(a, b, preferred_element_type=F32)


def _dot_nt(a, b, precision=None):
    return lax.dot_general(a, b, (((1,), (1,)), ((), ())), preferred_element_type=F32,
                           precision=precision)


def _dot_tn(a, b):
    return lax.dot_general(a, b, (((0,), (0,)), ((), ())), preferred_element_type=F32)


def _sigmoid(x):
    return 1.0 / (1.0 + jnp.exp(-x))


def _silu(x):
    return x * _sigmoid(x)


def _gelu(x):
    c = math.sqrt(2.0 / math.pi)
    return 0.5 * x * (1.0 + jnp.tanh(c * (x + 0.044715 * (x * x * x))))


def _softplus(x):
    return jnp.maximum(x, 0.0) + jnp.log(1.0 + jnp.exp(-jnp.abs(x)))


def _rms(x):
    return x * lax.rsqrt(jnp.mean(x * x, axis=-1, keepdims=True) + EPS)


def _mod_row_index(tile, n_ctx_tiles, tiles_per_lat_seq):
    return jnp.where(tile < n_ctx_tiles, 0, 1 + (tile - n_ctx_tiles) // tiles_per_lat_seq)


def _ada_kernel(c_ref, w_ref, b_ref, o_ref):
    h = _silu(c_ref[...])
    o_ref[0] = jnp.dot(h, w_ref[0], preferred_element_type=F32, precision=HIGHEST) + b_ref[0]


def _ada_mods(cvec, w_ada, b_ada):
    depth, d, n = w_ada.shape
    bn = 1536
    return pl.pallas_call(
        _ada_kernel,
        out_shape=jax.ShapeDtypeStruct((depth, 8, n), F32),
        grid=(depth, n // bn),
        in_specs=[pl.BlockSpec((8, d), lambda l, j: (0, 0)),
                  pl.BlockSpec((1, d, bn), lambda l, j: (l, 0, j)),
                  pl.BlockSpec((1, 1, bn), lambda l, j: (l, 0, j))],
        out_specs=pl.BlockSpec((1, 8, bn), lambda l, j: (l, 0, j)),
        compiler_params=_cparams(("parallel", "parallel")),
        name="ada_mods",
    )(cvec, w_ada, b_ada.reshape(depth, 1, n))


def _inproj_kernel(x_ref, mod_ref, g_ref, wm_ref, wab_ref,
                   su_ref, qkv_ref, z_ref, cu_ref, cv_ref, ab_ref):
    mod = mod_ref[0]
    shift = mod[:, 0:D_MODEL]
    scale = mod[:, D_MODEL:2 * D_MODEL]
    h = _rms(x_ref[...]) * g_ref[...] * (1.0 + scale) + shift
    hb = h.astype(BF16)
    p = _dot(hb, wm_ref[...])
    c0 = SSM_CH
    c1 = c0 + 3 * DN_WIDTH
    c2 = c1 + DN_WIDTH
    c3 = c2 + CM_WIDTH
    su_ref[...] = p[:, 0:c0]
    qkv_ref[...] = p[:, c0:c1]
    z_ref[...] = p[:, c1:c2]
    cu_ref[...] = p[:, c2:c3]
    cv_ref[...] = p[:, c3:MAIN_COLS]
    ab_ref[...] = _dot(hb, wab_ref[...])


def _in_projection(x, mods_l, norm1, w_main, w_ab, n_ctx_tok, lat_seq):
    n_tok = x.shape[0]
    tm = 512
    n_ctx_tiles = n_ctx_tok // tm
    tps = lat_seq // tm
    row = lambda i: (i, 0)
    const = lambda i: (0, 0)
    outs = [(SSM_CH, F32), (3 * DN_WIDTH, F32), (DN_WIDTH, F32), (CM_WIDTH, F32), (CM_WIDTH, F32),
            (128, F32)]
    return pl.pallas_call(
        _inproj_kernel,
        out_shape=[jax.ShapeDtypeStruct((n_tok, w), dt) for w, dt in outs],
        grid=(n_tok // tm,),
        in_specs=[pl.BlockSpec((tm, D_MODEL), row),
                  pl.BlockSpec((1, 1, N_MOD * D_MODEL),
                               lambda i: (_mod_row_index(i, n_ctx_tiles, tps), 0, 0)),
                  pl.BlockSpec((1, D_MODEL), const),
                  pl.BlockSpec((D_MODEL, MAIN_COLS), const),
                  pl.BlockSpec((D_MODEL, 128), const)],
        out_specs=[pl.BlockSpec((tm, w), row) for w, _ in outs],
        compiler_params=_cparams(("parallel",), VMEM_LIMIT),
        name="in_projection",
    )(x, mods_l, norm1.reshape(1, D_MODEL), w_main, w_ab)


def _s5_prep_kernel(lre_ref, lim_ref, ldt_ref, btre_ref, btim_ref, cre_ref, cim_ref,
                    mt_ref, pt_ref, qt_ref, at_ref):
    t_n, hg, w = S5_T, SSM_GROUP_CH, S5_W
    lre, lim = lre_ref[0], lim_ref[0]
    dt = jnp.exp(ldt_ref[0])
    xr, xi = lre * dt, lim * dt
    tau = lax.broadcasted_iota(jnp.int32, (t_n, 128), 0).astype(F32)
    mag = jnp.exp(tau * xr)
    a_re, a_im = mag * jnp.cos(tau * xi), mag * jnp.sin(tau * xi)
    ar, ai = jnp.exp(xr) * jnp.cos(xi), jnp.exp(xr) * jnp.sin(xi)
    inv = 1.0 / (lre * lre + lim * lim)
    br_, bi_ = lre * inv, -lim * inv
    fr = (ar - 1.0) * br_ - ai * bi_
    fi = (ar - 1.0) * bi_ + ai * br_
    btre, btim = btre_ref[0], btim_ref[0]
    bbr = fr * btre - fi * btim
    bbi = fr * btim + fi * btre
    cre, cim = cre_ref[0], cim_ref[0]

    lane = lax.broadcasted_iota(jnp.int32, (1, 128), 1)
    is_f = lane < SSM_STATE

    def rep_rows(a):
        return jnp.concatenate([jnp.broadcast_to(a[t:t + 1], (hg, 128)) for t in range(t_n)], axis=0)

    def tile_rows(c):
        return jnp.concatenate([c] * t_n, axis=0)

    def rev_rows(a):
        return jnp.concatenate([a[t_n - 1 - t:t_n - t] for t in range(t_n)], axis=0)

    cr_t, ci_t = tile_rows(cre), tile_rows(cim)
    a_re_rep, a_im_rep = rep_rows(a_re), rep_rows(a_im)
    w_re = cr_t * a_re_rep - ci_t * a_im_rep
    w_im = cr_t * a_im_rep + ci_t * a_re_rep
    zero = jnp.zeros_like(w_re)
    e_f = jnp.concatenate([jnp.where(is_f, w_re, zero), jnp.where(is_f, -w_im, zero)], axis=1)
    e_b = jnp.concatenate([jnp.where(is_f, zero, w_re), jnp.where(is_f, zero, -w_im)], axis=1)
    e_b_rev = jnp.concatenate([e_b[(t_n - 1 - t) * hg:(t_n - t) * hg] for t in range(t_n)], axis=0)
    bb_t = tile_rows(jnp.concatenate([bbr, bbi], axis=1))
    row_blk = lax.broadcasted_iota(jnp.int32, (w, w), 0) // hg

    m = jnp.zeros((w, w), F32)
    zrows = lambda n: jnp.zeros((n * hg, w), F32)
    for s in range(t_n):
        down = e_f if s == 0 else jnp.concatenate([zrows(s), e_f[:(t_n - s) * hg]], axis=0)
        k = t_n - 1 - s
        up = e_b_rev if k == 0 else jnp.concatenate([e_b_rev[k * hg:], zrows(k)], axis=0)
        rhs = jnp.where(row_blk == s, bb_t, 0.0)
        m = m + _dot_nt(down + up, rhs, precision=HIGHEST)
    mt_ref[0] = m.T.astype(BF16)

    a_re_rev, a_im_rev = rev_rows(a_re), rev_rows(a_im)
    ps_re = rep_rows(jnp.where(is_f, a_re_rev, a_re))
    ps_im = rep_rows(jnp.where(is_f, a_im_rev, a_im))
    bbr_t, bbi_t = tile_rows(bbr), tile_rows(bbi)
    p_re = ps_re * bbr_t - ps_im * bbi_t
    p_im = ps_re * bbi_t + ps_im * bbr_t
    pt_ref[0] = jnp.concatenate([p_re, p_im], axis=1).astype(BF16)

    a1_re = a_re * ar - a_im * ai
    a1_im = a_re * ai + a_im * ar
    aq_re = rep_rows(jnp.where(is_f, a1_re, rev_rows(a1_re)))
    aq_im = rep_rows(jnp.where(is_f, a1_im, rev_rows(a1_im)))
    q_re = cr_t * aq_re - ci_t * aq_im
    q_im = cr_t * aq_im + ci_t * aq_re
    qt_ref[0] = jnp.concatenate([q_re, -q_im], axis=1).T.astype(BF16)
    at_ref[0] = jnp.concatenate([a1_re[t_n - 1:t_n], a1_im[t_n - 1:t_n]], axis=1)


def _s5_prep(lam_re, lam_im, log_dt, b_re, b_im, c_re, c_im):
    g = SSM_GROUPS
    cat = lambda a: jnp.concatenate([a[0], a[1]], axis=-1)
    lre = cat(lam_re)[:, None, :]
    lim = cat(lam_im)[:, None, :]
    ldt = cat(jnp.broadcast_to(log_dt[..., None], (2, g, SSM_STATE)))[:, None, :]
    btre = cat(jnp.swapaxes(b_re, -1, -2))
    btim = cat(jnp.swapaxes(b_im, -1, -2))
    cre, cim = cat(c_re), cat(c_im)
    vec = pl.BlockSpec((1, 1, 128), lambda i: (i, 0, 0))
    mat = pl.BlockSpec((1, SSM_GROUP_CH, 128), lambda i: (i, 0, 0))
    big = pl.BlockSpec((1, S5_W, S5_W), lambda i: (i, 0, 0))
    return pl.pallas_call(
        _s5_prep_kernel,
        out_shape=[jax.ShapeDtypeStruct((g, S5_W, S5_W), BF16)] * 3
        + [jax.ShapeDtypeStruct((g, 1, 256), F32)],
        grid=(g,),
        in_specs=[vec, vec, vec, mat, mat, mat, mat],
        out_specs=[big, big, big, pl.BlockSpec((1, 1, 256), lambda i: (i, 0, 0))],
        compiler_params=_cparams(("parallel",)),
        name="s5_prep",
    )(lre, lim, ldt, btre, btim, cre, cim)


def _s5_z_kernel(u_ref, pt_ref, z_ref):
    z_ref[0, 0] = _dot(u_ref[0].astype(BF16), pt_ref[0])


def _s5_rec_kernel(z_ref, a_ref, h0_ref, hf_ref, hb_ref, fin_ref, *, n_ctx_blocks, rb, ctx_chunks):
    g = SSM_GROUPS
    j = pl.program_id(0)
    a = a_ref[...]
    a_re, a_im = a[:, :128], a[:, 128:]
    lane = lax.broadcasted_iota(jnp.int32, (g, 128), 1)
    is_f = lane < SSM_STATE

    def rows(c, part):
        return pl.ds(2 * c + part, g, stride=2 * rb)

    def run_seq(base, n, seq):
        def step(k, h):
            h_re, h_im = h
            cf = base + k
            cb = base + n - 1 - k
            hf_ref[0, rows(cf, 0), :] = h_re
            hf_ref[0, rows(cf, 1), :] = h_im
            hb_ref[0, rows(cb, 0), :] = h_re
            hb_ref[0, rows(cb, 1), :] = h_im
            z_re = jnp.where(is_f, z_ref[0, rows(cf, 0), :], z_ref[0, rows(cb, 0), :])
            z_im = jnp.where(is_f, z_ref[0, rows(cf, 1), :], z_ref[0, rows(cb, 1), :])
            return (a_re * h_re - a_im * h_im + z_re, a_re * h_im + a_im * h_re + z_im)
        h0 = h0_ref[seq]
        h_re, h_im = lax.fori_loop(0, n, step, (h0[:, :128], h0[:, 128:]))
        fin_ref[seq] = jnp.concatenate([h_re, h_im], axis=1)

    @pl.when(j < n_ctx_blocks)
    def _():
        seqs = rb // ctx_chunks
        def body(s, carry):
            run_seq(s * ctx_chunks, ctx_chunks, j * seqs + s)
            return carry
        lax.fori_loop(0, seqs, body, 0)

    @pl.when(j >= n_ctx_blocks)
    def _():
        run_seq(0, rb, n_ctx_blocks * (rb // ctx_chunks) + (j - n_ctx_blocks))


def _s5_y_kernel(u_ref, hf_ref, hb_ref, mt_ref, qt_ref, d_ref, y_ref):
    u = u_ref[0]
    lane = lax.broadcasted_iota(jnp.int32, (1, 256), 1)
    is_f = (lane % 128) < SSM_STATE
    h = jnp.where(is_f, hf_ref[0, 0], hb_ref[0, 0]).astype(BF16)
    y_ref[0] = _dot(u.astype(BF16), mt_ref[0]) + _dot(h, qt_ref[0]) + d_ref[0] * u


def _s5_mixer(s_u, mats, ssm_d, h0, n_ctx_seq, ctx_seq, lat_seq):
    mt, pt, qt, at = mats
    g, t_n, w = SSM_GROUPS, S5_T, S5_W
    n_tok = s_u.shape[0]
    nc = n_tok // t_n
    rb = lat_seq // t_n
    nj = nc // rb
    ctx_chunks = ctx_seq // t_n
    n_ctx_blocks = (n_ctx_seq * ctx_chunks) // rb
    n_seq = h0.shape[0]
    u = s_u.reshape(nc, t_n, g, SSM_GROUP_CH).transpose(2, 0, 1, 3).reshape(g, nc, w)

    u_spec = pl.BlockSpec((1, rb, w), lambda j, i: (i, j, 0))
    mat_spec = pl.BlockSpec((1, w, w), lambda j, i: (i, 0, 0))
    blk_spec = pl.BlockSpec((1, 1, rb, w), lambda j, i: (j, i, 0, 0))
    z = pl.pallas_call(
        _s5_z_kernel,
        out_shape=jax.ShapeDtypeStruct((nj, g, rb, w), F32),
        grid=(nj, g),
        in_specs=[u_spec, mat_spec],
        out_specs=blk_spec,
        compiler_params=_cparams(("parallel", "parallel")),
        name="s5_chunk_inputs",
    )(u, pt)

    flat = pl.BlockSpec((1, 2 * g * rb, 128), lambda j: (j, 0, 0))
    whole = lambda shape: pl.BlockSpec(shape, lambda j: (0,) * len(shape))
    hf, hb, fin = pl.pallas_call(
        functools.partial(_s5_rec_kernel, n_ctx_blocks=n_ctx_blocks, rb=rb, ctx_chunks=ctx_chunks),
        out_shape=[jax.ShapeDtypeStruct((nj, 2 * g * rb, 128), F32)] * 2
        + [jax.ShapeDtypeStruct((n_seq, g, w), F32)],
        grid=(nj,),
        in_specs=[flat, whole((g, w)), whole((n_seq, g, w))],
        out_specs=[flat, flat, whole((n_seq, g, w))],
        compiler_params=_cparams(("arbitrary",)),
        name="s5_recurrence",
    )(z.reshape(nj, 2 * g * rb, 128), at.reshape(g, w), h0)

    y = pl.pallas_call(
        _s5_y_kernel,
        out_shape=jax.ShapeDtypeStruct((g, nc, w), F32),
        grid=(nj, g),
        in_specs=[u_spec, blk_spec, blk_spec, mat_spec, mat_spec,
                  pl.BlockSpec((1, 1, w), lambda j, i: (i, 0, 0))],
        out_specs=u_spec,
        compiler_params=_cparams(("parallel", "parallel")),
        name="s5_outputs",
    )(u, hf.reshape(nj, g, rb, w), hb.reshape(nj, g, rb, w), mt, qt,
      jnp.tile(ssm_d.reshape(g, 1, SSM_GROUP_CH), (1, 1, t_n)))
    y_tok = y.reshape(g, nc, t_n, SSM_GROUP_CH).transpose(1, 2, 0, 3).reshape(n_tok, SSM_CH)
    return y_tok, fin


DN_TILE = 256


def _dn_pre_kernel(qkv_ref, halo_ref, w_ref, ab_ref, alog_ref, dtb_ref, qkvn_ref, gcb_ref):
    x = qkv_ref[...]
    n = x.shape[0]
    row = lax.broadcasted_iota(jnp.int32, (n, 1), 0)
    halo = halo_ref[0]
    xp = jnp.where(row == 0, halo[0:1], pltpu.roll(x, 1, 0))
    xn = jnp.where(row == n - 1, halo[1:2], pltpu.roll(x, n - 1, 0))
    w = w_ref[...]
    y = _silu(w[0:1] * xp + w[1:2] * x + w[2:3] * xn)
    for hcol in range(2 * DN_HEADS):
        seg = y[:, hcol * DN_HEAD_DIM:(hcol + 1) * DN_HEAD_DIM]
        nrm = seg * lax.rsqrt(jnp.sum(seg * seg, axis=-1, keepdims=True) + EPS)
        if hcol < DN_HEADS:
            nrm = nrm * (DN_HEAD_DIM ** -0.5)
        qkvn_ref[:, hcol * DN_HEAD_DIM:(hcol + 1) * DN_HEAD_DIM] = nrm.astype(BF16)
    qkvn_ref[:, 2 * DN_WIDTH:] = y[:, 2 * DN_WIDTH:].astype(BF16)

    ab = ab_ref[...]
    lane = lax.broadcasted_iota(jnp.int32, (1, 128), 1)
    glog = jnp.where(lane < 2 * DN_HEADS, -jnp.exp(alog_ref[...]) * _softplus(ab + dtb_ref[...]), 0.0)
    beta = _sigmoid(ab)
    ii = lax.broadcasted_iota(jnp.int32, (n, n), 0)
    jj = lax.broadcasted_iota(jnp.int32, (n, n), 1)
    same = (ii // DN_CHUNK) == (jj // DN_CHUNK)
    pre = jnp.where(same & (jj <= ii), 1.0, 0.0).astype(F32)
    suf = jnp.where(same & (jj >= ii), 1.0, 0.0).astype(F32)
    gpre = jnp.dot(pre, glog, preferred_element_type=F32, precision=HIGHEST)
    gsuf = jnp.dot(suf, glog, preferred_element_type=F32, precision=HIGHEST)
    gcb_ref[...] = jnp.where(lane < DN_HEADS, gpre, jnp.where(lane < 2 * DN_HEADS, gsuf, beta))


def _dn_pre(qkv, ab, conv_w, a_log, dt_bias, n_ctx_tok, lat_seq):
    n_tok = qkv.shape[0]
    tm = DN_TILE
    nt = n_tok // tm
    tiles = qkv.reshape(nt, tm, 3 * DN_WIDTH)
    first, last = tiles[:, 0], tiles[:, tm - 1]
    tile_id = jnp.arange(nt)
    n_ctx_tiles = n_ctx_tok // tm
    tps = lat_seq // tm
    lat_pos = (tile_id - n_ctx_tiles) % tps
    has_prev = (tile_id >= n_ctx_tiles) & (lat_pos > 0)
    has_next = (tile_id >= n_ctx_tiles) & (lat_pos < tps - 1)
    prev = jnp.where(has_prev[:, None], jnp.roll(last, 1, axis=0), 0.0)
    nxt = jnp.where(has_next[:, None], jnp.roll(first, -1, axis=0), 0.0)
    halo = jnp.concatenate([prev[:, None], nxt[:, None], jnp.zeros((nt, 6, 3 * DN_WIDTH), F32)], axis=1)
    w8 = jnp.concatenate([conv_w, jnp.zeros((5, 3 * DN_WIDTH), F32)], axis=0)
    pad = lambda v: jnp.concatenate([v.reshape(1, 2 * DN_HEADS), jnp.zeros((1, 120), F32)], axis=1)
    row = lambda i: (i, 0)
    const = lambda i: (0, 0)
    return pl.pallas_call(
        _dn_pre_kernel,
        out_shape=[jax.ShapeDtypeStruct((n_tok, 3 * DN_WIDTH), BF16),
                   jax.ShapeDtypeStruct((n_tok, 128), F32)],
        grid=(nt,),
        in_specs=[pl.BlockSpec((tm, 3 * DN_WIDTH), row),
                  pl.BlockSpec((1, 8, 3 * DN_WIDTH), lambda i: (i, 0, 0)),
                  pl.BlockSpec((8, 3 * DN_WIDTH), const),
                  pl.BlockSpec((tm, 128), row),
                  pl.BlockSpec((1, 128), const),
                  pl.BlockSpec((1, 128), const)],
        out_specs=[pl.BlockSpec((tm, 3 * DN_WIDTH), row), pl.BlockSpec((tm, 128), row)],
        compiler_params=_cparams(("parallel",), VMEM_LIMIT),
        name="dn_pre",
    )(qkv, halo, w8, ab, pad(a_log), pad(dt_bias))


def _tri_inverse(a):
    eye = (lax.broadcasted_iota(jnp.int32, a.shape, 0)
           == lax.broadcasted_iota(jnp.int32, a.shape, 1)).astype(F32)
    x = eye - a
    p = a
    n = a.shape[0]
    k = 1
    while 2 * k < n:
        p = _dot(p.astype(BF16), p.astype(BF16))
        x = x + _dot(x.astype(BF16), p.astype(BF16))
        k *= 2
    return x - eye


def _dn_chunk(q, k, v, gc_col, gc_row, beta, s, causal):
    c = q.shape[0]
    ii = lax.broadcasted_iota(jnp.int32, (c, c), 0)
    jj = lax.broadcasted_iota(jnp.int32, (c, c), 1)
    incl = (ii >= jj) if causal else (ii <= jj)
    strict = (ii > jj) if causal else (ii < jj)
    kf = k.astype(F32)
    kb = kf * beta
    vb = v.astype(F32) * beta
    decay = jnp.exp(jnp.where(incl, gc_col - gc_row, -jnp.inf))
    amat = jnp.where(strict, _dot_nt(kb.astype(BF16), k) * decay, 0.0)
    tinv = _tri_inverse(amat).astype(BF16)
    egc = jnp.exp(gc_col)
    rhs = jnp.concatenate([kb * egc, vb], axis=1)
    wu = rhs + _dot(tinv, rhs.astype(BF16))
    w_c, u_c = wu[:, :DN_HEAD_DIM], wu[:, DN_HEAD_DIM:]
    qk = jnp.where(incl, _dot_nt(q, k) * decay, 0.0)
    qg = q.astype(F32) * egc
    g_last = gc_col[c - 1:c] if causal else gc_col[0:1]
    kt = kf * jnp.exp(g_last - gc_col)
    sb = s.astype(BF16)
    v_new = u_c - _dot(w_c.astype(BF16), sb)
    vnb = v_new.astype(BF16)
    o = _dot(qg.astype(BF16), sb) + _dot(qk.astype(BF16), vnb)
    s_new = s * jnp.exp(g_last) + _dot_tn(kt.astype(BF16), vnb)
    return o, s_new


def _dn_scan_kernel(*refs, n_chunks, hpb, has_s0, has_prev_o, write_fin):
    refs = list(refs)
    q_ref, k_ref, v_ref, gc_ref, gt_ref = refs[:5]
    pos = 5
    s0_ref = None
    if has_s0:
        s0_ref = refs[pos]
        pos += 1
    if has_prev_o:
        pos += 1
    o_ref = refs[pos]
    pos += 1
    fin_ref = None
    if write_fin:
        fin_ref = refs[pos]
        pos += 1
    s_ref = refs[pos]
    hblk = pl.program_id(1)

    if has_s0:
        s_ref[...] = s0_ref[0]
    else:
        s_ref[...] = jnp.zeros_like(s_ref)
    o_ref[...] = jnp.zeros_like(o_ref)

    def body(kk, carry):
        for d in range(2):
            c = kk if d == 0 else n_chunks - 1 - kk
            r0 = pl.multiple_of(c * DN_CHUNK, DN_CHUNK)
            rows = pl.ds(r0, DN_CHUNK)
            gcol = gc_ref[rows, :]
            grow = gt_ref[c]
            for h in range(hpb):
                cols = slice(h * DN_HEAD_DIM, (h + 1) * DN_HEAD_DIM)
                head_lane = d * DN_HEADS + hblk * hpb + h
                lane = lax.broadcasted_iota(jnp.int32, (1, 128), 1)
                gc_col = jnp.sum(jnp.where(lane == head_lane, gcol, 0.0), axis=1, keepdims=True)
                beta = jnp.sum(jnp.where(lane == 2 * DN_HEADS + head_lane, gcol, 0.0), axis=1,
                               keepdims=True)
                sub = lax.broadcasted_iota(jnp.int32, (16, 1), 0)
                gc_row = jnp.sum(jnp.where(sub == head_lane, grow, 0.0), axis=0, keepdims=True)
                o, s_new = _dn_chunk(q_ref[rows, cols], k_ref[rows, cols], v_ref[rows, cols],
                                     gc_col, gc_row, beta, s_ref[d, h], causal=(d == 0))
                s_ref[d, h] = s_new
                o_ref[rows, cols] = o_ref[rows, cols] + o
        return carry

    lax.fori_loop(0, n_chunks, body, 0)
    if write_fin:
        fin_ref[0] = s_ref[...]


def _dn_scan(qkvn, gcb, gcb_t, seq_len, n_seq, first_tok, s0, prev_o, write_fin, hpb):
    n_tok = qkvn.shape[0]
    n_chunks = seq_len // DN_CHUNK
    nhb = DN_HEADS // hpb
    b0 = first_tok // seq_len
    cw = hpb * DN_HEAD_DIM
    kb0 = DN_WIDTH // cw
    in_specs = [pl.BlockSpec((seq_len, cw), lambda s, j: (b0 + s, j)),
                pl.BlockSpec((seq_len, cw), lambda s, j: (b0 + s, kb0 + j)),
                pl.BlockSpec((seq_len, cw), lambda s, j: (b0 + s, 2 * kb0 + j)),
                pl.BlockSpec((seq_len, 128), lambda s, j: (b0 + s, 0)),
                pl.BlockSpec((n_chunks, 16, DN_CHUNK), lambda s, j: (b0 + s, 0, 0))]
    args = [qkvn, qkvn, qkvn, gcb, gcb_t]
    aliases = {}
    if s0 is not None:
        in_specs.append(pl.BlockSpec((1, 2, hpb, DN_HEAD_DIM, DN_HEAD_DIM), lambda s, j: (s, 0, j, 0, 0)))
        args.append(s0)
    if prev_o is not None:
        in_specs.append(pl.BlockSpec(memory_space=pl.ANY))
        aliases = {len(args): 0}
        args.append(prev_o)
    out_shape = [jax.ShapeDtypeStruct((n_tok, DN_WIDTH), F32)]
    out_specs = [pl.BlockSpec((seq_len, cw), lambda s, j: (b0 + s, j))]
    if write_fin:
        out_shape.append(jax.ShapeDtypeStruct((n_seq, 2, DN_HEADS, DN_HEAD_DIM, DN_HEAD_DIM), F32))
        out_specs.append(pl.BlockSpec((1, 2, hpb, DN_HEAD_DIM, DN_HEAD_DIM), lambda s, j: (s, 0, j, 0, 0)))
    res = pl.pallas_call(
        functools.partial(_dn_scan_kernel, n_chunks=n_chunks, hpb=hpb, has_s0=s0 is not None,
                          has_prev_o=prev_o is not None, write_fin=write_fin),
        out_shape=out_shape,
        grid=(n_seq, nhb),
        in_specs=in_specs,
        out_specs=out_specs,
        scratch_shapes=[pltpu.VMEM((2, hpb, DN_HEAD_DIM, DN_HEAD_DIM), F32)],
        input_output_aliases=aliases,
        compiler_params=_cparams(("parallel", "parallel"), VMEM_LIMIT),
        name="dn_scan_%d" % seq_len,
    )(*args)
    return res


def _deltanet(qkv, ab, p, s0_lat, n_ctx_seq, ctx_seq, n_lat_seq, lat_seq):
    n_ctx_tok = n_ctx_seq * ctx_seq
    qkvn, gcb = _dn_pre(qkv, ab, p['dn_conv'], p['dn_a_log'], p['dn_dt_bias'], n_ctx_tok, lat_seq)
    n_tok = qkv.shape[0]
    gcb_t = gcb[:, :16].reshape(n_tok // DN_CHUNK, DN_CHUNK, 16).transpose(0, 2, 1)
    o_ctx, fin = _dn_scan(qkvn, gcb, gcb_t, ctx_seq, n_ctx_seq, 0, None, None, True, hpb=4)
    (o_all,) = _dn_scan(qkvn, gcb, gcb_t, lat_seq, n_lat_seq, n_ctx_tok, s0_lat, o_ctx, False, hpb=2)
    return o_all, fin


MIX_TILE = 256


def _mix_kernel(x_ref, mod_ref, ys_ref, o_ref, z_ref, cu_ref, cv_ref,
                wglu_ref, bglu_ref, dnn_ref, cmn_ref, ws_ref, bs_ref, wo_ref, n2_ref,
                x1_ref, h2_ref):
    mod = mod_ref[0]
    gate1 = mod[:, 2 * D_MODEL:3 * D_MODEL]
    shift2 = mod[:, 3 * D_MODEL:4 * D_MODEL]
    scale2 = mod[:, 4 * D_MODEL:5 * D_MODEL]
    ya = _gelu(ys_ref[...])
    ya = ya * _sigmoid(_dot(ya.astype(BF16), wglu_ref[...]) + bglu_ref[...])
    mix = _dot(ya.astype(BF16), wo_ref[0:SSM_CH, :])
    o = o_ref[...]
    z = z_ref[...]
    parts = []
    for h in range(DN_HEADS):
        sl = slice(h * DN_HEAD_DIM, (h + 1) * DN_HEAD_DIM)
        parts.append(_rms(o[:, sl]) * dnn_ref[...] * _silu(z[:, sl]))
    yb = jnp.concatenate(parts, axis=1)
    mix = mix + _dot(yb.astype(BF16), wo_ref[SSM_CH:SSM_CH + DN_WIDTH, :])
    u = _gelu(cu_ref[...])
    v = _gelu(cv_ref[...])
    seg_i = lax.broadcasted_iota(jnp.int32, (CM_WIDTH, CM_WIDTH), 0) // CM_HEAD_DIM
    seg_j = lax.broadcasted_iota(jnp.int32, (CM_WIDTH, CM_WIDTH), 1) // CM_HEAD_DIM
    seg_ones = jnp.where(seg_i == seg_j, 1.0, 0.0).astype(BF16)
    vv = v * v
    vv_hi = vv.astype(BF16)
    vv_lo = (vv - vv_hi.astype(F32)).astype(BF16)
    ms = (_dot(vv_hi, seg_ones) + _dot(vv_lo, seg_ones)) * (1.0 / CM_HEAD_DIM)
    vn = (v * lax.rsqrt(ms + EPS) * cmn_ref[...]).astype(BF16)
    head_of_lane = lax.broadcasted_iota(jnp.int32, (1, CM_WIDTH), 1) // CM_HEAD_DIM
    ycs = []
    for c in range(MIX_TILE // CM_CHUNK):
        vc = vn[c * CM_CHUNK:(c + 1) * CM_CHUNK]
        stack = jnp.concatenate([jnp.where(head_of_lane == h, vc, jnp.zeros_like(vc))
                                 for h in range(CM_HEADS)], axis=0)
        vs = _dot(ws_ref[...], stack) + bs_ref[...]
        ycs.append(u[c * CM_CHUNK:(c + 1) * CM_CHUNK] * vs)
    yc = jnp.concatenate(ycs, axis=0)
    mix = mix + _dot(yc.astype(BF16), wo_ref[SSM_CH + DN_WIDTH:, :])
    x1 = x_ref[...] + gate1 * mix
    x1_ref[...] = x1
    h2_ref[...] = (_rms(x1) * n2_ref[...] * (1.0 + scale2) + shift2).astype(BF16)


def _mix_out(x, mods_l, ys, o, z, cu, cv, p, n_ctx_tok, lat_seq):
    n_tok = x.shape[0]
    tm = MIX_TILE
    n_ctx_tiles = n_ctx_tok // tm
    tps = lat_seq // tm
    row = lambda i: (i, 0)
    const = lambda i: (0, 0)
    ws_cat = jnp.concatenate([p['cm_w_s'][h] for h in range(CM_HEADS)], axis=1).astype(BF16)
    bs_exp = jnp.repeat(p['cm_b_s'].T, CM_HEAD_DIM, axis=1)
    return pl.pallas_call(
        _mix_kernel,
        out_shape=[jax.ShapeDtypeStruct((n_tok, D_MODEL), F32),
                   jax.ShapeDtypeStruct((n_tok, D_MODEL), BF16)],
        grid=(n_tok // tm,),
        in_specs=[pl.BlockSpec((tm, D_MODEL), row),
                  pl.BlockSpec((1, 1, N_MOD * D_MODEL),
                               lambda i: (_mod_row_index(i, n_ctx_tiles, tps), 0, 0)),
                  pl.BlockSpec((tm, SSM_CH), row),
                  pl.BlockSpec((tm, DN_WIDTH), row),
                  pl.BlockSpec((tm, DN_WIDTH), row),
                  pl.BlockSpec((tm, CM_WIDTH), row),
                  pl.BlockSpec((tm, CM_WIDTH), row),
                  pl.BlockSpec((SSM_CH, SSM_CH), const),
                  pl.BlockSpec((1, SSM_CH), const),
                  pl.BlockSpec((1, DN_HEAD_DIM), const),
                  pl.BlockSpec((1, CM_WIDTH), const),
                  pl.BlockSpec((CM_CHUNK, CM_HEADS * CM_CHUNK), const),
                  pl.BlockSpec((CM_CHUNK, CM_WIDTH), const),
                  pl.BlockSpec((D_MODEL, D_MODEL), const),
                  pl.BlockSpec((1, D_MODEL), const)],
        out_specs=[pl.BlockSpec((tm, D_MODEL), row), pl.BlockSpec((tm, D_MODEL), row)],
        compiler_params=_cparams(("parallel",), VMEM_LIMIT),
        name="mix_out",
    )(x, mods_l, ys, o, z, cu, cv,
      p['ssm_w_glu'].astype(BF16), p['ssm_b_glu'].reshape(1, SSM_CH),
      p['dn_norm'].reshape(1, DN_HEAD_DIM), p['cm_norm'].reshape(1, CM_WIDTH),
      ws_cat, bs_exp, p['w_out'].astype(BF16), p['norm2'].reshape(1, D_MODEL))


ROUTE_TILE = 512


def _first_max_mask(x, idx, axis):
    m = jnp.max(x, axis=axis, keepdims=True)
    first = jnp.min(jnp.where(x == m, idx, jnp.int32(1 << 20)), axis=axis, keepdims=True)
    return idx == first


def _router_kernel(h_ref, wr_ref, bias_ref, g_ref):
    n = h_ref.shape[0]
    per = N_EXPERTS // N_ROUTE_GROUPS
    logits = _dot_nt(wr_ref[...], h_ref[...])
    scores = _sigmoid(logits)
    choice = scores + bias_ref[...]
    ninf = jnp.float32(-jnp.inf)
    c3 = choice.reshape(N_ROUTE_GROUPS, per, n)
    i3 = lax.broadcasted_iota(jnp.int32, c3.shape, 1)
    top1 = _first_max_mask(c3, i3, 1)
    m1 = jnp.max(c3, axis=1)
    m2 = jnp.max(jnp.where(top1, ninf, c3), axis=1)
    grp = m1 + m2
    gi = lax.broadcasted_iota(jnp.int32, grp.shape, 0)
    gsel = jnp.zeros(grp.shape, jnp.bool_)
    work = grp
    for _ in range(TOPK_ROUTE_GROUPS):
        pick = _first_max_mask(work, gi, 0)
        gsel = gsel | pick
        work = jnp.where(pick, ninf, work)
    gmask = jnp.broadcast_to(gsel[:, None, :], c3.shape).reshape(N_EXPERTS, n)
    work = jnp.where(gmask, choice, ninf)
    ei = lax.broadcasted_iota(jnp.int32, work.shape, 0)
    esel = jnp.zeros(work.shape, jnp.bool_)
    for _ in range(TOP_K):
        pick = _first_max_mask(work, ei, 0)
        esel = esel | pick
        work = jnp.where(pick, ninf, work)
    wts = jnp.where(esel, scores, 0.0)
    g_ref[...] = wts / (jnp.sum(wts, axis=0, keepdims=True) + 1e-20) * ROUTED_SCALE


def _router(h2, router_w, router_bias):
    n_tok = h2.shape[0]
    tm = ROUTE_TILE
    return pl.pallas_call(
        _router_kernel,
        out_shape=jax.ShapeDtypeStruct((N_EXPERTS, n_tok), F32),
        grid=(n_tok // tm,),
        in_specs=[pl.BlockSpec((tm, D_MODEL), lambda i: (i, 0)),
                  pl.BlockSpec((N_EXPERTS, D_MODEL), lambda i: (0, 0)),
                  pl.BlockSpec((N_EXPERTS, 1), lambda i: (0, 0))],
        out_specs=pl.BlockSpec((N_EXPERTS, tm), lambda i: (0, i)),
        compiler_params=_cparams(("parallel",)),
        name="router",
    )(h2, router_w.T.astype(BF16), router_bias.reshape(N_EXPERTS, 1))


MOE_TILE = 1024
MOE_EG = 8


def _moe_kernel(h_ref, x_ref, mod_ref, g_ref, wg_ref, wu_ref, wd_ref, sg_ref, su_ref, sd_ref, nf_ref,
                *out_refs, final):
    j = pl.program_id(1)
    nj = pl.num_programs(1)
    acc_ref = out_refs[-1]
    h = h_ref[...]

    @pl.when(j == 0)
    def _():
        sh = _silu(_dot(h, sg_ref[...])) * _dot(h, su_ref[...])
        acc_ref[...] = _dot(sh.astype(BF16), sd_ref[...])

    hg = _dot(h, wg_ref[...])
    hu = _dot(h, wu_ref[...])
    gates = g_ref[0]
    cols = []
    for e in range(MOE_EG):
        sl = slice(e * EXPERT_FF, (e + 1) * EXPERT_FF)
        cols.append((_silu(hg[:, sl]) * hu[:, sl] * gates[:, e:e + 1]).astype(BF16))
    act = jnp.concatenate(cols, axis=1)
    acc_ref[...] += _dot(act, wd_ref[...])

    @pl.when(j == nj - 1)
    def _():
        gate2 = mod_ref[0][:, 5 * D_MODEL:6 * D_MODEL]
        x2 = x_ref[...] + gate2 * acc_ref[...]
        if final:
            out_refs[0][...] = _rms(x2) * nf_ref[...]
        else:
            out_refs[0][...] = x2


def _moe(h2, x1, mods_l, gates_t, p, norm_f, final, n_ctx_tok, lat_seq):
    n_tok = h2.shape[0]
    tm = MOE_TILE
    n_ctx_tiles = n_ctx_tok // tm
    tps = lat_seq // tm
    neg = N_EXPERTS // MOE_EG
    ef = MOE_EG * EXPERT_FF
    gates = gates_t.reshape(neg, MOE_EG, n_tok).transpose(0, 2, 1)
    wg = p['moe_w_gate'].transpose(1, 0, 2).reshape(D_MODEL, N_EXPERTS * EXPERT_FF).astype(BF16)
    wu = p['moe_w_up'].transpose(1, 0, 2).reshape(D_MODEL, N_EXPERTS * EXPERT_FF).astype(BF16)
    wd = p['moe_w_down'].reshape(N_EXPERTS * EXPERT_FF, D_MODEL).astype(BF16)
    row = lambda i, j: (i, 0)
    const = lambda i, j: (0, 0)
    out = pl.pallas_call(
        functools.partial(_moe_kernel, final=final),
        out_shape=jax.ShapeDtypeStruct((n_tok, D_MODEL), F32),
        grid=(n_tok // tm, neg),
        in_specs=[pl.BlockSpec((tm, D_MODEL), row),
                  pl.BlockSpec((tm, D_MODEL), row),
                  pl.BlockSpec((1, 1, N_MOD * D_MODEL),
                               lambda i, j: (_mod_row_index(i, n_ctx_tiles, tps), 0, 0)),
                  pl.BlockSpec((1, tm, MOE_EG), lambda i, j: (j, i, 0)),
                  pl.BlockSpec((D_MODEL, ef), lambda i, j: (0, j)),
                  pl.BlockSpec((D_MODEL, ef), lambda i, j: (0, j)),
                  pl.BlockSpec((ef, D_MODEL), lambda i, j: (j, 0)),
                  pl.BlockSpec((D_MODEL, SHARED_FF), const),
                  pl.BlockSpec((D_MODEL, SHARED_FF), const),
                  pl.BlockSpec((SHARED_FF, D_MODEL), const),
                  pl.BlockSpec((1, D_MODEL), const)],
        out_specs=pl.BlockSpec((tm, D_MODEL), row),
        scratch_shapes=[pltpu.VMEM((tm, D_MODEL), F32)],
        compiler_params=_cparams(("parallel", "arbitrary"), VMEM_LIMIT),
        name="moe",
    )(h2, x1, mods_l, gates, wg, wu, wd,
      p['moe_ws_gate'].astype(BF16), p['moe_ws_up'].astype(BF16), p['moe_ws_down'].astype(BF16),
      norm_f.reshape(1, D_MODEL))
    return out


def kernel(x_prompt, x_sample, state_ssm, state_delta, c, c_ctx, w_ada, b_ada, norm1, norm2, w_in, w_out, ssm_lam_re, ssm_lam_im, ssm_log_dt, ssm_b_re, ssm_b_im, ssm_c_re, ssm_c_im, ssm_d, ssm_w_glu, ssm_b_glu, dn_conv, dn_a_log, dn_dt_bias, dn_norm, cm_norm, cm_w_s, cm_b_s, moe_router, moe_router_bias, moe_w_gate, moe_w_up, moe_w_down, moe_ws_gate, moe_ws_up, moe_ws_down, norm_f):
    n_ctx_seq, ctx_seq, _ = x_prompt.shape
    n_lat_seq, lat_seq, _ = x_sample.shape
    depth = w_ada.shape[0]
    n_ctx_tok = n_ctx_seq * ctx_seq
    n_seq = n_ctx_seq + n_lat_seq
    g = SSM_GROUPS

    x = jnp.concatenate([x_prompt.reshape(n_ctx_tok, D_MODEL),
                         x_sample.reshape(n_lat_seq * lat_seq, D_MODEL)], axis=0)
    cvec = jnp.concatenate([c_ctx[None, :], c, jnp.zeros((8 - 1 - n_lat_seq, D_MODEL), F32)], axis=0)
    mods = _ada_mods(cvec, w_ada, b_ada)

    ssm_states, dn_states = [], []
    y = None
    for l in range(depth):
        p = {'w_out': w_out[l], 'norm2': norm2[l], 'ssm_w_glu': ssm_w_glu[l], 'ssm_b_glu': ssm_b_glu[l],
             'dn_conv': dn_conv[l], 'dn_a_log': dn_a_log[l], 'dn_dt_bias': dn_dt_bias[l],
             'dn_norm': dn_norm[l], 'cm_norm': cm_norm[l], 'cm_w_s': cm_w_s[l], 'cm_b_s': cm_b_s[l],
             'moe_w_gate': moe_w_gate[l], 'moe_w_up': moe_w_up[l], 'moe_w_down': moe_w_down[l],
             'moe_ws_gate': moe_ws_gate[l], 'moe_ws_up': moe_ws_up[l], 'moe_ws_down': moe_ws_down[l]}
        mods_l = mods[l].reshape(8, 1, N_MOD * D_MODEL)
        wi = w_in[l]
        c_ab = SSM_CH + 4 * DN_WIDTH
        w_main = jnp.concatenate([wi[:, :c_ab], wi[:, c_ab + 16:]], axis=1).astype(BF16)
        w_ab = jnp.concatenate([wi[:, c_ab:c_ab + 16], jnp.zeros((D_MODEL, 112), F32)], axis=1).astype(BF16)
        s_u, qkv, z, cu, cv, ab = _in_projection(x, mods_l, norm1[l], w_main, w_ab, n_ctx_tok, lat_seq)

        mats = _s5_prep(ssm_lam_re[l], ssm_lam_im[l], ssm_log_dt[l], ssm_b_re[l], ssm_b_im[l],
                        ssm_c_re[l], ssm_c_im[l])
        h0_lat = state_ssm[:, l].transpose(0, 3, 2, 1, 4).reshape(n_lat_seq, g, 256)
        h0 = jnp.concatenate([jnp.zeros((n_ctx_seq, g, 256), F32), h0_lat], axis=0)
        ys, ssm_fin = _s5_mixer(s_u, mats, ssm_d[l], h0, n_ctx_seq, ctx_seq, lat_seq)
        ssm_states.append(ssm_fin[:n_ctx_seq].reshape(n_ctx_seq, g, 2, 2, SSM_STATE).transpose(0, 3, 2, 1, 4))

        o_dn, dn_fin = _deltanet(qkv, ab, p, state_delta[:, l], n_ctx_seq, ctx_seq, n_lat_seq, lat_seq)
        dn_states.append(dn_fin)

        x1, h2 = _mix_out(x, mods_l, ys, o_dn, z, cu, cv, p, n_ctx_tok, lat_seq)
        gates_t = _router(h2, moe_router[l], moe_router_bias[l])
        x = _moe(h2, x1, mods_l, gates_t, p, norm_f, l == depth - 1, n_ctx_tok, lat_seq)

    y_prompt = x[:n_ctx_tok].reshape(x_prompt.shape)
    y_sample = x[n_ctx_tok:].reshape(x_sample.shape)
    new_state_ssm = jnp.stack(ssm_states, axis=1)
    new_state_delta = jnp.stack(dn_states, axis=1)
    return (y_prompt, y_sample, new_state_ssm, new_state_delta)
```

```python
import functools
import math

import jax
import jax.numpy as jnp
from jax import lax
from jax.experimental import pallas as pl
from jax.experimental.pallas import tpu as pltpu

F32 = jnp.float32
BF16 = jnp.bfloat16
HIGHEST = lax.Precision.HIGHEST

D_MODEL = 1024
N_MOD = 6
EPS = 1e-6
SSM_CH = 256
SSM_GROUPS = 16
SSM_GROUP_CH = 16
SSM_STATE = 64
S5_T = 16
S5_W = S5_T * SSM_GROUP_CH
DN_HEADS = 4
DN_HEAD_DIM = 128
DN_WIDTH = 512
DN_CHUNK = 64
CM_HEADS = 4
CM_WIDTH = 256
CM_HEAD_DIM = 64
CM_CHUNK = 128
N_EXPERTS = 64
TOP_K = 6
N_ROUTE_GROUPS = 8
TOPK_ROUTE_GROUPS = 4
EXPERT_FF = 128
SHARED_FF = 256
ROUTED_SCALE = 2.5
MAIN_COLS = SSM_CH + 3 * DN_WIDTH + DN_WIDTH + 2 * CM_WIDTH
VMEM_LIMIT = 56 * 1024 * 1024


def _cparams(sem, vmem=None):
    return pltpu.CompilerParams(dimension_semantics=sem, vmem_limit_bytes=vmem)


def _dot(a, b):
    return jnp.dot(a, b, preferred_element_type=F32)


def _dot_nt(a, b, precision=None):
    return lax.dot_general(a, b, (((1,), (1,)), ((), ())), preferred_element_type=F32,
                           precision=precision)


def _dot_tn(a, b):
    return lax.dot_general(a, b, (((0,), (0,)), ((), ())), preferred_element_type=F32)


def _sigmoid(x):
    return 1.0 / (1.0 + jnp.exp(-x))


def _silu(x):
    return x * _sigmoid(x)


def _gelu(x):
    c = math.sqrt(2.0 / math.pi)
    return 0.5 * x * (1.0 + jnp.tanh(c * (x + 0.044715 * (x * x * x))))


def _softplus(x):
    return jnp.maximum(x, 0.0) + jnp.log(1.0 + jnp.exp(-jnp.abs(x)))


def _rms(x):
    return x * lax.rsqrt(jnp.mean(x * x, axis=-1, keepdims=True) + EPS)


def _mod_row_index(tile, n_ctx_tiles, tiles_per_lat_seq):
    return jnp.where(tile < n_ctx_tiles, 0, 1 + (tile - n_ctx_tiles) // tiles_per_lat_seq)


def _ada_kernel(c_ref, w_ref, b_ref, o_ref):
    h = _silu(c_ref[...])
    o_ref[0] = jnp.dot(h, w_ref[0], preferred_element_type=F32, precision=HIGHEST) + b_ref[0]


def _ada_mods(cvec, w_ada, b_ada):
    depth, d, n = w_ada.shape
    bn = 1536
    return pl.pallas_call(
        _ada_kernel,
        out_shape=jax.ShapeDtypeStruct((depth, 8, n), F32),
        grid=(depth, n // bn),
        in_specs=[pl.BlockSpec((8, d), lambda l, j: (0, 0)),
                  pl.BlockSpec((1, d, bn), lambda l, j: (l, 0, j)),
                  pl.BlockSpec((1, 1, bn), lambda l, j: (l, 0, j))],
        out_specs=pl.BlockSpec((1, 8, bn), lambda l, j: (l, 0, j)),
        compiler_params=_cparams(("parallel", "parallel")),
        name="ada_mods",
    )(cvec, w_ada, b_ada.reshape(depth, 1, n))


def _inproj_kernel(x_ref, mod_ref, g_ref, wm_ref, wab_ref,
                   su_ref, qkv_ref, z_ref, cu_ref, cv_ref, ab_ref):
    mod = mod_ref[0]
    shift = mod[:, 0:D_MODEL]
    scale = mod[:, D_MODEL:2 * D_MODEL]
    h = _rms(x_ref[...]) * g_ref[...] * (1.0 + scale) + shift
    hb = h.astype(BF16)
    p = _dot(hb, wm_ref[...])
    c0 = SSM_CH
    c1 = c0 + 3 * DN_WIDTH
    c2 = c1 + DN_WIDTH
    c3 = c2 + CM_WIDTH
    su_ref[...] = p[:, 0:c0]
    qkv_ref[...] = p[:, c0:c1]
    z_ref[...] = p[:, c1:c2]
    cu_ref[...] = p[:, c2:c3]
    cv_ref[...] = p[:, c3:MAIN_COLS]
    ab_ref[...] = _dot(hb, wab_ref[...])


def _in_projection(x, mods_l, norm1, w_main, w_ab, n_ctx_tok, lat_seq):
    n_tok = x.shape[0]
    tm = 512
    n_ctx_tiles = n_ctx_tok // tm
    tps = lat_seq // tm
    row = lambda i: (i, 0)
    const = lambda i: (0, 0)
    outs = [(SSM_CH, F32), (3 * DN_WIDTH, F32), (DN_WIDTH, F32), (CM_WIDTH, F32), (CM_WIDTH, F32),
            (128, F32)]
    return pl.pallas_call(
        _inproj_kernel,
        out_shape=[jax.ShapeDtypeStruct((n_tok, w), dt) for w, dt in outs],
        grid=(n_tok // tm,),
        in_specs=[pl.BlockSpec((tm, D_MODEL), row),
                  pl.BlockSpec((1, 1, N_MOD * D_MODEL),
                               lambda i: (_mod_row_index(i, n_ctx_tiles, tps), 0, 0)),
                  pl.BlockSpec((1, D_MODEL), const),
                  pl.BlockSpec((D_MODEL, MAIN_COLS), const),
                  pl.BlockSpec((D_MODEL, 128), const)],
        out_specs=[pl.BlockSpec((tm, w), row) for w, _ in outs],
        compiler_params=_cparams(("parallel",), VMEM_LIMIT),
        name="in_projection",
    )(x, mods_l, norm1.reshape(1, D_MODEL), w_main, w_ab)


def _s5_prep_kernel(lre_ref, lim_ref, ldt_ref, btre_ref, btim_ref, cre_ref, cim_ref,
                    mt_ref, pt_ref, qt_ref, at_ref):
    t_n, hg, w = S5_T, SSM_GROUP_CH, S5_W
    lre, lim = lre_ref[0], lim_ref[0]
    dt = jnp.exp(ldt_ref[0])
    xr, xi = lre * dt, lim * dt
    tau = lax.broadcasted_iota(jnp.int32, (t_n, 128), 0).astype(F32)
    mag = jnp.exp(tau * xr)
    a_re, a_im = mag * jnp.cos(tau * xi), mag * jnp.sin(tau * xi)
    ar, ai = jnp.exp(xr) * jnp.cos(xi), jnp.exp(xr) * jnp.sin(xi)
    inv = 1.0 / (lre * lre + lim * lim)
    br_, bi_ = lre * inv, -lim * inv
    fr = (ar - 1.0) * br_ - ai * bi_
    fi = (ar - 1.0) * bi_ + ai * br_
    btre, btim = btre_ref[0], btim_ref[0]
    bbr = fr * btre - fi * btim
    bbi = fr * btim + fi * btre
    cre, cim = cre_ref[0], cim_ref[0]

    lane = lax.broadcasted_iota(jnp.int32, (1, 128), 1)
    is_f = lane < SSM_STATE

    def rep_rows(a):
        return jnp.concatenate([jnp.broadcast_to(a[t:t + 1], (hg, 128)) for t in range(t_n)], axis=0)

    def tile_rows(c):
        return jnp.concatenate([c] * t_n, axis=0)

    def rev_rows(a):
        return jnp.concatenate([a[t_n - 1 - t:t_n - t] for t in range(t_n)], axis=0)

    cr_t, ci_t = tile_rows(cre), tile_rows(cim)
    a_re_rep, a_im_rep = rep_rows(a_re), rep_rows(a_im)
    w_re = cr_t * a_re_rep - ci_t * a_im_rep
    w_im = cr_t * a_im_rep + ci_t * a_re_rep
    zero = jnp.zeros_like(w_re)
    e_f = jnp.concatenate([jnp.where(is_f, w_re, zero), jnp.where(is_f, -w_im, zero)], axis=1)
    e_b = jnp.concatenate([jnp.where(is_f, zero, w_re), jnp.where(is_f, zero, -w_im)], axis=1)
    e_b_rev = jnp.concatenate([e_b[(t_n - 1 - t) * hg:(t_n - t) * hg] for t in range(t_n)], axis=0)
    bb_t = tile_rows(jnp.concatenate([bbr, bbi], axis=1))
    row_blk = lax.broadcasted_iota(jnp.int32, (w, w), 0) // hg

    m = jnp.zeros((w, w), F32)
    zrows = lambda n: jnp.zeros((n * hg, w), F32)
    for s in range(t_n):
        down = e_f if s == 0 else jnp.concatenate([zrows(s), e_f[:(t_n - s) * hg]], axis=0)
        k = t_n - 1 - s
        up = e_b_rev if k == 0 else jnp.concatenate([e_b_rev[k * hg:], zrows(k)], axis=0)
        rhs = jnp.where(row_blk == s, bb_t, 0.0)
        m = m + _dot_nt(down + up, rhs, precision=HIGHEST)
    mt_ref[0] = m.T.astype(BF16)

    a_re_rev, a_im_rev = rev_rows(a_re), rev_rows(a_im)
    ps_re = rep_rows(jnp.where(is_f, a_re_rev, a_re))
    ps_im = rep_rows(jnp.where(is_f, a_im_rev, a_im))
    bbr_t, bbi_t = tile_rows(bbr), tile_rows(bbi)
    p_re = ps_re * bbr_t - ps_im * bbi_t
    p_im = ps_re * bbi_t + ps_im * bbr_t
    pt_ref[0] = jnp.concatenate([p_re, p_im], axis=1).astype(BF16)

    a1_re = a_re * ar - a_im * ai
    a1_im = a_re * ai + a_im * ar
    aq_re = rep_rows(jnp.where(is_f, a1_re, rev_rows(a1_re)))
    aq_im = rep_rows(jnp.where(is_f, a1_im, rev_rows(a1_im)))
    q_re = cr_t * aq_re - ci_t * aq_im
    q_im = cr_t * aq_im + ci_t * aq_re
    qt_ref[0] = jnp.concatenate([q_re, -q_im], axis=1).T.astype(BF16)
    at_ref[0] = jnp.concatenate([a1_re[t_n - 1:t_n], a1_im[t_n - 1:t_n]], axis=1)


def _s5_prep(lam_re, lam_im, log_dt, b_re, b_im, c_re, c_im):
    g = SSM_GROUPS
    cat = lambda a: jnp.concatenate([a[0], a[1]], axis=-1)
    lre = cat(lam_re)[:, None, :]
    lim = cat(lam_im)[:, None, :]
    ldt = cat(jnp.broadcast_to(log_dt[..., None], (2, g, SSM_STATE)))[:, None, :]
    btre = cat(jnp.swapaxes(b_re, -1, -2))
    btim = cat(jnp.swapaxes(b_im, -1, -2))
    cre, cim = cat(c_re), cat(c_im)
    vec = pl.BlockSpec((1, 1, 128), lambda i: (i, 0, 0))
    mat = pl.BlockSpec((1, SSM_GROUP_CH, 128), lambda i: (i, 0, 0))
    big = pl.BlockSpec((1, S5_W, S5_W), lambda i: (i, 0, 0))
    return pl.pallas_call(
        _s5_prep_kernel,
        out_shape=[jax.ShapeDtypeStruct((g, S5_W, S5_W), BF16)] * 3
        + [jax.ShapeDtypeStruct((g, 1, 256), F32)],
        grid=(g,),
        in_specs=[vec, vec, vec, mat, mat, mat, mat],
        out_specs=[big, big, big, pl.BlockSpec((1, 1, 256), lambda i: (i, 0, 0))],
        compiler_params=_cparams(("parallel",)),
        name="s5_prep",
    )(lre, lim, ldt, btre, btim, cre, cim)


def _s5_z_kernel(u_ref, pt_ref, z_ref):
    z_ref[0, 0] = _dot(u_ref[0].astype(BF16), pt_ref[0])


def _s5_rec_kernel(z_ref, a_ref, h0_ref, hf_ref, hb_ref, fin_ref, *, n_ctx_blocks, rb, ctx_chunks):
    g = SSM_GROUPS
    j = pl.program_id(0)
    a = a_ref[...]
    a_re, a_im = a[:, :128], a[:, 128:]
    lane = lax.broadcasted_iota(jnp.int32, (g, 128), 1)
    is_f = lane < SSM_STATE

    def rows(c, part):
        return pl.ds(2 * c + part, g, stride=2 * rb)

    def run_seq(base, n, seq):
        def step(k, h):
            h_re, h_im = h
            cf = base + k
            cb = base + n - 1 - k
            hf_ref[0, rows(cf, 0), :] = h_re
            hf_ref[0, rows(cf, 1), :] = h_im
            hb_ref[0, rows(cb, 0), :] = h_re
            hb_ref[0, rows(cb, 1), :] = h_im
            z_re = jnp.where(is_f, z_ref[0, rows(cf, 0), :], z_ref[0, rows(cb, 0), :])
            z_im = jnp.where(is_f, z_ref[0, rows(cf, 1), :], z_ref[0, rows(cb, 1), :])
            return (a_re * h_re - a_im * h_im + z_re, a_re * h_im + a_im * h_re + z_im)
        h0 = h0_ref[seq]
        h_re, h_im = lax.fori_loop(0, n, step, (h0[:, :128], h0[:, 128:]))
        fin_ref[seq] = jnp.concatenate([h_re, h_im], axis=1)

    @pl.when(j < n_ctx_blocks)
    def _():
        seqs = rb // ctx_chunks
        def body(s, carry):
            run_seq(s * ctx_chunks, ctx_chunks, j * seqs + s)
            return carry
        lax.fori_loop(0, seqs, body, 0)

    @pl.when(j >= n_ctx_blocks)
    def _():
        run_seq(0, rb, n_ctx_blocks * (rb // ctx_chunks) + (j - n_ctx_blocks))


def _s5_y_kernel(u_ref, hf_ref, hb_ref, mt_ref, qt_ref, d_ref, y_ref):
    u = u_ref[0]
    lane = lax.broadcasted_iota(jnp.int32, (1, 256), 1)
    is_f = (lane % 128) < SSM_STATE
    h = jnp.where(is_f, hf_ref[0, 0], hb_ref[0, 0]).astype(BF16)
    y_ref[0] = _dot(u.astype(BF16), mt_ref[0]) + _dot(h, qt_ref[0]) + d_ref[0] * u


def _s5_mixer(s_u, mats, ssm_d, h0, n_ctx_seq, ctx_seq, lat_seq):
    mt, pt, qt, at = mats
    g, t_n, w = SSM_GROUPS, S5_T, S5_W
    n_tok = s_u.shape[0]
    nc = n_tok // t_n
    rb = lat_seq // t_n
    nj = nc // rb
    ctx_chunks = ctx_seq // t_n
    n_ctx_blocks = (n_ctx_seq * ctx_chunks) // rb
    n_seq = h0.shape[0]
    u = s_u.reshape(nc, t_n, g, SSM_GROUP_CH).transpose(2, 0, 1, 3).reshape(g, nc, w)

    u_spec = pl.BlockSpec((1, rb, w), lambda j, i: (i, j, 0))
    mat_spec = pl.BlockSpec((1, w, w), lambda j, i: (i, 0, 0))
    blk_spec = pl.BlockSpec((1, 1, rb, w), lambda j, i: (j, i, 0, 0))
    z = pl.pallas_call(
        _s5_z_kernel,
        out_shape=jax.ShapeDtypeStruct((nj, g, rb, w), F32),
        grid=(nj, g),
        in_specs=[u_spec, mat_spec],
        out_specs=blk_spec,
        compiler_params=_cparams(("parallel", "parallel")),
        name="s5_chunk_inputs",
    )(u, pt)

    flat = pl.BlockSpec((1, 2 * g * rb, 128), lambda j: (j, 0, 0))
    whole = lambda shape: pl.BlockSpec(shape, lambda j: (0,) * len(shape))
    hf, hb, fin = pl.pallas_call(
        functools.partial(_s5_rec_kernel, n_ctx_blocks=n_ctx_blocks, rb=rb, ctx_chunks=ctx_chunks),
        out_shape=[jax.ShapeDtypeStruct((nj, 2 * g * rb, 128), F32)] * 2
        + [jax.ShapeDtypeStruct((n_seq, g, w), F32)],
        grid=(nj,),
        in_specs=[flat, whole((g, w)), whole((n_seq, g, w))],
        out_specs=[flat, flat, whole((n_seq, g, w))],
        compiler_params=_cparams(("arbitrary",)),
        name="s5_recurrence",
    )(z.reshape(nj, 2 * g * rb, 128), at.reshape(g, w), h0)

    y = pl.pallas_call(
        _s5_y_kernel,
        out_shape=jax.ShapeDtypeStruct((g, nc, w), F32),
        grid=(nj, g),
        in_specs=[u_spec, blk_spec, blk_spec, mat_spec, mat_spec,
                  pl.BlockSpec((1, 1, w), lambda j, i: (i, 0, 0))],
        out_specs=u_spec,
        compiler_params=_cparams(("parallel", "parallel")),
        name="s5_outputs",
    )(u, hf.reshape(nj, g, rb, w), hb.reshape(nj, g, rb, w), mt, qt,
      jnp.tile(ssm_d.reshape(g, 1, SSM_GROUP_CH), (1, 1, t_n)))
    y_tok = y.reshape(g, nc, t_n, SSM_GROUP_CH).transpose(1, 2, 0, 3).reshape(n_tok, SSM_CH)
    return y_tok, fin


DN_TILE = 256


def _dn_pre_kernel(qkv_ref, halo_ref, w_ref, ab_ref, alog_ref, dtb_ref, qkvn_ref, gcb_ref):
    x = qkv_ref[...]
    n = x.shape[0]
    row = lax.broadcasted_iota(jnp.int32, (n, 1), 0)
    halo = halo_ref[0]
    xp = jnp.where(row == 0, halo[0:1], pltpu.roll(x, 1, 0))
    xn = jnp.where(row == n - 1, halo[1:2], pltpu.roll(x, n - 1, 0))
    w = w_ref[...]
    y = _silu(w[0:1] * xp + w[1:2] * x + w[2:3] * xn)
    for hcol in range(2 * DN_HEADS):
        seg = y[:, hcol * DN_HEAD_DIM:(hcol + 1) * DN_HEAD_DIM]
        nrm = seg * lax.rsqrt(jnp.sum(seg * seg, axis=-1, keepdims=True) + EPS)
        if hcol < DN_HEADS:
            nrm = nrm * (DN_HEAD_DIM ** -0.5)
        qkvn_ref[:, hcol * DN_HEAD_DIM:(hcol + 1) * DN_HEAD_DIM] = nrm.astype(BF16)
    qkvn_ref[:, 2 * DN_WIDTH:] = y[:, 2 * DN_WIDTH:].astype(BF16)

    ab = ab_ref[...]
    lane = lax.broadcasted_iota(jnp.int32, (1, 128), 1)
    glog = jnp.where(lane < 2 * DN_HEADS, -jnp.exp(alog_ref[...]) * _softplus(ab + dtb_ref[...]), 0.0)
    beta = _sigmoid(ab)
    ii = lax.broadcasted_iota(jnp.int32, (n, n), 0)
    jj = lax.broadcasted_iota(jnp.int32, (n, n), 1)
    same = (ii // DN_CHUNK) == (jj // DN_CHUNK)
    pre = jnp.where(same & (jj <= ii), 1.0, 0.0).astype(F32)
    suf = jnp.where(same & (jj >= ii), 1.0, 0.0).astype(F32)
    gpre = jnp.dot(pre, glog, preferred_element_type=F32, precision=HIGHEST)
    gsuf = jnp.dot(suf, glog, preferred_element_type=F32, precision=HIGHEST)
    gcb_ref[...] = jnp.where(lane < DN_HEADS, gpre, jnp.where(lane < 2 * DN_HEADS, gsuf, beta))


def _dn_pre(qkv, ab, conv_w, a_log, dt_bias, n_ctx_tok, lat_seq):
    n_tok = qkv.shape[0]
    tm = DN_TILE
    nt = n_tok // tm
    tiles = qkv.reshape(nt, tm, 3 * DN_WIDTH)
    first, last = tiles[:, 0], tiles[:, tm - 1]
    tile_id = jnp.arange(nt)
    n_ctx_tiles = n_ctx_tok // tm
    tps = lat_seq // tm
    lat_pos = (tile_id - n_ctx_tiles) % tps
    has_prev = (tile_id >= n_ctx_tiles) & (lat_pos > 0)
    has_next = (tile_id >= n_ctx_tiles) & (lat_pos < tps - 1)
    prev = jnp.where(has_prev[:, None], jnp.roll(last, 1, axis=0), 0.0)
    nxt = jnp.where(has_next[:, None], jnp.roll(first, -1, axis=0), 0.0)
    halo = jnp.concatenate([prev[:, None], nxt[:, None], jnp.zeros((nt, 6, 3 * DN_WIDTH), F32)], axis=1)
    w8 = jnp.concatenate([conv_w, jnp.zeros((5, 3 * DN_WIDTH), F32)], axis=0)
    pad = lambda v: jnp.concatenate([v.reshape(1, 2 * DN_HEADS), jnp.zeros((1, 120), F32)], axis=1)
    row = lambda i: (i, 0)
    const = lambda i: (0, 0)
    return pl.pallas_call(
        _dn_pre_kernel,
        out_shape=[jax.ShapeDtypeStruct((n_tok, 3 * DN_WIDTH), BF16),
                   jax.ShapeDtypeStruct((n_tok, 128), F32)],
        grid=(nt,),
        in_specs=[pl.BlockSpec((tm, 3 * DN_WIDTH), row),
                  pl.BlockSpec((1, 8, 3 * DN_WIDTH), lambda i: (i, 0, 0)),
                  pl.BlockSpec((8, 3 * DN_WIDTH), const),
                  pl.BlockSpec((tm, 128), row),
                  pl.BlockSpec((1, 128), const),
                  pl.BlockSpec((1, 128), const)],
        out_specs=[pl.BlockSpec((tm, 3 * DN_WIDTH), row), pl.BlockSpec((tm, 128), row)],
        compiler_params=_cparams(("parallel",), VMEM_LIMIT),
        name="dn_pre",
    )(qkv, halo, w8, ab, pad(a_log), pad(dt_bias))


DN_CHAINS = 2 * DN_HEADS


def _split(x):
    hi = x.astype(BF16)
    lo = (x - hi.astype(F32)).astype(BF16)
    return hi, lo


def _split_lhs(a):
    hi, lo = _split(a)
    hif = hi.astype(F32)
    return jnp.concatenate([hif, hif, lo.astype(F32)], axis=1).astype(BF16)


def _split_rhs(b):
    hi, lo = _split(b)
    return jnp.concatenate([hi, lo, hi], axis=0)


def _dn_prepare(q_ref, k_ref, v_ref, gc_ref, gt_ref, row0, chunk_idx):
    c = DN_CHUNK
    ii = lax.broadcasted_iota(jnp.int32, (c, c), 0)
    jj = lax.broadcasted_iota(jnp.int32, (c, c), 1)
    items = []
    for d in range(2):
        rows = pl.ds(row0[d], c)
        gcol = gc_ref[rows, :]
        grow = gt_ref[chunk_idx[d]]
        incl = (ii >= jj) if d == 0 else (ii <= jj)
        strict = (ii > jj) if d == 0 else (ii < jj)
        for h in range(DN_HEADS):
            ch = d * DN_HEADS + h
            cols = slice(h * DN_HEAD_DIM, (h + 1) * DN_HEAD_DIM)
            q, k, v = q_ref[rows, cols], k_ref[rows, cols], v_ref[rows, cols]
            gc_col = gcol[:, ch:ch + 1]
            beta = gcol[:, DN_CHAINS + ch:DN_CHAINS + ch + 1]
            gc_row = grow[ch:ch + 1, :]
            kf = k.astype(F32)
            kb = kf * beta
            decay = jnp.exp(jnp.where(incl, gc_col - gc_row, -jnp.inf))
            egc = jnp.exp(gc_col)
            g_last = gc_col[c - 1:c] if d == 0 else gc_col[0:1]
            items.append(dict(
                q=q, k=k, kb=kb.astype(BF16), decay=decay, incl=incl, strict=strict,
                rhs=jnp.concatenate([kb * egc, v.astype(F32) * beta], axis=1),
                qg=(q.astype(F32) * egc).astype(BF16),
                kt=(kf * jnp.exp(g_last - gc_col)).astype(BF16),
                gl=jnp.broadcast_to(jnp.exp(g_last), (8, 128))))
    for it in items:
        it['a'] = jnp.where(it['strict'], _dot_nt(it['kb'], it['k']) * it['decay'], 0.0)
        it['qk'] = jnp.where(it['incl'], _dot_nt(it['q'], it['k']) * it['decay'], 0.0).astype(BF16)
    power = 1
    while power < c:
        for it in items:
            it['lhs'] = _split_lhs(it['a'])
        for it in items:
            upd = _dot(it['lhs'], _split_rhs(it['rhs'] if power == 1 else it['r']))
            it['r'] = it['rhs'] - upd if power == 1 else it['r'] + upd
        if 2 * power < c:
            for it in items:
                it['a'] = _dot(it['lhs'], _split_rhs(it['a']))
        power *= 2
    return items


def _dn_scan_kernel(*refs, n_chunks, seq_len, seqs_per_blk, has_s0, has_prev_o, write_fin):
    refs = list(refs)
    q_ref, k_ref, v_ref, gc_ref, gt_ref = refs[:5]
    pos = 5
    s0_ref = None
    if has_s0:
        s0_ref = refs[pos]
        pos += 1
    if has_prev_o:
        pos += 1
    o_ref = refs[pos]
    pos += 1
    fin_ref = None
    if write_fin:
        fin_ref = refs[pos]
        pos += 1
    s_ref, w_s, u_s, qk_s, qg_s, kt_s, gl_s = refs[pos:pos + 7]
    n_steps = seqs_per_blk * n_chunks

    o_ref[...] = jnp.zeros_like(o_ref)
    for r in (s_ref, w_s, u_s, qk_s, qg_s, kt_s, gl_s):
        r[...] = jnp.zeros_like(r)

    def locate(step):
        seq = step // n_chunks
        kk = step % n_chunks
        cf = seq * n_chunks + kk
        cb = seq * n_chunks + (n_chunks - 1 - kk)
        rows = [pl.multiple_of(cf * DN_CHUNK, DN_CHUNK), pl.multiple_of(cb * DN_CHUNK, DN_CHUNK)]
        return seq, kk, rows, [cf, cb]

    def body(t, carry):
        tb = jnp.maximum(t, 0)
        valid = t >= 0
        seq, kk, rows_b, _ = locate(tb)
        first = kk == 0
        states, vnews = [], []
        for ch in range(DN_CHAINS):
            d, h = divmod(ch, DN_HEADS)
            s0 = s0_ref[0, d, h] if has_s0 else jnp.zeros((DN_HEAD_DIM, DN_HEAD_DIM), F32)
            states.append(jnp.where(first, s0, s_ref[ch]))
        for ch in range(DN_CHAINS):
            vnews.append((u_s[ch] - _dot(w_s[ch], states[ch].astype(BF16))).astype(BF16))
        tn = jnp.minimum(t + 1, n_steps - 1)
        _, _, rows_a, chunks_a = locate(tn)
        items = _dn_prepare(q_ref, k_ref, v_ref, gc_ref, gt_ref, rows_a, chunks_a)
        for ch in range(DN_CHAINS):
            d, h = divmod(ch, DN_HEADS)
            sb = states[ch].astype(BF16)
            o = _dot(qg_s[ch], sb) + _dot(qk_s[ch], vnews[ch])
            s_new = states[ch] * gl_s[ch][0:1, 0:1] + _dot_tn(kt_s[ch], vnews[ch])
            s_ref[ch] = s_new
            if write_fin:
                fin_ref[seq, d, h] = s_new
            rows = pl.ds(rows_b[d], DN_CHUNK)
            cols = slice(h * DN_HEAD_DIM, (h + 1) * DN_HEAD_DIM)
            o_ref[rows, cols] = o_ref[rows, cols] + jnp.where(valid, o, 0.0)
        for ch, it in enumerate(items):
            w_s[ch] = it['r'][:, :DN_HEAD_DIM].astype(BF16)
            u_s[ch] = it['r'][:, DN_HEAD_DIM:]
            qk_s[ch] = it['qk']
            qg_s[ch] = it['qg']
            kt_s[ch] = it['kt']
            gl_s[ch] = it['gl']
        return carry

    lax.fori_loop(-1, n_steps, body, 0)


def _dn_scan(qkvn, gcb, gcb_t, seq_len, n_seq, seqs_per_blk, first_tok, s0, prev_o, write_fin):
    n_tok = qkvn.shape[0]
    n_chunks = seq_len // DN_CHUNK
    rows_blk = seqs_per_blk * seq_len
    b0 = first_tok // rows_blk
    hd = DN_HEAD_DIM
    st_spec = pl.BlockSpec((seqs_per_blk, 2, DN_HEADS, hd, hd), lambda s: (s, 0, 0, 0, 0))
    in_specs = [pl.BlockSpec((rows_blk, DN_WIDTH), lambda s: (b0 + s, 0)),
                pl.BlockSpec((rows_blk, DN_WIDTH), lambda s: (b0 + s, 1)),
                pl.BlockSpec((rows_blk, DN_WIDTH), lambda s: (b0 + s, 2)),
                pl.BlockSpec((rows_blk, 128), lambda s: (b0 + s, 0)),
                pl.BlockSpec((rows_blk // DN_CHUNK, 16, DN_CHUNK), lambda s: (b0 + s, 0, 0))]
    args = [qkvn, qkvn, qkvn, gcb, gcb_t]
    aliases = {}
    if s0 is not None:
        assert seqs_per_blk == 1
        in_specs.append(st_spec)
        args.append(s0)
    if prev_o is not None:
        in_specs.append(pl.BlockSpec(memory_space=pl.ANY))
        aliases = {len(args): 0}
        args.append(prev_o)
    out_shape = [jax.ShapeDtypeStruct((n_tok, DN_WIDTH), F32)]
    out_specs = [pl.BlockSpec((rows_blk, DN_WIDTH), lambda s: (b0 + s, 0))]
    if write_fin:
        out_shape.append(jax.ShapeDtypeStruct((n_seq, 2, DN_HEADS, hd, hd), F32))
        out_specs.append(st_spec)
    c = DN_CHUNK
    scratch = [pltpu.VMEM((DN_CHAINS, hd, hd), F32),
               pltpu.VMEM((DN_CHAINS, c, hd), BF16),
               pltpu.VMEM((DN_CHAINS, c, hd), F32),
               pltpu.VMEM((DN_CHAINS, c, c), BF16),
               pltpu.VMEM((DN_CHAINS, c, hd), BF16),
               pltpu.VMEM((DN_CHAINS, c, hd), BF16),
               pltpu.VMEM((DN_CHAINS, 8, 128), F32)]
    return pl.pallas_call(
        functools.partial(_dn_scan_kernel, n_chunks=n_chunks, seq_len=seq_len, seqs_per_blk=seqs_per_blk,
                          has_s0=s0 is not None, has_prev_o=prev_o is not None, write_fin=write_fin),
        out_shape=out_shape,
        grid=(n_seq // seqs_per_blk,),
        in_specs=in_specs,
        out_specs=out_specs,
        scratch_shapes=scratch,
        input_output_aliases=aliases,
        compiler_params=_cparams(("parallel",), VMEM_LIMIT),
        name="dn_scan_%d" % seq_len,
    )(*args)


def _deltanet(qkv, ab, p, s0_lat, n_ctx_seq, ctx_seq, n_lat_seq, lat_seq):
    n_ctx_tok = n_ctx_seq * ctx_seq
    qkvn, gcb = _dn_pre(qkv, ab, p['dn_conv'], p['dn_a_log'], p['dn_dt_bias'], n_ctx_tok, lat_seq)
    n_tok = qkv.shape[0]
    gcb_t = gcb[:, :16].reshape(n_tok // DN_CHUNK, DN_CHUNK, 16).transpose(0, 2, 1)
    ctx_blk = math.gcd(n_ctx_seq, 8)
    o_ctx, fin = _dn_scan(qkvn, gcb, gcb_t, ctx_seq, n_ctx_seq, ctx_blk, 0, None, None, True)
    (o_all,) = _dn_scan(qkvn, gcb, gcb_t, lat_seq, n_lat_seq, 1, n_ctx_tok, s0_lat, o_ctx, False)
    return o_all, fin


MIX_TILE = 256


def _mix_kernel(x_ref, mod_ref, ys_ref, o_ref, z_ref, cu_ref, cv_ref,
                wglu_ref, bglu_ref, dnn_ref, cmn_ref, ws_ref, bs_ref, wo_ref, n2_ref,
                x1_ref, h2_ref):
    mod = mod_ref[0]
    gate1 = mod[:, 2 * D_MODEL:3 * D_MODEL]
    shift2 = mod[:, 3 * D_MODEL:4 * D_MODEL]
    scale2 = mod[:, 4 * D_MODEL:5 * D_MODEL]
    ya = _gelu(ys_ref[...])
    ya = ya * _sigmoid(_dot(ya.astype(BF16), wglu_ref[...]) + bglu_ref[...])
    mix = _dot(ya.astype(BF16), wo_ref[0:SSM_CH, :])
    o = o_ref[...]
    z = z_ref[...]
    parts = []
    for h in range(DN_HEADS):
        sl = slice(h * DN_HEAD_DIM, (h + 1) * DN_HEAD_DIM)
        parts.append(_rms(o[:, sl]) * dnn_ref[...] * _silu(z[:, sl]))
    yb = jnp.concatenate(parts, axis=1)
    mix = mix + _dot(yb.astype(BF16), wo_ref[SSM_CH:SSM_CH + DN_WIDTH, :])
    u = _gelu(cu_ref[...])
    v = _gelu(cv_ref[...])
    seg_i = lax.broadcasted_iota(jnp.int32, (CM_WIDTH, CM_WIDTH), 0) // CM_HEAD_DIM
    seg_j = lax.broadcasted_iota(jnp.int32, (CM_WIDTH, CM_WIDTH), 1) // CM_HEAD_DIM
    seg_ones = jnp.where(seg_i == seg_j, 1.0, 0.0).astype(BF16)
    vv = v * v
    vv_hi = vv.astype(BF16)
    vv_lo = (vv - vv_hi.astype(F32)).astype(BF16)
    ms = (_dot(vv_hi, seg_ones) + _dot(vv_lo, seg_ones)) * (1.0 / CM_HEAD_DIM)
    vn = (v * lax.rsqrt(ms + EPS) * cmn_ref[...]).astype(BF16)
    head_of_lane = lax.broadcasted_iota(jnp.int32, (1, CM_WIDTH), 1) // CM_HEAD_DIM
    ycs = []
    for c in range(MIX_TILE // CM_CHUNK):
        vc = vn[c * CM_CHUNK:(c + 1) * CM_CHUNK]
        stack = jnp.concatenate([jnp.where(head_of_lane == h, vc, jnp.zeros_like(vc))
                                 for h in range(CM_HEADS)], axis=0)
        vs = _dot(ws_ref[...], stack) + bs_ref[...]
        ycs.append(u[c * CM_CHUNK:(c + 1) * CM_CHUNK] * vs)
    yc = jnp.concatenate(ycs, axis=0)
    mix = mix + _dot(yc.astype(BF16), wo_ref[SSM_CH + DN_WIDTH:, :])
    x1 = x_ref[...] + gate1 * mix
    x1_ref[...] = x1
    h2_ref[...] = (_rms(x1) * n2_ref[...] * (1.0 + scale2) + shift2).astype(BF16)


def _mix_out(x, mods_l, ys, o, z, cu, cv, p, n_ctx_tok, lat_seq):
    n_tok = x.shape[0]
    tm = MIX_TILE
    n_ctx_tiles = n_ctx_tok // tm
    tps = lat_seq // tm
    row = lambda i: (i, 0)
    const = lambda i: (0, 0)
    ws_cat = jnp.concatenate([p['cm_w_s'][h] for h in range(CM_HEADS)], axis=1).astype(BF16)
    bs_exp = jnp.repeat(p['cm_b_s'].T, CM_HEAD_DIM, axis=1)
    return pl.pallas_call(
        _mix_kernel,
        out_shape=[jax.ShapeDtypeStruct((n_tok, D_MODEL), F32),
                   jax.ShapeDtypeStruct((n_tok, D_MODEL), BF16)],
        grid=(n_tok // tm,),
        in_specs=[pl.BlockSpec((tm, D_MODEL), row),
                  pl.BlockSpec((1, 1, N_MOD * D_MODEL),
                               lambda i: (_mod_row_index(i, n_ctx_tiles, tps), 0, 0)),
                  pl.BlockSpec((tm, SSM_CH), row),
                  pl.BlockSpec((tm, DN_WIDTH), row),
                  pl.BlockSpec((tm, DN_WIDTH), row),
                  pl.BlockSpec((tm, CM_WIDTH), row),
                  pl.BlockSpec((tm, CM_WIDTH), row),
                  pl.BlockSpec((SSM_CH, SSM_CH), const),
                  pl.BlockSpec((1, SSM_CH), const),
                  pl.BlockSpec((1, DN_HEAD_DIM), const),
                  pl.BlockSpec((1, CM_WIDTH), const),
                  pl.BlockSpec((CM_CHUNK, CM_HEADS * CM_CHUNK), const),
                  pl.BlockSpec((CM_CHUNK, CM_WIDTH), const),
                  pl.BlockSpec((D_MODEL, D_MODEL), const),
                  pl.BlockSpec((1, D_MODEL), const)],
        out_specs=[pl.BlockSpec((tm, D_MODEL), row), pl.BlockSpec((tm, D_MODEL), row)],
        compiler_params=_cparams(("parallel",), VMEM_LIMIT),
        name="mix_out",
    )(x, mods_l, ys, o, z, cu, cv,
      p['ssm_w_glu'].astype(BF16), p['ssm_b_glu'].reshape(1, SSM_CH),
      p['dn_norm'].reshape(1, DN_HEAD_DIM), p['cm_norm'].reshape(1, CM_WIDTH),
      ws_cat, bs_exp, p['w_out'].astype(BF16), p['norm2'].reshape(1, D_MODEL))


ROUTE_TILE = 512


def _first_max_mask(x, idx, axis):
    m = jnp.max(x, axis=axis, keepdims=True)
    first = jnp.min(jnp.where(x == m, idx, jnp.int32(1 << 20)), axis=axis, keepdims=True)
    return idx == first


def _router_kernel(h_ref, wr_ref, bias_ref, g_ref):
    n = h_ref.shape[0]
    per = N_EXPERTS // N_ROUTE_GROUPS
    logits = _dot_nt(wr_ref[...], h_ref[...])
    scores = _sigmoid(logits)
    choice = scores + bias_ref[...]
    ninf = jnp.float32(-jnp.inf)
    c3 = choice.reshape(N_ROUTE_GROUPS, per, n)
    i3 = lax.broadcasted_iota(jnp.int32, c3.shape, 1)
    top1 = _first_max_mask(c3, i3, 1)
    m1 = jnp.max(c3, axis=1)
    m2 = jnp.max(jnp.where(top1, ninf, c3), axis=1)
    grp = m1 + m2
    gi = lax.broadcasted_iota(jnp.int32, grp.shape, 0)
    gsel = jnp.zeros(grp.shape, jnp.bool_)
    work = grp
    for _ in range(TOPK_ROUTE_GROUPS):
        pick = _first_max_mask(work, gi, 0)
        gsel = gsel | pick
        work = jnp.where(pick, ninf, work)
    gmask = jnp.broadcast_to(gsel[:, None, :], c3.shape).reshape(N_EXPERTS, n)
    work = jnp.where(gmask, choice, ninf)
    ei = lax.broadcasted_iota(jnp.int32, work.shape, 0)
    esel = jnp.zeros(work.shape, jnp.bool_)
    for _ in range(TOP_K):
        pick = _first_max_mask(work, ei, 0)
        esel = esel | pick
        work = jnp.where(pick, ninf, work)
    wts = jnp.where(esel, scores, 0.0)
    g_ref[...] = wts / (jnp.sum(wts, axis=0, keepdims=True) + 1e-20) * ROUTED_SCALE


def _router(h2, router_w, router_bias):
    n_tok = h2.shape[0]
    tm = ROUTE_TILE
    return pl.pallas_call(
        _router_kernel,
        out_shape=jax.ShapeDtypeStruct((N_EXPERTS, n_tok), F32),
        grid=(n_tok // tm,),
        in_specs=[pl.BlockSpec((tm, D_MODEL), lambda i: (i, 0)),
                  pl.BlockSpec((N_EXPERTS, D_MODEL), lambda i: (0, 0)),
                  pl.BlockSpec((N_EXPERTS, 1), lambda i: (0, 0))],
        out_specs=pl.BlockSpec((N_EXPERTS, tm), lambda i: (0, i)),
        compiler_params=_cparams(("parallel",)),
        name="router",
    )(h2, router_w.T.astype(BF16), router_bias.reshape(N_EXPERTS, 1))


MOE_TILE = 1024
MOE_EG = 8


def _moe_kernel(h_ref, x_ref, mod_ref, g_ref, wg_ref, wu_ref, wd_ref, sg_ref, su_ref, sd_ref, nf_ref,
                *out_refs, final):
    j = pl.program_id(1)
    nj = pl.num_programs(1)
    acc_ref = out_refs[-1]
    h = h_ref[...]

    @pl.when(j == 0)
    def _():
        sh = _silu(_dot(h, sg_ref[...])) * _dot(h, su_ref[...])
        acc_ref[...] = _dot(sh.astype(BF16), sd_ref[...])

    hg = _dot(h, wg_ref[...])
    hu = _dot(h, wu_ref[...])
    gates = g_ref[0]
    cols = []
    for e in range(MOE_EG):
        sl = slice(e * EXPERT_FF, (e + 1) * EXPERT_FF)
        cols.append((_silu(hg[:, sl]) * hu[:, sl] * gates[:, e:e + 1]).astype(BF16))
    act = jnp.concatenate(cols, axis=1)
    acc_ref[...] += _dot(act, wd_ref[...])

    @pl.when(j == nj - 1)
    def _():
        gate2 = mod_ref[0][:, 5 * D_MODEL:6 * D_MODEL]
        x2 = x_ref[...] + gate2 * acc_ref[...]
        if final:
            out_refs[0][...] = _rms(x2) * nf_ref[...]
        else:
            out_refs[0][...] = x2


def _moe(h2, x1, mods_l, gates_t, p, norm_f, final, n_ctx_tok, lat_seq):
    n_tok = h2.shape[0]
    tm = MOE_TILE
    n_ctx_tiles = n_ctx_tok // tm
    tps = lat_seq // tm
    neg = N_EXPERTS // MOE_EG
    ef = MOE_EG * EXPERT_FF
    gates = gates_t.reshape(neg, MOE_EG, n_tok).transpose(0, 2, 1)
    wg = p['moe_w_gate'].transpose(1, 0, 2).reshape(D_MODEL, N_EXPERTS * EXPERT_FF).astype(BF16)
    wu = p['moe_w_up'].transpose(1, 0, 2).reshape(D_MODEL, N_EXPERTS * EXPERT_FF).astype(BF16)
    wd = p['moe_w_down'].reshape(N_EXPERTS * EXPERT_FF, D_MODEL).astype(BF16)
    row = lambda i, j: (i, 0)
    const = lambda i, j: (0, 0)
    out = pl.pallas_call(
        functools.partial(_moe_kernel, final=final),
        out_shape=jax.ShapeDtypeStruct((n_tok, D_MODEL), F32),
        grid=(n_tok // tm, neg),
        in_specs=[pl.BlockSpec((tm, D_MODEL), row),
                  pl.BlockSpec((tm, D_MODEL), row),
                  pl.BlockSpec((1, 1, N_MOD * D_MODEL),
                               lambda i, j: (_mod_row_index(i, n_ctx_tiles, tps), 0, 0)),
                  pl.BlockSpec((1, tm, MOE_EG), lambda i, j: (j, i, 0)),
                  pl.BlockSpec((D_MODEL, ef), lambda i, j: (0, j)),
                  pl.BlockSpec((D_MODEL, ef), lambda i, j: (0, j)),
                  pl.BlockSpec((ef, D_MODEL), lambda i, j: (j, 0)),
                  pl.BlockSpec((D_MODEL, SHARED_FF), const),
                  pl.BlockSpec((D_MODEL, SHARED_FF), const),
                  pl.BlockSpec((SHARED_FF, D_MODEL), const),
                  pl.BlockSpec((1, D_MODEL), const)],
        out_specs=pl.BlockSpec((tm, D_MODEL), row),
        scratch_shapes=[pltpu.VMEM((tm, D_MODEL), F32)],
        compiler_params=_cparams(("parallel", "arbitrary"), VMEM_LIMIT),
        name="moe",
    )(h2, x1, mods_l, gates, wg, wu, wd,
      p['moe_ws_gate'].astype(BF16), p['moe_ws_up'].astype(BF16), p['moe_ws_down'].astype(BF16),
      norm_f.reshape(1, D_MODEL))
    return out


def kernel(x_prompt, x_sample, state_ssm, state_delta, c, c_ctx, w_ada, b_ada, norm1, norm2, w_in, w_out, ssm_lam_re, ssm_lam_im, ssm_log_dt, ssm_b_re, ssm_b_im, ssm_c_re, ssm_c_im, ssm_d, ssm_w_glu, ssm_b_glu, dn_conv, dn_a_log, dn_dt_bias, dn_norm, cm_norm, cm_w_s, cm_b_s, moe_router, moe_router_bias, moe_w_gate, moe_w_up, moe_w_down, moe_ws_gate, moe_ws_up, moe_ws_down, norm_f):
    n_ctx_seq, ctx_seq, _ = x_prompt.shape
    n_lat_seq, lat_seq, _ = x_sample.shape
    depth = w_ada.shape[0]
    n_ctx_tok = n_ctx_seq * ctx_seq
    n_seq = n_ctx_seq + n_lat_seq
    g = SSM_GROUPS

    x = jnp.concatenate([x_prompt.reshape(n_ctx_tok, D_MODEL),
                         x_sample.reshape(n_lat_seq * lat_seq, D_MODEL)], axis=0)
    cvec = jnp.concatenate([c_ctx[None, :], c, jnp.zeros((8 - 1 - n_lat_seq, D_MODEL), F32)], axis=0)
    mods = _ada_mods(cvec, w_ada, b_ada)

    ssm_states, dn_states = [], []
    y = None
    for l in range(depth):
        p = {'w_out': w_out[l], 'norm2': norm2[l], 'ssm_w_glu': ssm_w_glu[l], 'ssm_b_glu': ssm_b_glu[l],
             'dn_conv': dn_conv[l], 'dn_a_log': dn_a_log[l], 'dn_dt_bias': dn_dt_bias[l],
             'dn_norm': dn_norm[l], 'cm_norm': cm_norm[l], 'cm_w_s': cm_w_s[l], 'cm_b_s': cm_b_s[l],
             'moe_w_gate': moe_w_gate[l], 'moe_w_up': moe_w_up[l], 'moe_w_down': moe_w_down[l],
             'moe_ws_gate': moe_ws_gate[l], 'moe_ws_up': moe_ws_up[l], 'moe_ws_down': moe_ws_down[l]}
        mods_l = mods[l].reshape(8, 1, N_MOD * D_MODEL)
        wi = w_in[l]
        c_ab = SSM_CH + 4 * DN_WIDTH
        w_main = jnp.concatenate([wi[:, :c_ab], wi[:, c_ab + 16:]], axis=1).astype(BF16)
        w_ab = jnp.concatenate([wi[:, c_ab:c_ab + 16], jnp.zeros((D_MODEL, 112), F32)], axis=1).astype(BF16)
        s_u, qkv, z, cu, cv, ab = _in_projection(x, mods_l, norm1[l], w_main, w_ab, n_ctx_tok, lat_seq)

        mats = _s5_prep(ssm_lam_re[l], ssm_lam_im[l], ssm_log_dt[l], ssm_b_re[l], ssm_b_im[l],
                        ssm_c_re[l], ssm_c_im[l])
        h0_lat = state_ssm[:, l].transpose(0, 3, 2, 1, 4).reshape(n_lat_seq, g, 256)
        h0 = jnp.concatenate([jnp.zeros((n_ctx_seq, g, 256), F32), h0_lat], axis=0)
        ys, ssm_fin = _s5_mixer(s_u, mats, ssm_d[l], h0, n_ctx_seq, ctx_seq, lat_seq)
        ssm_states.append(ssm_fin[:n_ctx_seq].reshape(n_ctx_seq, g, 2, 2, SSM_STATE).transpose(0, 3, 2, 1, 4))

        o_dn, dn_fin = _deltanet(qkv, ab, p, state_delta[:, l], n_ctx_seq, ctx_seq, n_lat_seq, lat_seq)
        dn_states.append(dn_fin)

        x1, h2 = _mix_out(x, mods_l, ys, o_dn, z, cu, cv, p, n_ctx_tok, lat_seq)
        gates_t = _router(h2, moe_router[l], moe_router_bias[l])
        x = _moe(h2, x1, mods_l, gates_t, p, norm_f, l == depth - 1, n_ctx_tok, lat_seq)

    y_prompt = x[:n_ctx_tok].reshape(x_prompt.shape)
    y_sample = x[n_ctx_tok:].reshape(x_sample.shape)
    new_state_ssm = jnp.stack(ssm_states, axis=1)
    new_state_delta = jnp.stack(dn_states, axis=1)
    return (y_prompt, y_sample, new_state_ssm, new_state_delta)
```

```python
import functools
import math

import jax
import jax.numpy as jnp
from jax import lax
from jax.experimental import pallas as pl
from jax.experimental.pallas import tpu as pltpu

F32 = jnp.float32
BF16 = jnp.bfloat16
HIGHEST = lax.Precision.HIGHEST

D_MODEL = 1024
N_MOD = 6
EPS = 1e-6
SSM_CH = 256
SSM_GROUPS = 16
SSM_GROUP_CH = 16
SSM_STATE = 64
S5_T = 16
S5_W = S5_T * SSM_GROUP_CH
DN_HEADS = 4
DN_HEAD_DIM = 128
DN_WIDTH = 512
DN_CHUNK = 64
CM_HEADS = 4
CM_WIDTH = 256
CM_HEAD_DIM = 64
CM_CHUNK = 128
N_EXPERTS = 64
TOP_K = 6
N_ROUTE_GROUPS = 8
TOPK_ROUTE_GROUPS = 4
EXPERT_FF = 128
SHARED_FF = 256
ROUTED_SCALE = 2.5
MAIN_COLS = SSM_CH + 3 * DN_WIDTH + DN_WIDTH + 2 * CM_WIDTH
VMEM_LIMIT = 56 * 1024 * 1024


def _cparams(sem, vmem=None):
    return pltpu.CompilerParams(dimension_semantics=sem, vmem_limit_bytes=vmem)


def _dot(a, b):
    return jnp.dot(a, b, preferred_element_type=F32)


def _dot_nt(a, b, precision=None):
    return lax.dot_general(a, b, (((1,), (1,)), ((), ())), preferred_element_type=F32,
                           precision=precision)


def _dot_tn(a, b):
    return lax.dot_general(a, b, (((0,), (0,)), ((), ())), preferred_element_type=F32)


def _sigmoid(x):
    return 1.0 / (1.0 + jnp.exp(-x))


def _silu(x):
    return x * _sigmoid(x)


def _gelu(x):
    c = math.sqrt(2.0 / math.pi)
    return 0.5 * x * (1.0 + jnp.tanh(c * (x + 0.044715 * (x * x * x))))


def _softplus(x):
    return jnp.maximum(x, 0.0) + jnp.log(1.0 + jnp.exp(-jnp.abs(x)))


def _rms(x):
    return x * lax.rsqrt(jnp.mean(x * x, axis=-1, keepdims=True) + EPS)


def _mod_row_index(tile, n_ctx_tiles, tiles_per_lat_seq):
    return jnp.where(tile < n_ctx_tiles, 0, 1 + (tile - n_ctx_tiles) // tiles_per_lat_seq)


def _ada_kernel(c_ref, w_ref, b_ref, o_ref):
    h = _silu(c_ref[...])
    o_ref[0] = jnp.dot(h, w_ref[0], preferred_element_type=F32, precision=HIGHEST) + b_ref[0]


def _ada_mods(cvec, w_ada, b_ada):
    depth, d, n = w_ada.shape
    bn = 1536
    return pl.pallas_call(
        _ada_kernel,
        out_shape=jax.ShapeDtypeStruct((depth, 8, n), F32),
        grid=(depth, n // bn),
        in_specs=[pl.BlockSpec((8, d), lambda l, j: (0, 0)),
                  pl.BlockSpec((1, d, bn), lambda l, j: (l, 0, j)),
                  pl.BlockSpec((1, 1, bn), lambda l, j: (l, 0, j))],
        out_specs=pl.BlockSpec((1, 8, bn), lambda l, j: (l, 0, j)),
        compiler_params=_cparams(("parallel", "parallel")),
        name="ada_mods",
    )(cvec, w_ada, b_ada.reshape(depth, 1, n))


def _inproj_kernel(x_ref, mod_ref, g_ref, wm_ref, wab_ref,
                   su_ref, qkv_ref, z_ref, cu_ref, cv_ref, ab_ref):
    mod = mod_ref[0]
    shift = mod[:, 0:D_MODEL]
    scale = mod[:, D_MODEL:2 * D_MODEL]
    h = _rms(x_ref[...]) * g_ref[...] * (1.0 + scale) + shift
    hb = h.astype(BF16)
    p = _dot(hb, wm_ref[...])
    c0 = SSM_CH
    c1 = c0 + 3 * DN_WIDTH
    c2 = c1 + DN_WIDTH
    c3 = c2 + CM_WIDTH
    su_ref[0] = p[:, 0:128]
    su_ref[1] = p[:, 128:c0]
    qkv_ref[...] = p[:, c0:c1]
    z_ref[...] = p[:, c1:c2]
    cu_ref[...] = p[:, c2:c3]
    cv_ref[...] = p[:, c3:MAIN_COLS]
    ab_ref[...] = _dot(hb, wab_ref[...])


def _in_projection(x, mods_l, norm1, w_main, w_ab, n_ctx_tok, lat_seq):
    n_tok = x.shape[0]
    tm = 512
    n_ctx_tiles = n_ctx_tok // tm
    tps = lat_seq // tm
    row = lambda i: (i, 0)
    const = lambda i: (0, 0)
    outs = [(3 * DN_WIDTH, F32), (DN_WIDTH, F32), (CM_WIDTH, F32), (CM_WIDTH, F32), (128, F32)]
    return pl.pallas_call(
        _inproj_kernel,
        out_shape=[jax.ShapeDtypeStruct((2, n_tok, 128), F32)]
        + [jax.ShapeDtypeStruct((n_tok, w), dt) for w, dt in outs],
        grid=(n_tok // tm,),
        in_specs=[pl.BlockSpec((tm, D_MODEL), row),
                  pl.BlockSpec((1, 1, N_MOD * D_MODEL),
                               lambda i: (_mod_row_index(i, n_ctx_tiles, tps), 0, 0)),
                  pl.BlockSpec((1, D_MODEL), const),
                  pl.BlockSpec((D_MODEL, MAIN_COLS), const),
                  pl.BlockSpec((D_MODEL, 128), const)],
        out_specs=[pl.BlockSpec((2, tm, 128), lambda i: (0, i, 0))]
        + [pl.BlockSpec((tm, w), row) for w, _ in outs],
        compiler_params=_cparams(("parallel",), VMEM_LIMIT),
        name="in_projection",
    )(x, mods_l, norm1.reshape(1, D_MODEL), w_main, w_ab)


def _s5_prep_kernel(lre_ref, lim_ref, ldt_ref, btre_ref, btim_ref, cre_ref, cim_ref,
                    mt_ref, pt_ref, qt_ref, at_ref):
    t_n, hg, w = S5_T, SSM_GROUP_CH, S5_W
    lre, lim = lre_ref[0], lim_ref[0]
    dt = jnp.exp(ldt_ref[0])
    xr, xi = lre * dt, lim * dt
    tau = lax.broadcasted_iota(jnp.int32, (t_n, 128), 0).astype(F32)
    mag = jnp.exp(tau * xr)
    a_re, a_im = mag * jnp.cos(tau * xi), mag * jnp.sin(tau * xi)
    ar, ai = jnp.exp(xr) * jnp.cos(xi), jnp.exp(xr) * jnp.sin(xi)
    inv = 1.0 / (lre * lre + lim * lim)
    br_, bi_ = lre * inv, -lim * inv
    fr = (ar - 1.0) * br_ - ai * bi_
    fi = (ar - 1.0) * bi_ + ai * br_
    btre, btim = btre_ref[0], btim_ref[0]
    bbr = fr * btre - fi * btim
    bbi = fr * btim + fi * btre
    cre, cim = cre_ref[0], cim_ref[0]

    lane = lax.broadcasted_iota(jnp.int32, (1, 128), 1)
    is_f = lane < SSM_STATE

    def rep_rows(a):
        return jnp.concatenate([jnp.broadcast_to(a[t:t + 1], (hg, 128)) for t in range(t_n)], axis=0)

    def tile_rows(c):
        return jnp.concatenate([c] * t_n, axis=0)

    def rev_rows(a):
        return jnp.concatenate([a[t_n - 1 - t:t_n - t] for t in range(t_n)], axis=0)

    cr_t, ci_t = tile_rows(cre), tile_rows(cim)
    a_re_rep, a_im_rep = rep_rows(a_re), rep_rows(a_im)
    w_re = cr_t * a_re_rep - ci_t * a_im_rep
    w_im = cr_t * a_im_rep + ci_t * a_re_rep
    zero = jnp.zeros_like(w_re)
    e_f = jnp.concatenate([jnp.where(is_f, w_re, zero), jnp.where(is_f, -w_im, zero)], axis=1)
    e_b = jnp.concatenate([jnp.where(is_f, zero, w_re), jnp.where(is_f, zero, -w_im)], axis=1)
    e_b_rev = jnp.concatenate([e_b[(t_n - 1 - t) * hg:(t_n - t) * hg] for t in range(t_n)], axis=0)
    bb_t = tile_rows(jnp.concatenate([bbr, bbi], axis=1))
    row_blk = lax.broadcasted_iota(jnp.int32, (w, w), 0) // hg

    m = jnp.zeros((w, w), F32)
    zrows = lambda n: jnp.zeros((n * hg, w), F32)
    for s in range(t_n):
        down = e_f if s == 0 else jnp.concatenate([zrows(s), e_f[:(t_n - s) * hg]], axis=0)
        k = t_n - 1 - s
        up = e_b_rev if k == 0 else jnp.concatenate([e_b_rev[k * hg:], zrows(k)], axis=0)
        rhs = jnp.where(row_blk == s, bb_t, 0.0)
        m = m + _dot_nt(down + up, rhs, precision=HIGHEST)
    mt_ref[0] = m.T.astype(BF16)

    a_re_rev, a_im_rev = rev_rows(a_re), rev_rows(a_im)
    ps_re = rep_rows(jnp.where(is_f, a_re_rev, a_re))
    ps_im = rep_rows(jnp.where(is_f, a_im_rev, a_im))
    bbr_t, bbi_t = tile_rows(bbr), tile_rows(bbi)
    p_re = ps_re * bbr_t - ps_im * bbi_t
    p_im = ps_re * bbi_t + ps_im * bbr_t
    pt_ref[0] = jnp.concatenate([p_re, p_im], axis=1).astype(BF16)

    a1_re = a_re * ar - a_im * ai
    a1_im = a_re * ai + a_im * ar
    aq_re = rep_rows(jnp.where(is_f, a1_re, rev_rows(a1_re)))
    aq_im = rep_rows(jnp.where(is_f, a1_im, rev_rows(a1_im)))
    q_re = cr_t * aq_re - ci_t * aq_im
    q_im = cr_t * aq_im + ci_t * aq_re
    qt_ref[0] = jnp.concatenate([q_re, -q_im], axis=1).T.astype(BF16)
    at_ref[0] = jnp.concatenate([a1_re[t_n - 1:t_n], a1_im[t_n - 1:t_n]], axis=1)


def _s5_prep(lam_re, lam_im, log_dt, b_re, b_im, c_re, c_im):
    g = SSM_GROUPS
    cat = lambda a: jnp.concatenate([a[0], a[1]], axis=-1)
    lre = cat(lam_re)[:, None, :]
    lim = cat(lam_im)[:, None, :]
    ldt = cat(jnp.broadcast_to(log_dt[..., None], (2, g, SSM_STATE)))[:, None, :]
    btre = cat(jnp.swapaxes(b_re, -1, -2))
    btim = cat(jnp.swapaxes(b_im, -1, -2))
    cre, cim = cat(c_re), cat(c_im)
    vec = pl.BlockSpec((1, 1, 128), lambda i: (i, 0, 0))
    mat = pl.BlockSpec((1, SSM_GROUP_CH, 128), lambda i: (i, 0, 0))
    big = pl.BlockSpec((1, S5_W, S5_W), lambda i: (i, 0, 0))
    return pl.pallas_call(
        _s5_prep_kernel,
        out_shape=[jax.ShapeDtypeStruct((g, S5_W, S5_W), BF16)] * 3
        + [jax.ShapeDtypeStruct((g, 1, 256), F32)],
        grid=(g,),
        in_specs=[vec, vec, vec, mat, mat, mat, mat],
        out_specs=[big, big, big, pl.BlockSpec((1, 1, 256), lambda i: (i, 0, 0))],
        compiler_params=_cparams(("parallel",)),
        name="s5_prep",
    )(lre, lim, ldt, btre, btim, cre, cim)


def _s5_z_kernel(u_ref, pt_ref, z_ref):
    z_ref[0] = _dot(u_ref[0].astype(BF16), pt_ref[0])


def _s5_rec_kernel(z_ref, a_ref, h0_ref, hf_ref, hb_ref, fin_ref, *, n_ctx_seq, ctx_chunks, n_lat_seq,
                   lat_chunks):
    sw = z_ref.shape[-1]
    ng = sw // 256
    lane = lax.broadcasted_iota(jnp.int32, (1, sw), 1)
    is_f = (lane % 128) < SSM_STATE

    def split(row):
        re = jnp.concatenate([row[:, g * 256:g * 256 + 128] for g in range(ng)], axis=1)
        im = jnp.concatenate([row[:, g * 256 + 128:(g + 1) * 256] for g in range(ng)], axis=1)
        return re, im

    def merge(re, im):
        parts = []
        for g in range(ng):
            parts += [re[:, g * 128:(g + 1) * 128], im[:, g * 128:(g + 1) * 128]]
        return jnp.concatenate(parts, axis=1)

    a_re, a_im = split(a_ref[0])

    def run_seq(base, n, seq):
        def step(k, h):
            h_re, h_im = h
            cf = base + k
            cb = base + n - 1 - k
            row = merge(h_re, h_im)
            hf_ref[0, pl.ds(cf, 1), :] = row
            hb_ref[0, pl.ds(cb, 1), :] = row
            z_re, z_im = split(jnp.where(is_f, z_ref[0, pl.ds(cf, 1), :], z_ref[0, pl.ds(cb, 1), :]))
            return (a_re * h_re - a_im * h_im + z_re, a_re * h_im + a_im * h_re + z_im)
        h_re, h_im = lax.fori_loop(0, n, step, split(h0_ref[0, pl.ds(seq, 1), :]))
        fin_ref[0, pl.ds(seq, 1), :] = merge(h_re, h_im)

    def ctx_body(s, carry):
        run_seq(s * ctx_chunks, ctx_chunks, s)
        return carry
    lax.fori_loop(0, n_ctx_seq, ctx_body, 0)

    def lat_body(s, carry):
        run_seq(n_ctx_seq * ctx_chunks + s * lat_chunks, lat_chunks, n_ctx_seq + s)
        return carry
    lax.fori_loop(0, n_lat_seq, lat_body, 0)


def _s5_y_kernel(u_ref, hf_ref, hb_ref, mt_ref, qt_ref, d_ref, y_ref):
    u = u_ref[0]
    lane = lax.broadcasted_iota(jnp.int32, (1, u.shape[-1]), 1)
    is_f = (lane % 128) < SSM_STATE
    h = jnp.where(is_f, hf_ref[0], hb_ref[0]).astype(BF16)
    y_ref[0] = _dot(u.astype(BF16), mt_ref[0]) + _dot(h, qt_ref[0]) + d_ref[0] * u


def _s5_mixer(s_u, mats, ssm_d, h0, n_ctx_seq, ctx_seq, lat_seq):
    mt, pt, qt, at = mats
    g, t_n, w, hg = SSM_GROUPS, S5_T, S5_W, SSM_GROUP_CH
    gh = g // 2
    n_tok = s_u.shape[1]
    nc = n_tok // t_n
    bw = t_n * 128
    sw = gh * w
    n_seq = h0.shape[0]
    ctx_chunks, lat_chunks = ctx_seq // t_n, lat_seq // t_n
    n_lat_seq = n_seq - n_ctx_seq
    u = s_u.reshape(2, nc, bw)
    eye = jnp.eye(gh, dtype=mt.dtype)
    m_big = jnp.einsum('zgshto,gk->zsghtko', mt.reshape(2, gh, t_n, hg, t_n, hg), eye).reshape(2, bw, bw)
    p_big = jnp.einsum('zgshq,gk->zsghkq', pt.reshape(2, gh, t_n, hg, w), eye).reshape(2, bw, sw)
    q_big = jnp.einsum('zgqto,gk->zkqtgo', qt.reshape(2, gh, w, t_n, hg), eye).reshape(2, sw, bw)
    d_big = jnp.tile(ssm_d.reshape(2, 1, 128), (1, 1, t_n))

    rb = math.gcd(nc, 256)
    u_spec = pl.BlockSpec((1, rb, bw), lambda z, i: (z, i, 0))
    s_spec = pl.BlockSpec((1, rb, sw), lambda z, i: (z, i, 0))
    z = pl.pallas_call(
        _s5_z_kernel,
        out_shape=jax.ShapeDtypeStruct((2, nc, sw), F32),
        grid=(2, nc // rb),
        in_specs=[u_spec, pl.BlockSpec((1, bw, sw), lambda z, i: (z, 0, 0))],
        out_specs=s_spec,
        compiler_params=_cparams(("parallel", "parallel"), VMEM_LIMIT),
        name="s5_chunk_inputs",
    )(u, p_big)

    half = lambda rows: pl.BlockSpec((1, rows, sw), lambda z: (z, 0, 0))
    hf, hb, fin = pl.pallas_call(
        functools.partial(_s5_rec_kernel, n_ctx_seq=n_ctx_seq, ctx_chunks=ctx_chunks,
                          n_lat_seq=n_lat_seq, lat_chunks=lat_chunks),
        out_shape=[jax.ShapeDtypeStruct((2, nc, sw), F32)] * 2 + [jax.ShapeDtypeStruct((2, n_seq, sw), F32)],
        grid=(2,),
        in_specs=[half(nc), half(1), half(n_seq)],
        out_specs=[half(nc), half(nc), half(n_seq)],
        compiler_params=_cparams(("parallel",), VMEM_LIMIT),
        name="s5_recurrence",
    )(z, at.reshape(2, 1, sw), h0.reshape(n_seq, 2, sw).transpose(1, 0, 2))

    y = pl.pallas_call(
        _s5_y_kernel,
        out_shape=jax.ShapeDtypeStruct((2, nc, bw), F32),
        grid=(2, nc // rb),
        in_specs=[u_spec, s_spec, s_spec,
                  pl.BlockSpec((1, bw, bw), lambda z, i: (z, 0, 0)),
                  pl.BlockSpec((1, sw, bw), lambda z, i: (z, 0, 0)),
                  pl.BlockSpec((1, 1, bw), lambda z, i: (z, 0, 0))],
        out_specs=u_spec,
        compiler_params=_cparams(("parallel", "parallel"), VMEM_LIMIT),
        name="s5_outputs",
    )(u, hf, hb, m_big, q_big, d_big)
    return y.reshape(2, n_tok, 128), fin.transpose(1, 0, 2).reshape(n_seq, g, w)


DN_TILE = 256


def _dn_pre_kernel(qkv_ref, halo_ref, w_ref, ab_ref, alog_ref, dtb_ref, qkvn_ref, gcb_ref):
    x = qkv_ref[...]
    n = x.shape[0]
    row = lax.broadcasted_iota(jnp.int32, (n, 1), 0)
    halo = halo_ref[0]
    xp = jnp.where(row == 0, halo[0:1], pltpu.roll(x, 1, 0))
    xn = jnp.where(row == n - 1, halo[1:2], pltpu.roll(x, n - 1, 0))
    w = w_ref[...]
    y = _silu(w[0:1] * xp + w[1:2] * x + w[2:3] * xn)
    for hcol in range(2 * DN_HEADS):
        seg = y[:, hcol * DN_HEAD_DIM:(hcol + 1) * DN_HEAD_DIM]
        nrm = seg * lax.rsqrt(jnp.sum(seg * seg, axis=-1, keepdims=True) + EPS)
        if hcol < DN_HEADS:
            nrm = nrm * (DN_HEAD_DIM ** -0.5)
        qkvn_ref[:, hcol * DN_HEAD_DIM:(hcol + 1) * DN_HEAD_DIM] = nrm.astype(BF16)
    qkvn_ref[:, 2 * DN_WIDTH:] = y[:, 2 * DN_WIDTH:].astype(BF16)

    ab = ab_ref[...]
    lane = lax.broadcasted_iota(jnp.int32, (1, 128), 1)
    glog = jnp.where(lane < 2 * DN_HEADS, -jnp.exp(alog_ref[...]) * _softplus(ab + dtb_ref[...]), 0.0)
    beta = _sigmoid(ab)
    ii = lax.broadcasted_iota(jnp.int32, (n, n), 0)
    jj = lax.broadcasted_iota(jnp.int32, (n, n), 1)
    same = (ii // DN_CHUNK) == (jj // DN_CHUNK)
    pre = jnp.where(same & (jj <= ii), 1.0, 0.0).astype(F32)
    suf = jnp.where(same & (jj >= ii), 1.0, 0.0).astype(F32)
    gpre = jnp.dot(pre, glog, preferred_element_type=F32, precision=HIGHEST)
    gsuf = jnp.dot(suf, glog, preferred_element_type=F32, precision=HIGHEST)
    gcb_ref[...] = jnp.where(lane < DN_HEADS, gpre, jnp.where(lane < 2 * DN_HEADS, gsuf, beta))


def _dn_pre(qkv, ab, conv_w, a_log, dt_bias, n_ctx_tok, lat_seq):
    n_tok = qkv.shape[0]
    tm = DN_TILE
    nt = n_tok // tm
    tiles = qkv.reshape(nt, tm, 3 * DN_WIDTH)
    first, last = tiles[:, 0], tiles[:, tm - 1]
    tile_id = jnp.arange(nt)
    n_ctx_tiles = n_ctx_tok // tm
    tps = lat_seq // tm
    lat_pos = (tile_id - n_ctx_tiles) % tps
    has_prev = (tile_id >= n_ctx_tiles) & (lat_pos > 0)
    has_next = (tile_id >= n_ctx_tiles) & (lat_pos < tps - 1)
    prev = jnp.where(has_prev[:, None], jnp.roll(last, 1, axis=0), 0.0)
    nxt = jnp.where(has_next[:, None], jnp.roll(first, -1, axis=0), 0.0)
    halo = jnp.concatenate([prev[:, None], nxt[:, None], jnp.zeros((nt, 6, 3 * DN_WIDTH), F32)], axis=1)
    w8 = jnp.concatenate([conv_w, jnp.zeros((5, 3 * DN_WIDTH), F32)], axis=0)
    pad = lambda v: jnp.concatenate([v.reshape(1, 2 * DN_HEADS), jnp.zeros((1, 120), F32)], axis=1)
    row = lambda i: (i, 0)
    const = lambda i: (0, 0)
    return pl.pallas_call(
        _dn_pre_kernel,
        out_shape=[jax.ShapeDtypeStruct((n_tok, 3 * DN_WIDTH), BF16),
                   jax.ShapeDtypeStruct((n_tok, 128), F32)],
        grid=(nt,),
        in_specs=[pl.BlockSpec((tm, 3 * DN_WIDTH), row),
                  pl.BlockSpec((1, 8, 3 * DN_WIDTH), lambda i: (i, 0, 0)),
                  pl.BlockSpec((8, 3 * DN_WIDTH), const),
                  pl.BlockSpec((tm, 128), row),
                  pl.BlockSpec((1, 128), const),
                  pl.BlockSpec((1, 128), const)],
        out_specs=[pl.BlockSpec((tm, 3 * DN_WIDTH), row), pl.BlockSpec((tm, 128), row)],
        compiler_params=_cparams(("parallel",), VMEM_LIMIT),
        name="dn_pre",
    )(qkv, halo, w8, ab, pad(a_log), pad(dt_bias))


DN_CHAINS = 2 * DN_HEADS


def _split(x):
    hi = x.astype(BF16)
    lo = (x - hi.astype(F32)).astype(BF16)
    return hi, lo


def _split_lhs(a):
    hi, lo = _split(a)
    hif = hi.astype(F32)
    return jnp.concatenate([hif, hif, lo.astype(F32)], axis=1).astype(BF16)


def _split_rhs(b):
    hi, lo = _split(b)
    return jnp.concatenate([hi, lo, hi], axis=0)


def _dn_prepare(q_ref, k_ref, v_ref, gc_ref, gt_ref, row0, chunk_idx):
    c = DN_CHUNK
    ii = lax.broadcasted_iota(jnp.int32, (c, c), 0)
    jj = lax.broadcasted_iota(jnp.int32, (c, c), 1)
    items = []
    for d in range(2):
        rows = pl.ds(row0[d], c)
        gcol = gc_ref[rows, :]
        grow = gt_ref[chunk_idx[d]]
        incl = (ii >= jj) if d == 0 else (ii <= jj)
        strict = (ii > jj) if d == 0 else (ii < jj)
        for h in range(DN_HEADS):
            ch = d * DN_HEADS + h
            cols = slice(h * DN_HEAD_DIM, (h + 1) * DN_HEAD_DIM)
            q, k, v = q_ref[rows, cols], k_ref[rows, cols], v_ref[rows, cols]
            gc_col = gcol[:, ch:ch + 1]
            beta = gcol[:, DN_CHAINS + ch:DN_CHAINS + ch + 1]
            gc_row = grow[ch:ch + 1, :]
            kf = k.astype(F32)
            kb = kf * beta
            decay = jnp.exp(jnp.where(incl, gc_col - gc_row, -jnp.inf))
            egc = jnp.exp(gc_col)
            g_last = gc_col[c - 1:c] if d == 0 else gc_col[0:1]
            items.append(dict(
                q=q, k=k, kb=kb.astype(BF16), decay=decay, incl=incl, strict=strict,
                rhs=jnp.concatenate([kb * egc, v.astype(F32) * beta], axis=1),
                qg=(q.astype(F32) * egc).astype(BF16),
                kt=(kf * jnp.exp(g_last - gc_col)).astype(BF16),
                gl=jnp.broadcast_to(jnp.exp(g_last), (8, 128))))
    for it in items:
        it['a'] = jnp.where(it['strict'], _dot_nt(it['kb'], it['k']) * it['decay'], 0.0)
        it['qk'] = jnp.where(it['incl'], _dot_nt(it['q'], it['k']) * it['decay'], 0.0).astype(BF16)
    power = 1
    while power < c:
        for it in items:
            it['lhs'] = _split_lhs(it['a'])
        for it in items:
            upd = _dot(it['lhs'], _split_rhs(it['rhs'] if power == 1 else it['r']))
            it['r'] = it['rhs'] - upd if power == 1 else it['r'] + upd
        if 2 * power < c:
            for it in items:
                it['a'] = _dot(it['lhs'], _split_rhs(it['a']))
        power *= 2
    return items


def _dn_scan_kernel(*refs, n_chunks, seq_len, seqs_per_blk, has_s0, write_fin):
    refs = list(refs)
    q_ref, k_ref, v_ref, gc_ref, gt_ref = refs[:5]
    pos = 5
    s0_ref = None
    if has_s0:
        s0_ref = refs[pos]
        pos += 1
    o_ref = refs[pos]
    pos += 1
    fin_ref = None
    if write_fin:
        fin_ref = refs[pos]
        pos += 1
    s_ref, w_s, u_s, qk_s, qg_s, kt_s, gl_s = refs[pos:pos + 7]
    n_steps = seqs_per_blk * n_chunks

    o_ref[...] = jnp.zeros_like(o_ref)
    for r in (s_ref, w_s, u_s, qk_s, qg_s, kt_s, gl_s):
        r[...] = jnp.zeros_like(r)

    def locate(step):
        seq = step // n_chunks
        kk = step % n_chunks
        cf = seq * n_chunks + kk
        cb = seq * n_chunks + (n_chunks - 1 - kk)
        rows = [pl.multiple_of(cf * DN_CHUNK, DN_CHUNK), pl.multiple_of(cb * DN_CHUNK, DN_CHUNK)]
        return seq, kk, rows, [cf, cb]

    def body(t, carry):
        tb = jnp.maximum(t, 0)
        valid = t >= 0
        seq, kk, rows_b, _ = locate(tb)
        first = kk == 0
        states, vnews = [], []
        for ch in range(DN_CHAINS):
            d, h = divmod(ch, DN_HEADS)
            s0 = s0_ref[0, d, h] if has_s0 else jnp.zeros((DN_HEAD_DIM, DN_HEAD_DIM), F32)
            states.append(jnp.where(first, s0, s_ref[ch]))
        for ch in range(DN_CHAINS):
            vnews.append((u_s[ch] - _dot(w_s[ch], states[ch].astype(BF16))).astype(BF16))
        tn = jnp.minimum(t + 1, n_steps - 1)
        _, _, rows_a, chunks_a = locate(tn)
        items = _dn_prepare(q_ref, k_ref, v_ref, gc_ref, gt_ref, rows_a, chunks_a)
        for ch in range(DN_CHAINS):
            d, h = divmod(ch, DN_HEADS)
            sb = states[ch].astype(BF16)
            o = _dot(qg_s[ch], sb) + _dot(qk_s[ch], vnews[ch])
            s_new = states[ch] * gl_s[ch][0:1, 0:1] + _dot_tn(kt_s[ch], vnews[ch])
            s_ref[ch] = s_new
            if write_fin:
                fin_ref[seq, d, h] = s_new
            rows = pl.ds(rows_b[d], DN_CHUNK)
            cols = slice(h * DN_HEAD_DIM, (h + 1) * DN_HEAD_DIM)
            o_ref[rows, cols] = o_ref[rows, cols] + jnp.where(valid, o, 0.0)
        for ch, it in enumerate(items):
            w_s[ch] = it['r'][:, :DN_HEAD_DIM].astype(BF16)
            u_s[ch] = it['r'][:, DN_HEAD_DIM:]
            qk_s[ch] = it['qk']
            qg_s[ch] = it['qg']
            kt_s[ch] = it['kt']
            gl_s[ch] = it['gl']
        return carry

    lax.fori_loop(-1, n_steps, body, 0)


def _dn_scan(qkvn, gcb, gcb_t, seq_len, n_seq, seqs_per_blk, first_tok, s0, write_fin):
    n_chunks = seq_len // DN_CHUNK
    rows_blk = seqs_per_blk * seq_len
    b0 = first_tok // rows_blk
    hd = DN_HEAD_DIM
    st_spec = pl.BlockSpec((seqs_per_blk, 2, DN_HEADS, hd, hd), lambda s: (s, 0, 0, 0, 0))
    in_specs = [pl.BlockSpec((rows_blk, DN_WIDTH), lambda s: (b0 + s, 0)),
                pl.BlockSpec((rows_blk, DN_WIDTH), lambda s: (b0 + s, 1)),
                pl.BlockSpec((rows_blk, DN_WIDTH), lambda s: (b0 + s, 2)),
                pl.BlockSpec((rows_blk, 128), lambda s: (b0 + s, 0)),
                pl.BlockSpec((rows_blk // DN_CHUNK, 16, DN_CHUNK), lambda s: (b0 + s, 0, 0))]
    args = [qkvn, qkvn, qkvn, gcb, gcb_t]
    if s0 is not None:
        assert seqs_per_blk == 1
        in_specs.append(st_spec)
        args.append(s0)
    out_shape = [jax.ShapeDtypeStruct((n_seq * seq_len, DN_WIDTH), F32)]
    out_specs = [pl.BlockSpec((rows_blk, DN_WIDTH), lambda s: (s, 0))]
    if write_fin:
        out_shape.append(jax.ShapeDtypeStruct((n_seq, 2, DN_HEADS, hd, hd), F32))
        out_specs.append(st_spec)
    c = DN_CHUNK
    scratch = [pltpu.VMEM((DN_CHAINS, hd, hd), F32),
               pltpu.VMEM((DN_CHAINS, c, hd), BF16),
               pltpu.VMEM((DN_CHAINS, c, hd), F32),
               pltpu.VMEM((DN_CHAINS, c, c), BF16),
               pltpu.VMEM((DN_CHAINS, c, hd), BF16),
               pltpu.VMEM((DN_CHAINS, c, hd), BF16),
               pltpu.VMEM((DN_CHAINS, 8, 128), F32)]
    return pl.pallas_call(
        functools.partial(_dn_scan_kernel, n_chunks=n_chunks, seq_len=seq_len, seqs_per_blk=seqs_per_blk,
                          has_s0=s0 is not None, write_fin=write_fin),
        out_shape=out_shape,
        grid=(n_seq // seqs_per_blk,),
        in_specs=in_specs,
        out_specs=out_specs,
        scratch_shapes=scratch,
        compiler_params=_cparams(("parallel",), VMEM_LIMIT),
        name="dn_scan_%d" % seq_len,
    )(*args)


def _deltanet(qkv, ab, p, s0_lat, n_ctx_seq, ctx_seq, n_lat_seq, lat_seq):
    n_ctx_tok = n_ctx_seq * ctx_seq
    qkvn, gcb = _dn_pre(qkv, ab, p['dn_conv'], p['dn_a_log'], p['dn_dt_bias'], n_ctx_tok, lat_seq)
    n_tok = qkv.shape[0]
    gcb_t = gcb[:, :16].reshape(n_tok // DN_CHUNK, DN_CHUNK, 16).transpose(0, 2, 1)
    ctx_blk = math.gcd(n_ctx_seq, 8)
    o_ctx, fin = _dn_scan(qkvn, gcb, gcb_t, ctx_seq, n_ctx_seq, ctx_blk, 0, None, True)
    (o_lat,) = _dn_scan(qkvn, gcb, gcb_t, lat_seq, n_lat_seq, 1, n_ctx_tok, s0_lat, False)
    return o_ctx, o_lat, fin


MIX_TILE = 256


def _mix_kernel(x_ref, mod_ref, ys_ref, octx_ref, olat_ref, z_ref, cu_ref, cv_ref,
                wglu_ref, bglu_ref, dnn_ref, cmn_ref, ws_ref, bs_ref, wo_ref, n2_ref,
                x1_ref, h2_ref, *, n_ctx_tiles):
    mod = mod_ref[0]
    gate1 = mod[:, 2 * D_MODEL:3 * D_MODEL]
    shift2 = mod[:, 3 * D_MODEL:4 * D_MODEL]
    scale2 = mod[:, 4 * D_MODEL:5 * D_MODEL]
    ya = _gelu(jnp.concatenate([ys_ref[0], ys_ref[1]], axis=1))
    ya = ya * _sigmoid(_dot(ya.astype(BF16), wglu_ref[...]) + bglu_ref[...])
    mix = _dot(ya.astype(BF16), wo_ref[0:SSM_CH, :])
    o = jnp.where(pl.program_id(0) < n_ctx_tiles, octx_ref[...], olat_ref[...])
    z = z_ref[...]
    parts = []
    for h in range(DN_HEADS):
        sl = slice(h * DN_HEAD_DIM, (h + 1) * DN_HEAD_DIM)
        parts.append(_rms(o[:, sl]) * dnn_ref[...] * _silu(z[:, sl]))
    yb = jnp.concatenate(parts, axis=1)
    mix = mix + _dot(yb.astype(BF16), wo_ref[SSM_CH:SSM_CH + DN_WIDTH, :])
    u = _gelu(cu_ref[...])
    v = _gelu(cv_ref[...])
    seg_i = lax.broadcasted_iota(jnp.int32, (CM_WIDTH, CM_WIDTH), 0) // CM_HEAD_DIM
    seg_j = lax.broadcasted_iota(jnp.int32, (CM_WIDTH, CM_WIDTH), 1) // CM_HEAD_DIM
    seg_ones = jnp.where(seg_i == seg_j, 1.0, 0.0).astype(BF16)
    vv = v * v
    vv_hi = vv.astype(BF16)
    vv_lo = (vv - vv_hi.astype(F32)).astype(BF16)
    ms = (_dot(vv_hi, seg_ones) + _dot(vv_lo, seg_ones)) * (1.0 / CM_HEAD_DIM)
    vn = (v * lax.rsqrt(ms + EPS) * cmn_ref[...]).astype(BF16)
    head_of_lane = lax.broadcasted_iota(jnp.int32, (1, CM_WIDTH), 1) // CM_HEAD_DIM
    ycs = []
    for c in range(MIX_TILE // CM_CHUNK):
        vc = vn[c * CM_CHUNK:(c + 1) * CM_CHUNK]
        stack = jnp.concatenate([jnp.where(head_of_lane == h, vc, jnp.zeros_like(vc))
                                 for h in range(CM_HEADS)], axis=0)
        vs = _dot(ws_ref[...], stack) + bs_ref[...]
        ycs.append(u[c * CM_CHUNK:(c + 1) * CM_CHUNK] * vs)
    yc = jnp.concatenate(ycs, axis=0)
    mix = mix + _dot(yc.astype(BF16), wo_ref[SSM_CH + DN_WIDTH:, :])
    x1 = x_ref[...] + gate1 * mix
    x1_ref[...] = x1
    h2_ref[...] = (_rms(x1) * n2_ref[...] * (1.0 + scale2) + shift2).astype(BF16)


def _mix_out(x, mods_l, ys, o_ctx, o_lat, z, cu, cv, p, n_ctx_tok, lat_seq):
    n_tok = x.shape[0]
    tm = MIX_TILE
    n_ctx_tiles = n_ctx_tok // tm
    tps = lat_seq // tm
    row = lambda i: (i, 0)
    const = lambda i: (0, 0)
    ctx_row = lambda i: (jnp.minimum(i, n_ctx_tiles - 1), 0)
    lat_row = lambda i: (jnp.maximum(i - n_ctx_tiles, 0), 0)
    ws_cat = jnp.concatenate([p['cm_w_s'][h] for h in range(CM_HEADS)], axis=1).astype(BF16)
    bs_exp = jnp.repeat(p['cm_b_s'].T, CM_HEAD_DIM, axis=1)
    return pl.pallas_call(
        functools.partial(_mix_kernel, n_ctx_tiles=n_ctx_tiles),
        out_shape=[jax.ShapeDtypeStruct((n_tok, D_MODEL), F32),
                   jax.ShapeDtypeStruct((n_tok, D_MODEL), BF16)],
        grid=(n_tok // tm,),
        in_specs=[pl.BlockSpec((tm, D_MODEL), row),
                  pl.BlockSpec((1, 1, N_MOD * D_MODEL),
                               lambda i: (_mod_row_index(i, n_ctx_tiles, tps), 0, 0)),
                  pl.BlockSpec((2, tm, 128), lambda i: (0, i, 0)),
                  pl.BlockSpec((tm, DN_WIDTH), ctx_row),
                  pl.BlockSpec((tm, DN_WIDTH), lat_row),
                  pl.BlockSpec((tm, DN_WIDTH), row),
                  pl.BlockSpec((tm, CM_WIDTH), row),
                  pl.BlockSpec((tm, CM_WIDTH), row),
                  pl.BlockSpec((SSM_CH, SSM_CH), const),
                  pl.BlockSpec((1, SSM_CH), const),
                  pl.BlockSpec((1, DN_HEAD_DIM), const),
                  pl.BlockSpec((1, CM_WIDTH), const),
                  pl.BlockSpec((CM_CHUNK, CM_HEADS * CM_CHUNK), const),
                  pl.BlockSpec((CM_CHUNK, CM_WIDTH), const),
                  pl.BlockSpec((D_MODEL, D_MODEL), const),
                  pl.BlockSpec((1, D_MODEL), const)],
        out_specs=[pl.BlockSpec((tm, D_MODEL), row), pl.BlockSpec((tm, D_MODEL), row)],
        compiler_params=_cparams(("parallel",), VMEM_LIMIT),
        name="mix_out",
    )(x, mods_l, ys, o_ctx, o_lat, z, cu, cv,
      p['ssm_w_glu'].astype(BF16), p['ssm_b_glu'].reshape(1, SSM_CH),
      p['dn_norm'].reshape(1, DN_HEAD_DIM), p['cm_norm'].reshape(1, CM_WIDTH),
      ws_cat, bs_exp, p['w_out'].astype(BF16), p['norm2'].reshape(1, D_MODEL))


ROUTE_TILE = 512


def _first_max_mask(x, idx, axis):
    m = jnp.max(x, axis=axis, keepdims=True)
    first = jnp.min(jnp.where(x == m, idx, jnp.int32(1 << 20)), axis=axis, keepdims=True)
    return idx == first


def _router_kernel(h_ref, wr_ref, bias_ref, g_ref):
    n = h_ref.shape[0]
    per = N_EXPERTS // N_ROUTE_GROUPS
    logits = _dot_nt(wr_ref[...], h_ref[...])
    scores = _sigmoid(logits)
    choice = scores + bias_ref[...]
    ninf = jnp.float32(-jnp.inf)
    c3 = choice.reshape(N_ROUTE_GROUPS, per, n)
    i3 = lax.broadcasted_iota(jnp.int32, c3.shape, 1)
    top1 = _first_max_mask(c3, i3, 1)
    m1 = jnp.max(c3, axis=1)
    m2 = jnp.max(jnp.where(top1, ninf, c3), axis=1)
    grp = m1 + m2
    gi = lax.broadcasted_iota(jnp.int32, grp.shape, 0)
    gsel = jnp.zeros(grp.shape, jnp.bool_)
    work = grp
    for _ in range(TOPK_ROUTE_GROUPS):
        pick = _first_max_mask(work, gi, 0)
        gsel = gsel | pick
        work = jnp.where(pick, ninf, work)
    gmask = jnp.broadcast_to(gsel[:, None, :], c3.shape).reshape(N_EXPERTS, n)
    work = jnp.where(gmask, choice, ninf)
    ei = lax.broadcasted_iota(jnp.int32, work.shape, 0)
    esel = jnp.zeros(work.shape, jnp.bool_)
    for _ in range(TOP_K):
        pick = _first_max_mask(work, ei, 0)
        esel = esel | pick
        work = jnp.where(pick, ninf, work)
    wts = jnp.where(esel, scores, 0.0)
    g_ref[...] = wts / (jnp.sum(wts, axis=0, keepdims=True) + 1e-20) * ROUTED_SCALE


def _router(h2, router_w, router_bias):
    n_tok = h2.shape[0]
    tm = ROUTE_TILE
    return pl.pallas_call(
        _router_kernel,
        out_shape=jax.ShapeDtypeStruct((N_EXPERTS, n_tok), F32),
        grid=(n_tok // tm,),
        in_specs=[pl.BlockSpec((tm, D_MODEL), lambda i: (i, 0)),
                  pl.BlockSpec((N_EXPERTS, D_MODEL), lambda i: (0, 0)),
                  pl.BlockSpec((N_EXPERTS, 1), lambda i: (0, 0))],
        out_specs=pl.BlockSpec((N_EXPERTS, tm), lambda i: (0, i)),
        compiler_params=_cparams(("parallel",)),
        name="router",
    )(h2, router_w.T.astype(BF16), router_bias.reshape(N_EXPERTS, 1))


MOE_TILE = 1024
MOE_EG = 8
MOE_ROWS = 256


def _moe_kernel(h_ref, x_ref, mod_ref, g_ref, wg_ref, wu_ref, wd_ref, sg_ref, su_ref, sd_ref, nf_ref,
                *out_refs, final):
    j = pl.program_id(1)
    nj = pl.num_programs(1)
    acc_ref = out_refs[-1]
    h = h_ref[...]

    @pl.when(j == 0)
    def _():
        sh = _silu(_dot(h, sg_ref[...])) * _dot(h, su_ref[...])
        acc_ref[...] = _dot(sh.astype(BF16), sd_ref[...])

    for r in range(h_ref.shape[0] // MOE_ROWS):
        rows = slice(r * MOE_ROWS, (r + 1) * MOE_ROWS)
        hr = h_ref[rows, :]
        hg = _dot(hr, wg_ref[...])
        hu = _dot(hr, wu_ref[...])
        gates = g_ref[0, rows, :]
        cols = []
        for e in range(MOE_EG):
            sl = slice(e * EXPERT_FF, (e + 1) * EXPERT_FF)
            cols.append((_silu(hg[:, sl]) * hu[:, sl] * gates[:, e:e + 1]).astype(BF16))
        act = jnp.concatenate(cols, axis=1)
        acc_ref[rows, :] += _dot(act, wd_ref[...])

    @pl.when(j == nj - 1)
    def _():
        gate2 = mod_ref[0][:, 5 * D_MODEL:6 * D_MODEL]
        x2 = x_ref[...] + gate2 * acc_ref[...]
        if final:
            out_refs[0][...] = _rms(x2) * nf_ref[...]
        else:
            out_refs[0][...] = x2


def _expert_cols_kernel(w_ref, o_ref):
    for e in range(w_ref.shape[0]):
        o_ref[:, e * EXPERT_FF:(e + 1) * EXPERT_FF] = w_ref[e].astype(BF16)


def _expert_cols(w):
    n_e, d, f = w.shape
    return pl.pallas_call(
        _expert_cols_kernel,
        out_shape=jax.ShapeDtypeStruct((d, n_e * f), BF16),
        grid=(n_e // MOE_EG,),
        in_specs=[pl.BlockSpec((MOE_EG, d, f), lambda j: (j, 0, 0))],
        out_specs=pl.BlockSpec((d, MOE_EG * f), lambda j: (0, j)),
        compiler_params=_cparams(("parallel",)),
        name="expert_cols",
    )(w)


def _cast_rows_kernel(w_ref, o_ref):
    o_ref[...] = w_ref[...].astype(BF16)


def _cast_rows(w, rows):
    r, c = w.shape
    return pl.pallas_call(
        _cast_rows_kernel,
        out_shape=jax.ShapeDtypeStruct((r, c), BF16),
        grid=(r // rows,),
        in_specs=[pl.BlockSpec((rows, c), lambda j: (j, 0))],
        out_specs=pl.BlockSpec((rows, c), lambda j: (j, 0)),
        compiler_params=_cparams(("parallel",)),
        name="cast_rows",
    )(w)


def _moe(h2, x1, mods_l, gates_t, p, norm_f, final, n_ctx_tok, lat_seq):
    n_tok = h2.shape[0]
    tm = MOE_TILE
    n_ctx_tiles = n_ctx_tok // tm
    tps = lat_seq // tm
    neg = N_EXPERTS // MOE_EG
    ef = MOE_EG * EXPERT_FF
    gates = gates_t.reshape(neg, MOE_EG, n_tok).transpose(0, 2, 1)
    wg = _expert_cols(p['moe_w_gate'])
    wu = _expert_cols(p['moe_w_up'])
    wd = _cast_rows(p['moe_w_down'].reshape(N_EXPERTS * EXPERT_FF, D_MODEL), 1024)
    row = lambda i, j: (i, 0)
    const = lambda i, j: (0, 0)
    out = pl.pallas_call(
        functools.partial(_moe_kernel, final=final),
        out_shape=jax.ShapeDtypeStruct((n_tok, D_MODEL), F32),
        grid=(n_tok // tm, neg),
        in_specs=[pl.BlockSpec((tm, D_MODEL), row),
                  pl.BlockSpec((tm, D_MODEL), row),
                  pl.BlockSpec((1, 1, N_MOD * D_MODEL),
                               lambda i, j: (_mod_row_index(i, n_ctx_tiles, tps), 0, 0)),
                  pl.BlockSpec((1, tm, MOE_EG), lambda i, j: (j, i, 0)),
                  pl.BlockSpec((D_MODEL, ef), lambda i, j: (0, j)),
                  pl.BlockSpec((D_MODEL, ef), lambda i, j: (0, j)),
                  pl.BlockSpec((ef, D_MODEL), lambda i, j: (j, 0)),
                  pl.BlockSpec((D_MODEL, SHARED_FF), const),
                  pl.BlockSpec((D_MODEL, SHARED_FF), const),
                  pl.BlockSpec((SHARED_FF, D_MODEL), const),
                  pl.BlockSpec((1, D_MODEL), const)],
        out_specs=pl.BlockSpec((tm, D_MODEL), row),
        scratch_shapes=[pltpu.VMEM((tm, D_MODEL), F32)],
        compiler_params=_cparams(("parallel", "arbitrary"), VMEM_LIMIT),
        name="moe",
    )(h2, x1, mods_l, gates, wg, wu, wd,
      p['moe_ws_gate'].astype(BF16), p['moe_ws_up'].astype(BF16), p['moe_ws_down'].astype(BF16),
      norm_f.reshape(1, D_MODEL))
    return out


def kernel(x_prompt, x_sample, state_ssm, state_delta, c, c_ctx, w_ada, b_ada, norm1, norm2, w_in, w_out, ssm_lam_re, ssm_lam_im, ssm_log_dt, ssm_b_re, ssm_b_im, ssm_c_re, ssm_c_im, ssm_d, ssm_w_glu, ssm_b_glu, dn_conv, dn_a_log, dn_dt_bias, dn_norm, cm_norm, cm_w_s, cm_b_s, moe_router, moe_router_bias, moe_w_gate, moe_w_up, moe_w_down, moe_ws_gate, moe_ws_up, moe_ws_down, norm_f):
    n_ctx_seq, ctx_seq, _ = x_prompt.shape
    n_lat_seq, lat_seq, _ = x_sample.shape
    depth = w_ada.shape[0]
    n_ctx_tok = n_ctx_seq * ctx_seq
    n_seq = n_ctx_seq + n_lat_seq
    g = SSM_GROUPS

    x = jnp.concatenate([x_prompt.reshape(n_ctx_tok, D_MODEL),
                         x_sample.reshape(n_lat_seq * lat_seq, D_MODEL)], axis=0)
    cvec = jnp.concatenate([c_ctx[None, :], c, jnp.zeros((8 - 1 - n_lat_seq, D_MODEL), F32)], axis=0)
    mods = _ada_mods(cvec, w_ada, b_ada)

    ssm_states, dn_states = [], []
    y = None
    for l in range(depth):
        p = {'w_out': w_out[l], 'norm2': norm2[l], 'ssm_w_glu': ssm_w_glu[l], 'ssm_b_glu': ssm_b_glu[l],
             'dn_conv': dn_conv[l], 'dn_a_log': dn_a_log[l], 'dn_dt_bias': dn_dt_bias[l],
             'dn_norm': dn_norm[l], 'cm_norm': cm_norm[l], 'cm_w_s': cm_w_s[l], 'cm_b_s': cm_b_s[l],
             'moe_w_gate': moe_w_gate[l], 'moe_w_up': moe_w_up[l], 'moe_w_down': moe_w_down[l],
             'moe_ws_gate': moe_ws_gate[l], 'moe_ws_up': moe_ws_up[l], 'moe_ws_down': moe_ws_down[l]}
        mods_l = mods[l].reshape(8, 1, N_MOD * D_MODEL)
        wi = w_in[l]
        c_ab = SSM_CH + 4 * DN_WIDTH
        w_main = jnp.concatenate([wi[:, :c_ab], wi[:, c_ab + 16:]], axis=1).astype(BF16)
        w_ab = jnp.concatenate([wi[:, c_ab:c_ab + 16], jnp.zeros((D_MODEL, 112), F32)], axis=1).astype(BF16)
        s_u, qkv, z, cu, cv, ab = _in_projection(x, mods_l, norm1[l], w_main, w_ab, n_ctx_tok, lat_seq)

        mats = _s5_prep(ssm_lam_re[l], ssm_lam_im[l], ssm_log_dt[l], ssm_b_re[l], ssm_b_im[l],
                        ssm_c_re[l], ssm_c_im[l])
        h0_lat = state_ssm[:, l].transpose(0, 3, 2, 1, 4).reshape(n_lat_seq, g, 256)
        h0 = jnp.concatenate([jnp.zeros((n_ctx_seq, g, 256), F32), h0_lat], axis=0)
        ys, ssm_fin = _s5_mixer(s_u, mats, ssm_d[l], h0, n_ctx_seq, ctx_seq, lat_seq)
        ssm_states.append(ssm_fin[:n_ctx_seq].reshape(n_ctx_seq, g, 2, 2, SSM_STATE).transpose(0, 3, 2, 1, 4))

        o_ctx, o_lat, dn_fin = _deltanet(qkv, ab, p, state_delta[:, l], n_ctx_seq, ctx_seq, n_lat_seq, lat_seq)
        dn_states.append(dn_fin)

        x1, h2 = _mix_out(x, mods_l, ys, o_ctx, o_lat, z, cu, cv, p, n_ctx_tok, lat_seq)
        gates_t = _router(h2, moe_router[l], moe_router_bias[l])
        x = _moe(h2, x1, mods_l, gates_t, p, norm_f, l == depth - 1, n_ctx_tok, lat_seq)

    y_prompt = x[:n_ctx_tok].reshape(x_prompt.shape)
    y_sample = x[n_ctx_tok:].reshape(x_sample.shape)
    new_state_ssm = jnp.stack(ssm_states, axis=1)
    new_state_delta = jnp.stack(dn_states, axis=1)
    return (y_prompt, y_sample, new_state_ssm, new_state_delta)
```

```python
import functools
import math

import jax
import jax.numpy as jnp
from jax import lax
from jax.experimental import pallas as pl
from jax.experimental.pallas import tpu as pltpu

F32 = jnp.float32
BF16 = jnp.bfloat16
HIGHEST = lax.Precision.HIGHEST

D_MODEL = 1024
N_MOD = 6
EPS = 1e-6
SSM_CH = 256
SSM_GROUPS = 16
SSM_GROUP_CH = 16
SSM_STATE = 64
S5_T = 16
S5_W = S5_T * SSM_GROUP_CH
DN_HEADS = 4
DN_HEAD_DIM = 128
DN_WIDTH = 512
DN_CHUNK = 64
CM_HEADS = 4
CM_WIDTH = 256
CM_HEAD_DIM = 64
CM_CHUNK = 128
N_EXPERTS = 64
TOP_K = 6
N_ROUTE_GROUPS = 8
TOPK_ROUTE_GROUPS = 4
EXPERT_FF = 128
SHARED_FF = 256
ROUTED_SCALE = 2.5
MAIN_COLS = SSM_CH + 3 * DN_WIDTH + DN_WIDTH + 2 * CM_WIDTH
VMEM_LIMIT = 56 * 1024 * 1024


def _cparams(sem, vmem=None):
    return pltpu.CompilerParams(dimension_semantics=sem, vmem_limit_bytes=vmem)


def _dot(a, b):
    return jnp.dot(a, b, preferred_element_type=F32)


def _dot_nt(a, b, precision=None):
    return lax.dot_general(a, b, (((1,), (1,)), ((), ())), preferred_element_type=F32,
                           precision=precision)


def _dot_tn(a, b):
    return lax.dot_general(a, b, (((0,), (0,)), ((), ())), preferred_element_type=F32)


def _sigmoid(x):
    return 1.0 / (1.0 + jnp.exp(-x))


def _silu(x):
    return x * _sigmoid(x)


def _gelu(x):
    c = math.sqrt(2.0 / math.pi)
    return 0.5 * x * (1.0 + jnp.tanh(c * (x + 0.044715 * (x * x * x))))


def _softplus(x):
    return jnp.maximum(x, 0.0) + jnp.log(1.0 + jnp.exp(-jnp.abs(x)))


def _rms(x):
    return x * lax.rsqrt(jnp.mean(x * x, axis=-1, keepdims=True) + EPS)


def _mod_row_index(tile, n_ctx_tiles, tiles_per_lat_seq):
    return jnp.where(tile < n_ctx_tiles, 0, 1 + (tile - n_ctx_tiles) // tiles_per_lat_seq)


def _ada_kernel(c_ref, w_ref, b_ref, o_ref):
    h = _silu(c_ref[...])
    o_ref[0] = jnp.dot(h, w_ref[0], preferred_element_type=F32, precision=HIGHEST) + b_ref[0]


def _ada_mods(cvec, w_ada, b_ada):
    depth, d, n = w_ada.shape
    bn = 1536
    return pl.pallas_call(
        _ada_kernel,
        out_shape=jax.ShapeDtypeStruct((depth, 8, n), F32),
        grid=(depth, n // bn),
        in_specs=[pl.BlockSpec((8, d), lambda l, j: (0, 0)),
                  pl.BlockSpec((1, d, bn), lambda l, j: (l, 0, j)),
                  pl.BlockSpec((1, 1, bn), lambda l, j: (l, 0, j))],
        out_specs=pl.BlockSpec((1, 8, bn), lambda l, j: (l, 0, j)),
        compiler_params=_cparams(("parallel", "parallel")),
        name="ada_mods",
    )(cvec, w_ada, b_ada.reshape(depth, 1, n))


def _token_specs(xs, tm, n_ctx_tiles, width):
    if len(xs) == 1:
        return [pl.BlockSpec((tm, width), lambda i, *_: (i, 0))]
    return [pl.BlockSpec((tm, width), lambda i, *_: (jnp.minimum(i, n_ctx_tiles - 1), 0)),
            pl.BlockSpec((tm, width), lambda i, *_: (jnp.maximum(i - n_ctx_tiles, 0), 0))]


def _token_tile(refs, n_ctx_tiles):
    if len(refs) == 1:
        return refs[0][...]
    return jnp.where(pl.program_id(0) < n_ctx_tiles, refs[0][...], refs[1][...])


def _inproj_kernel(*refs, n_x, n_ctx_tiles):
    x_refs = refs[:n_x]
    mod_ref, g_ref, w_ref, su_ref, qkv_ref, z_ref, cu_ref, cv_ref, ab_ref, scr = refs[n_x:]
    mod = mod_ref[0]
    shift = mod[:, 0:D_MODEL]
    scale = mod[:, D_MODEL:2 * D_MODEL]
    h = _rms(_token_tile(x_refs, n_ctx_tiles)) * g_ref[...] * (1.0 + scale) + shift
    p = _dot(h.astype(BF16), w_ref[...])
    c0 = SSM_CH
    c1 = c0 + 3 * DN_WIDTH
    c2 = c1 + DN_WIDTH
    c3 = c2 + 4 * DN_HEADS
    c4 = c3 + CM_WIDTH
    qkv_ref[...] = p[:, c0:c1]
    z_ref[...] = p[:, c1:c2]
    ab_ref[...] = p[:, c2:c2 + 128]
    cu_ref[...] = p[:, c3:c4]
    cv_ref[...] = p[:, c4:c4 + CM_WIDTH]
    scr[0] = p[:, 0:128]
    scr[1] = p[:, 128:c0]
    n_rows = su_ref.shape[1]
    for half in range(2):
        for t in range(S5_T):
            su_ref[half, :, t * 128:(t + 1) * 128] = scr[half, pl.ds(t, n_rows, stride=S5_T), :]


def _in_projection(xs, mods_l, norm1, w_in_bf, n_ctx_tok, lat_seq):
    n_tok = sum(x.shape[0] for x in xs)
    tm = 512
    n_ctx_tiles = n_ctx_tok // tm
    tps = lat_seq // tm
    row = lambda i: (i, 0)
    const = lambda i: (0, 0)
    in_cols = w_in_bf.shape[1]
    outs = [(3 * DN_WIDTH, F32), (DN_WIDTH, F32), (CM_WIDTH, F32), (CM_WIDTH, F32), (128, F32)]
    return pl.pallas_call(
        functools.partial(_inproj_kernel, n_x=len(xs), n_ctx_tiles=n_ctx_tiles),
        out_shape=[jax.ShapeDtypeStruct((2, n_tok // S5_T, S5_T * 128), F32)]
        + [jax.ShapeDtypeStruct((n_tok, w), dt) for w, dt in outs],
        grid=(n_tok // tm,),
        in_specs=_token_specs(xs, tm, n_ctx_tiles, D_MODEL)
        + [pl.BlockSpec((1, 1, N_MOD * D_MODEL),
                        lambda i: (_mod_row_index(i, n_ctx_tiles, tps), 0, 0)),
           pl.BlockSpec((1, D_MODEL), const),
           pl.BlockSpec((D_MODEL, in_cols), const)],
        out_specs=[pl.BlockSpec((2, tm // S5_T, S5_T * 128), lambda i: (0, i, 0))]
        + [pl.BlockSpec((tm, w), row) for w, _ in outs],
        scratch_shapes=[pltpu.VMEM((2, tm, 128), F32)],
        compiler_params=_cparams(("parallel",), VMEM_LIMIT),
        name="in_projection",
    )(*xs, mods_l, norm1.reshape(1, D_MODEL), w_in_bf)


def _s5_prep_kernel(lre_ref, lim_ref, ldt_ref, btre_ref, btim_ref, cre_ref, cim_ref,
                    mt_ref, pt_ref, qt_ref, at_ref):
    t_n, hg, w = S5_T, SSM_GROUP_CH, S5_W
    lre, lim = lre_ref[0], lim_ref[0]
    dt = jnp.exp(ldt_ref[0])
    xr, xi = lre * dt, lim * dt
    tau = lax.broadcasted_iota(jnp.int32, (t_n, 128), 0).astype(F32)
    mag = jnp.exp(tau * xr)
    a_re, a_im = mag * jnp.cos(tau * xi), mag * jnp.sin(tau * xi)
    ar, ai = jnp.exp(xr) * jnp.cos(xi), jnp.exp(xr) * jnp.sin(xi)
    inv = 1.0 / (lre * lre + lim * lim)
    br_, bi_ = lre * inv, -lim * inv
    fr = (ar - 1.0) * br_ - ai * bi_
    fi = (ar - 1.0) * bi_ + ai * br_
    btre, btim = btre_ref[0], btim_ref[0]
    bbr = fr * btre - fi * btim
    bbi = fr * btim + fi * btre
    cre, cim = cre_ref[0], cim_ref[0]

    lane = lax.broadcasted_iota(jnp.int32, (1, 128), 1)
    is_f = lane < SSM_STATE

    def rep_rows(a):
        return jnp.concatenate([jnp.broadcast_to(a[t:t + 1], (hg, 128)) for t in range(t_n)], axis=0)

    def tile_rows(c):
        return jnp.concatenate([c] * t_n, axis=0)

    def rev_rows(a):
        return jnp.concatenate([a[t_n - 1 - t:t_n - t] for t in range(t_n)], axis=0)

    cr_t, ci_t = tile_rows(cre), tile_rows(cim)
    a_re_rep, a_im_rep = rep_rows(a_re), rep_rows(a_im)
    w_re = cr_t * a_re_rep - ci_t * a_im_rep
    w_im = cr_t * a_im_rep + ci_t * a_re_rep
    zero = jnp.zeros_like(w_re)
    e_f = jnp.concatenate([jnp.where(is_f, w_re, zero), jnp.where(is_f, -w_im, zero)], axis=1)
    e_b = jnp.concatenate([jnp.where(is_f, zero, w_re), jnp.where(is_f, zero, -w_im)], axis=1)
    e_b_rev = jnp.concatenate([e_b[(t_n - 1 - t) * hg:(t_n - t) * hg] for t in range(t_n)], axis=0)
    bb_t = tile_rows(jnp.concatenate([bbr, bbi], axis=1))
    row_blk = lax.broadcasted_iota(jnp.int32, (w, w), 0) // hg

    m = jnp.zeros((w, w), F32)
    zrows = lambda n: jnp.zeros((n * hg, w), F32)
    for s in range(t_n):
        down = e_f if s == 0 else jnp.concatenate([zrows(s), e_f[:(t_n - s) * hg]], axis=0)
        k = t_n - 1 - s
        up = e_b_rev if k == 0 else jnp.concatenate([e_b_rev[k * hg:], zrows(k)], axis=0)
        rhs = jnp.where(row_blk == s, bb_t, 0.0)
        m = m + _dot_nt(down + up, rhs, precision=HIGHEST)
    mt_ref[...] = m.T.reshape(t_n, hg, w).astype(BF16)

    a_re_rev, a_im_rev = rev_rows(a_re), rev_rows(a_im)
    ps_re = rep_rows(jnp.where(is_f, a_re_rev, a_re))
    ps_im = rep_rows(jnp.where(is_f, a_im_rev, a_im))
    bbr_t, bbi_t = tile_rows(bbr), tile_rows(bbi)
    p_re = ps_re * bbr_t - ps_im * bbi_t
    p_im = ps_re * bbi_t + ps_im * bbr_t
    pt_ref[...] = jnp.concatenate([p_re, p_im], axis=1).reshape(t_n, hg, w).astype(BF16)

    a1_re = a_re * ar - a_im * ai
    a1_im = a_re * ai + a_im * ar
    aq_re = rep_rows(jnp.where(is_f, a1_re, rev_rows(a1_re)))
    aq_im = rep_rows(jnp.where(is_f, a1_im, rev_rows(a1_im)))
    q_re = cr_t * aq_re - ci_t * aq_im
    q_im = cr_t * aq_im + ci_t * aq_re
    qt_ref[0] = jnp.concatenate([q_re, -q_im], axis=1).T.astype(BF16)
    at_ref[0] = jnp.concatenate([a1_re[t_n - 1:t_n], a1_im[t_n - 1:t_n]], axis=1)


def _s5_prep(lam_re, lam_im, log_dt, b_re, b_im, c_re, c_im):
    g = SSM_GROUPS
    cat = lambda a: jnp.concatenate([a[0], a[1]], axis=-1)
    lre = cat(lam_re)[:, None, :]
    lim = cat(lam_im)[:, None, :]
    ldt = cat(jnp.broadcast_to(log_dt[..., None], (2, g, SSM_STATE)))[:, None, :]
    btre = cat(jnp.swapaxes(b_re, -1, -2))
    btim = cat(jnp.swapaxes(b_im, -1, -2))
    cre, cim = cat(c_re), cat(c_im)
    vec = pl.BlockSpec((1, 1, 128), lambda i: (i, 0, 0))
    mat = pl.BlockSpec((1, SSM_GROUP_CH, 128), lambda i: (i, 0, 0))
    big = pl.BlockSpec((1, S5_W, S5_W), lambda i: (i, 0, 0))
    gh = g // 2
    arranged = pl.BlockSpec((None, S5_T, None, SSM_GROUP_CH, S5_W), lambda i: (i // gh, 0, i % gh, 0, 0))
    arranged_shape = jax.ShapeDtypeStruct((2, S5_T, gh, SSM_GROUP_CH, S5_W), BF16)
    return pl.pallas_call(
        _s5_prep_kernel,
        out_shape=[arranged_shape, arranged_shape, jax.ShapeDtypeStruct((g, S5_W, S5_W), BF16),
                   jax.ShapeDtypeStruct((g, 1, 256), F32)],
        grid=(g,),
        in_specs=[vec, vec, vec, mat, mat, mat, mat],
        out_specs=[arranged, arranged, big, pl.BlockSpec((1, 1, 256), lambda i: (i, 0, 0))],
        compiler_params=_cparams(("parallel",)),
        name="s5_prep",
    )(lre, lim, ldt, btre, btim, cre, cim)


S5_SPREAD_ROWS = 512


def _s5_spread_kernel(a_ref, o_ref, *, rows_per_group, gh):
    hg, w = SSM_GROUP_CH, S5_W
    n_rows, n_cols = o_ref.shape
    src = lax.broadcasted_iota(jnp.int32, (w, n_cols), 0)
    dst = lax.broadcasted_iota(jnp.int32, (w, n_cols), 1)
    tile = jnp.where((src // hg == dst // 128) & (src % hg == dst % hg), 1.0, 0.0).astype(BF16)
    row = pl.program_id(1) * n_rows + lax.broadcasted_iota(jnp.int32, (n_rows, n_cols), 0)
    col = lax.broadcasted_iota(jnp.int32, (n_rows, n_cols), 1)
    keep = (row // rows_per_group) % gh == (col // hg) % gh
    o_ref[...] = jnp.where(keep, _dot(a_ref[...], tile), 0.0).astype(BF16)


def _s5_spread_cols(a, rows_per_group, gh):
    _, n_rows, w = a.shape
    rb = S5_SPREAD_ROWS
    return pl.pallas_call(
        functools.partial(_s5_spread_kernel, rows_per_group=rows_per_group, gh=gh),
        out_shape=jax.ShapeDtypeStruct((2, n_rows, S5_T * 128), BF16),
        grid=(2, n_rows // rb),
        in_specs=[pl.BlockSpec((None, rb, w), lambda z, i: (z, i, 0))],
        out_specs=pl.BlockSpec((None, rb, S5_T * 128), lambda z, i: (z, i, 0)),
        compiler_params=_cparams(("parallel", "parallel")),
        name="s5_spread_cols",
    )(a)


def _s5_place_kernel(a_ref, o_ref, *, gh):
    a = a_ref[...]
    n_rows, w = a.shape
    grp = ((pl.program_id(1) * n_rows + lax.broadcasted_iota(jnp.int32, (n_rows, w), 0))
           // SSM_GROUP_CH) % gh
    for k in range(gh):
        o_ref[:, k * w:(k + 1) * w] = jnp.where(grp == k, a, jnp.zeros_like(a))


def _s5_place_cols(a, gh):
    _, n_rows, w = a.shape
    rb = S5_SPREAD_ROWS
    return pl.pallas_call(
        functools.partial(_s5_place_kernel, gh=gh),
        out_shape=jax.ShapeDtypeStruct((2, n_rows, gh * w), BF16),
        grid=(2, n_rows // rb),
        in_specs=[pl.BlockSpec((None, rb, w), lambda z, i: (z, i, 0))],
        out_specs=pl.BlockSpec((None, rb, gh * w), lambda z, i: (z, i, 0)),
        compiler_params=_cparams(("parallel", "parallel")),
        name="s5_place_cols",
    )(a)


def _s5_z_kernel(u_ref, pt_ref, z_ref):
    z_ref[0] = _dot(u_ref[0].astype(BF16), pt_ref[0])


def _s5_rec_kernel(z_ref, a_ref, h0_ref, hf_ref, hb_ref, fin_ref, *, n_ctx_seq, ctx_chunks, n_lat_seq,
                   lat_chunks):
    sw = z_ref.shape[-1]
    ng = sw // 256
    lane = lax.broadcasted_iota(jnp.int32, (1, sw), 1)
    is_f = (lane % 128) < SSM_STATE

    def split(row):
        re = jnp.concatenate([row[:, g * 256:g * 256 + 128] for g in range(ng)], axis=1)
        im = jnp.concatenate([row[:, g * 256 + 128:(g + 1) * 256] for g in range(ng)], axis=1)
        return re, im

    def merge(re, im):
        parts = []
        for g in range(ng):
            parts += [re[:, g * 128:(g + 1) * 128], im[:, g * 128:(g + 1) * 128]]
        return jnp.concatenate(parts, axis=1)

    a_re, a_im = split(a_ref[0])

    def run_seq(base, n, seq):
        def step(k, h):
            h_re, h_im = h
            cf = base + k
            cb = base + n - 1 - k
            row = merge(h_re, h_im)
            hf_ref[0, pl.ds(cf, 1), :] = row
            hb_ref[0, pl.ds(cb, 1), :] = row
            z_re, z_im = split(jnp.where(is_f, z_ref[0, pl.ds(cf, 1), :], z_ref[0, pl.ds(cb, 1), :]))
            return (a_re * h_re - a_im * h_im + z_re, a_re * h_im + a_im * h_re + z_im)
        h_re, h_im = lax.fori_loop(0, n, step, split(h0_ref[0, pl.ds(seq, 1), :]))
        fin_ref[0, pl.ds(seq, 1), :] = merge(h_re, h_im)

    def ctx_body(s, carry):
        run_seq(s * ctx_chunks, ctx_chunks, s)
        return carry
    lax.fori_loop(0, n_ctx_seq, ctx_body, 0)

    def lat_body(s, carry):
        run_seq(n_ctx_seq * ctx_chunks + s * lat_chunks, lat_chunks, n_ctx_seq + s)
        return carry
    lax.fori_loop(0, n_lat_seq, lat_body, 0)


def _s5_y_kernel(u_ref, hf_ref, hb_ref, mt_ref, qt_ref, d_ref, y_ref):
    u = u_ref[0]
    lane = lax.broadcasted_iota(jnp.int32, (1, u.shape[-1]), 1)
    is_f = (lane % 128) < SSM_STATE
    h = jnp.where(is_f, hf_ref[0], hb_ref[0]).astype(BF16)
    y_ref[0] = _dot(u.astype(BF16), mt_ref[0]) + _dot(h, qt_ref[0]) + d_ref[0] * u


def _s5_mixer(s_u, mats, ssm_d, h0, n_ctx_seq, ctx_seq, lat_seq):
    mt, pt, qt, at = mats
    g, t_n, w, hg = SSM_GROUPS, S5_T, S5_W, SSM_GROUP_CH
    gh = g // 2
    nc = s_u.shape[1]
    bw = t_n * 128
    sw = gh * w
    n_seq = h0.shape[0]
    ctx_chunks, lat_chunks = ctx_seq // t_n, lat_seq // t_n
    n_lat_seq = n_seq - n_ctx_seq
    u = s_u
    m_big = _s5_spread_cols(mt.reshape(2, bw, w), hg, gh)
    q_big = _s5_spread_cols(qt.reshape(2, sw, w), w, gh)
    p_big = _s5_place_cols(pt.reshape(2, bw, w), gh)
    d_big = jnp.tile(ssm_d.reshape(2, 1, 128), (1, 1, t_n))

    rb = math.gcd(nc, 256)
    u_spec = pl.BlockSpec((1, rb, bw), lambda z, i: (z, i, 0))
    s_spec = pl.BlockSpec((1, rb, sw), lambda z, i: (z, i, 0))
    z = pl.pallas_call(
        _s5_z_kernel,
        out_shape=jax.ShapeDtypeStruct((2, nc, sw), F32),
        grid=(2, nc // rb),
        in_specs=[u_spec, pl.BlockSpec((1, bw, sw), lambda z, i: (z, 0, 0))],
        out_specs=s_spec,
        compiler_params=_cparams(("parallel", "parallel"), VMEM_LIMIT),
        name="s5_chunk_inputs",
    )(u, p_big)

    half = lambda rows: pl.BlockSpec((1, rows, sw), lambda z: (z, 0, 0))
    hf, hb, fin = pl.pallas_call(
        functools.partial(_s5_rec_kernel, n_ctx_seq=n_ctx_seq, ctx_chunks=ctx_chunks,
                          n_lat_seq=n_lat_seq, lat_chunks=lat_chunks),
        out_shape=[jax.ShapeDtypeStruct((2, nc, sw), F32)] * 2 + [jax.ShapeDtypeStruct((2, n_seq, sw), F32)],
        grid=(2,),
        in_specs=[half(nc), half(1), half(n_seq)],
        out_specs=[half(nc), half(nc), half(n_seq)],
        compiler_params=_cparams(("parallel",), VMEM_LIMIT),
        name="s5_recurrence",
    )(z, at.reshape(2, 1, sw), h0.reshape(n_seq, 2, sw).transpose(1, 0, 2))

    y = pl.pallas_call(
        _s5_y_kernel,
        out_shape=jax.ShapeDtypeStruct((2, nc, bw), F32),
        grid=(2, nc // rb),
        in_specs=[u_spec, s_spec, s_spec,
                  pl.BlockSpec((1, bw, bw), lambda z, i: (z, 0, 0)),
                  pl.BlockSpec((1, sw, bw), lambda z, i: (z, 0, 0)),
                  pl.BlockSpec((1, 1, bw), lambda z, i: (z, 0, 0))],
        out_specs=u_spec,
        compiler_params=_cparams(("parallel", "parallel"), VMEM_LIMIT),
        name="s5_outputs",
    )(u, hf, hb, m_big, q_big, d_big)
    return y, fin.transpose(1, 0, 2).reshape(n_seq, g, w)


DN_TILE = 256


def _dn_pre_kernel(qkv_ref, prev_ref, next_ref, w_ref, ab_ref, alog_ref, dtb_ref, qkvn_ref, gcb_ref, gt_ref,
                   *, n_ctx_tiles, tiles_per_lat_seq):
    x = qkv_ref[...]
    n = x.shape[0]
    row = lax.broadcasted_iota(jnp.int32, (n, 1), 0)
    i = pl.program_id(0)
    lat_pos = (i - n_ctx_tiles) % tiles_per_lat_seq
    has_prev = (i >= n_ctx_tiles) & (lat_pos > 0)
    has_next = (i >= n_ctx_tiles) & (lat_pos < tiles_per_lat_seq - 1)
    prev_row = jnp.where(has_prev, prev_ref[7:8, :], 0.0)
    next_row = jnp.where(has_next, next_ref[0:1, :], 0.0)
    xp = jnp.where(row == 0, prev_row, pltpu.roll(x, 1, 0))
    xn = jnp.where(row == n - 1, next_row, pltpu.roll(x, n - 1, 0))
    w = w_ref[...]
    y = _silu(w[0:1] * xp + w[1:2] * x + w[2:3] * xn)
    for hcol in range(2 * DN_HEADS):
        seg = y[:, hcol * DN_HEAD_DIM:(hcol + 1) * DN_HEAD_DIM]
        nrm = seg * lax.rsqrt(jnp.sum(seg * seg, axis=-1, keepdims=True) + EPS)
        if hcol < DN_HEADS:
            nrm = nrm * (DN_HEAD_DIM ** -0.5)
        qkvn_ref[:, hcol * DN_HEAD_DIM:(hcol + 1) * DN_HEAD_DIM] = nrm.astype(BF16)
    qkvn_ref[:, 2 * DN_WIDTH:] = y[:, 2 * DN_WIDTH:].astype(BF16)

    ab = ab_ref[...]
    lane = lax.broadcasted_iota(jnp.int32, (1, 128), 1)
    glog = jnp.where(lane < 2 * DN_HEADS, -jnp.exp(alog_ref[...]) * _softplus(ab + dtb_ref[...]), 0.0)
    beta = _sigmoid(ab)
    ii = lax.broadcasted_iota(jnp.int32, (n, n), 0)
    jj = lax.broadcasted_iota(jnp.int32, (n, n), 1)
    same = (ii // DN_CHUNK) == (jj // DN_CHUNK)
    pre = jnp.where(same & (jj <= ii), 1.0, 0.0).astype(F32)
    suf = jnp.where(same & (jj >= ii), 1.0, 0.0).astype(F32)
    gpre = jnp.dot(pre, glog, preferred_element_type=F32, precision=HIGHEST)
    gsuf = jnp.dot(suf, glog, preferred_element_type=F32, precision=HIGHEST)
    gcb = jnp.where(lane < DN_HEADS, gpre, jnp.where(lane < 2 * DN_HEADS, gsuf, beta))
    gcb_ref[...] = gcb
    gcb_t = gcb.T
    for c in range(n // DN_CHUNK):
        gt_ref[c] = gcb_t[0:16, c * DN_CHUNK:(c + 1) * DN_CHUNK]


def _dn_pre(qkv, ab, conv_w, a_log, dt_bias, n_ctx_tok, lat_seq):
    n_tok = qkv.shape[0]
    tm = DN_TILE
    nt = n_tok // tm
    n_ctx_tiles = n_ctx_tok // tm
    tps = lat_seq // tm
    w8 = jnp.concatenate([conv_w, jnp.zeros((5, 3 * DN_WIDTH), F32)], axis=0)
    pad = lambda v: jnp.concatenate([v.reshape(1, 2 * DN_HEADS), jnp.zeros((1, 120), F32)], axis=1)
    row = lambda i: (i, 0)
    const = lambda i: (0, 0)
    r8 = tm // 8
    return pl.pallas_call(
        functools.partial(_dn_pre_kernel, n_ctx_tiles=n_ctx_tiles, tiles_per_lat_seq=tps),
        out_shape=[jax.ShapeDtypeStruct((n_tok, 3 * DN_WIDTH), BF16),
                   jax.ShapeDtypeStruct((n_tok, 128), F32),
                   jax.ShapeDtypeStruct((n_tok // DN_CHUNK, 16, DN_CHUNK), F32)],
        grid=(nt,),
        in_specs=[pl.BlockSpec((tm, 3 * DN_WIDTH), row),
                  pl.BlockSpec((8, 3 * DN_WIDTH), lambda i: (jnp.maximum(i * r8 - 1, 0), 0)),
                  pl.BlockSpec((8, 3 * DN_WIDTH), lambda i: (jnp.minimum((i + 1) * r8, nt * r8 - 1), 0)),
                  pl.BlockSpec((8, 3 * DN_WIDTH), const),
                  pl.BlockSpec((tm, 128), row),
                  pl.BlockSpec((1, 128), const),
                  pl.BlockSpec((1, 128), const)],
        out_specs=[pl.BlockSpec((tm, 3 * DN_WIDTH), row), pl.BlockSpec((tm, 128), row),
                   pl.BlockSpec((tm // DN_CHUNK, 16, DN_CHUNK), lambda i: (i, 0, 0))],
        compiler_params=_cparams(("parallel",), VMEM_LIMIT),
        name="dn_pre",
    )(qkv, qkv, qkv, w8, ab, pad(a_log), pad(dt_bias))


DN_CHAINS = 2 * DN_HEADS


def _split(x):
    hi = x.astype(BF16)
    lo = (x - hi.astype(F32)).astype(BF16)
    return hi, lo


def _split_lhs(a):
    hi, lo = _split(a)
    hif = hi.astype(F32)
    return jnp.concatenate([hif, hif, lo.astype(F32)], axis=1).astype(BF16)


def _split_rhs(b):
    hi, lo = _split(b)
    return jnp.concatenate([hi, lo, hi], axis=0)


def _dn_prepare(q_ref, k_ref, v_ref, gc_ref, gt_ref, row0, chunk_idx):
    c = DN_CHUNK
    ii = lax.broadcasted_iota(jnp.int32, (c, c), 0)
    jj = lax.broadcasted_iota(jnp.int32, (c, c), 1)
    items = []
    for d in range(2):
        rows = pl.ds(row0[d], c)
        gcol = gc_ref[rows, :]
        grow = gt_ref[chunk_idx[d]]
        incl = (ii >= jj) if d == 0 else (ii <= jj)
        strict = (ii > jj) if d == 0 else (ii < jj)
        for h in range(DN_HEADS):
            ch = d * DN_HEADS + h
            cols = slice(h * DN_HEAD_DIM, (h + 1) * DN_HEAD_DIM)
            q, k, v = q_ref[rows, cols], k_ref[rows, cols], v_ref[rows, cols]
            gc_col = gcol[:, ch:ch + 1]
            beta = gcol[:, DN_CHAINS + ch:DN_CHAINS + ch + 1]
            gc_row = grow[ch:ch + 1, :]
            kf = k.astype(F32)
            kb = kf * beta
            decay = jnp.exp(jnp.where(incl, gc_col - gc_row, -jnp.inf))
            egc = jnp.exp(gc_col)
            g_last = gc_col[c - 1:c] if d == 0 else gc_col[0:1]
            items.append(dict(
                q=q, k=k, kb=kb.astype(BF16), decay=decay, incl=incl, strict=strict,
                rhs=jnp.concatenate([kb * egc, v.astype(F32) * beta], axis=1),
                qg=(q.astype(F32) * egc).astype(BF16),
                kt=(kf * jnp.exp(g_last - gc_col)).astype(BF16),
                gl=jnp.broadcast_to(jnp.exp(g_last), (8, 128))))
    for it in items:
        it['a'] = jnp.where(it['strict'], _dot_nt(it['kb'], it['k']) * it['decay'], 0.0)
        it['qk'] = jnp.where(it['incl'], _dot_nt(it['q'], it['k']) * it['decay'], 0.0).astype(BF16)
    power = 1
    while power < c:
        for it in items:
            it['lhs'] = _split_lhs(it['a'])
        for it in items:
            upd = _dot(it['lhs'], _split_rhs(it['rhs'] if power == 1 else it['r']))
            it['r'] = it['rhs'] - upd if power == 1 else it['r'] + upd
        if 2 * power < c:
            for it in items:
                it['a'] = _dot(it['lhs'], _split_rhs(it['a']))
        power *= 2
    return items


def _dn_scan_kernel(*refs, n_chunks, seq_len, seqs_per_blk, has_s0, write_fin):
    refs = list(refs)
    q_ref, k_ref, v_ref, gc_ref, gt_ref = refs[:5]
    pos = 5
    s0_ref = None
    if has_s0:
        s0_ref = refs[pos]
        pos += 1
    o_ref = refs[pos]
    pos += 1
    fin_ref = None
    if write_fin:
        fin_ref = refs[pos]
        pos += 1
    s_ref, w_s, u_s, qk_s, qg_s, kt_s, gl_s = refs[pos:pos + 7]
    n_steps = seqs_per_blk * n_chunks

    o_ref[...] = jnp.zeros_like(o_ref)
    for r in (s_ref, w_s, u_s, qk_s, qg_s, kt_s, gl_s):
        r[...] = jnp.zeros_like(r)

    def locate(step):
        seq = step // n_chunks
        kk = step % n_chunks
        cf = seq * n_chunks + kk
        cb = seq * n_chunks + (n_chunks - 1 - kk)
        rows = [pl.multiple_of(cf * DN_CHUNK, DN_CHUNK), pl.multiple_of(cb * DN_CHUNK, DN_CHUNK)]
        return seq, kk, rows, [cf, cb]

    def body(t, carry):
        tb = jnp.maximum(t, 0)
        valid = t >= 0
        seq, kk, rows_b, _ = locate(tb)
        first = kk == 0
        states, vnews = [], []
        for ch in range(DN_CHAINS):
            d, h = divmod(ch, DN_HEADS)
            s0 = s0_ref[0, d, h] if has_s0 else jnp.zeros((DN_HEAD_DIM, DN_HEAD_DIM), F32)
            states.append(jnp.where(first, s0, s_ref[ch]))
        for ch in range(DN_CHAINS):
            vnews.append((u_s[ch] - _dot(w_s[ch], states[ch].astype(BF16))).astype(BF16))
        tn = jnp.minimum(t + 1, n_steps - 1)
        _, _, rows_a, chunks_a = locate(tn)
        items = _dn_prepare(q_ref, k_ref, v_ref, gc_ref, gt_ref, rows_a, chunks_a)
        for ch in range(DN_CHAINS):
            d, h = divmod(ch, DN_HEADS)
            sb = states[ch].astype(BF16)
            o = _dot(qg_s[ch], sb) + _dot(qk_s[ch], vnews[ch])
            s_new = states[ch] * gl_s[ch][0:1, 0:1] + _dot_tn(kt_s[ch], vnews[ch])
            s_ref[ch] = s_new
            if write_fin:
                fin_ref[seq, d, h] = s_new
            rows = pl.ds(rows_b[d], DN_CHUNK)
            cols = slice(h * DN_HEAD_DIM, (h + 1) * DN_HEAD_DIM)
            o_ref[rows, cols] = o_ref[rows, cols] + jnp.where(valid, o, 0.0)
        for ch, it in enumerate(items):
            w_s[ch] = it['r'][:, :DN_HEAD_DIM].astype(BF16)
            u_s[ch] = it['r'][:, DN_HEAD_DIM:]
            qk_s[ch] = it['qk']
            qg_s[ch] = it['qg']
            kt_s[ch] = it['kt']
            gl_s[ch] = it['gl']
        return carry

    lax.fori_loop(-1, n_steps, body, 0)


def _dn_scan(qkvn, gcb, gcb_t, seq_len, n_seq, seqs_per_blk, first_tok, s0, write_fin):
    n_chunks = seq_len // DN_CHUNK
    rows_blk = seqs_per_blk * seq_len
    b0 = first_tok // rows_blk
    hd = DN_HEAD_DIM
    st_spec = pl.BlockSpec((seqs_per_blk, 2, DN_HEADS, hd, hd), lambda s: (s, 0, 0, 0, 0))
    in_specs = [pl.BlockSpec((rows_blk, DN_WIDTH), lambda s: (b0 + s, 0)),
                pl.BlockSpec((rows_blk, DN_WIDTH), lambda s: (b0 + s, 1)),
                pl.BlockSpec((rows_blk, DN_WIDTH), lambda s: (b0 + s, 2)),
                pl.BlockSpec((rows_blk, 128), lambda s: (b0 + s, 0)),
                pl.BlockSpec((rows_blk // DN_CHUNK, 16, DN_CHUNK), lambda s: (b0 + s, 0, 0))]
    args = [qkvn, qkvn, qkvn, gcb, gcb_t]
    if s0 is not None:
        assert seqs_per_blk == 1
        in_specs.append(st_spec)
        args.append(s0)
    out_shape = [jax.ShapeDtypeStruct((n_seq * seq_len, DN_WIDTH), F32)]
    out_specs = [pl.BlockSpec((rows_blk, DN_WIDTH), lambda s: (s, 0))]
    if write_fin:
        out_shape.append(jax.ShapeDtypeStruct((n_seq, 2, DN_HEADS, hd, hd), F32))
        out_specs.append(st_spec)
    c = DN_CHUNK
    scratch = [pltpu.VMEM((DN_CHAINS, hd, hd), F32),
               pltpu.VMEM((DN_CHAINS, c, hd), BF16),
               pltpu.VMEM((DN_CHAINS, c, hd), F32),
               pltpu.VMEM((DN_CHAINS, c, c), BF16),
               pltpu.VMEM((DN_CHAINS, c, hd), BF16),
               pltpu.VMEM((DN_CHAINS, c, hd), BF16),
               pltpu.VMEM((DN_CHAINS, 8, 128), F32)]
    return pl.pallas_call(
        functools.partial(_dn_scan_kernel, n_chunks=n_chunks, seq_len=seq_len, seqs_per_blk=seqs_per_blk,
                          has_s0=s0 is not None, write_fin=write_fin),
        out_shape=out_shape,
        grid=(n_seq // seqs_per_blk,),
        in_specs=in_specs,
        out_specs=out_specs,
        scratch_shapes=scratch,
        compiler_params=_cparams(("parallel",), VMEM_LIMIT),
        name="dn_scan_%d" % seq_len,
    )(*args)


def _deltanet(qkv, ab, p, s0_lat, n_ctx_seq, ctx_seq, n_lat_seq, lat_seq):
    n_ctx_tok = n_ctx_seq * ctx_seq
    qkvn, gcb, gcb_t = _dn_pre(qkv, ab, p['dn_conv'], p['dn_a_log'], p['dn_dt_bias'], n_ctx_tok, lat_seq)
    ctx_blk = math.gcd(n_ctx_seq, 8)
    o_ctx, fin = _dn_scan(qkvn, gcb, gcb_t, ctx_seq, n_ctx_seq, ctx_blk, 0, None, True)
    (o_lat,) = _dn_scan(qkvn, gcb, gcb_t, lat_seq, n_lat_seq, 1, n_ctx_tok, s0_lat, False)
    return o_ctx, o_lat, fin


MIX_TILE = 256


def _mix_kernel(*refs, n_x, n_ctx_tiles):
    x_refs = refs[:n_x]
    (mod_ref, ys_ref, octx_ref, olat_ref, z_ref, cu_ref, cv_ref,
     wglu_ref, bglu_ref, dnn_ref, cmn_ref, ws_ref, bs_ref, wo_ref, n2_ref,
     x1_ref, h2_ref, scr) = refs[n_x:]
    mod = mod_ref[0]
    gate1 = mod[:, 2 * D_MODEL:3 * D_MODEL]
    shift2 = mod[:, 3 * D_MODEL:4 * D_MODEL]
    scale2 = mod[:, 4 * D_MODEL:5 * D_MODEL]
    n_rows = ys_ref.shape[1]
    for half in range(2):
        for t in range(S5_T):
            scr[half, pl.ds(t, n_rows, stride=S5_T), :] = ys_ref[half, :, t * 128:(t + 1) * 128]
    ya = _gelu(jnp.concatenate([scr[0], scr[1]], axis=1))
    ya = ya * _sigmoid(_dot(ya.astype(BF16), wglu_ref[...]) + bglu_ref[...])
    mix = _dot(ya.astype(BF16), wo_ref[0:SSM_CH, :])
    o = jnp.where(pl.program_id(0) < n_ctx_tiles, octx_ref[...], olat_ref[...])
    z = z_ref[...]
    parts = []
    for h in range(DN_HEADS):
        sl = slice(h * DN_HEAD_DIM, (h + 1) * DN_HEAD_DIM)
        parts.append(_rms(o[:, sl]) * dnn_ref[...] * _silu(z[:, sl]))
    yb = jnp.concatenate(parts, axis=1)
    mix = mix + _dot(yb.astype(BF16), wo_ref[SSM_CH:SSM_CH + DN_WIDTH, :])
    u = _gelu(cu_ref[...])
    v = _gelu(cv_ref[...])
    seg_i = lax.broadcasted_iota(jnp.int32, (CM_WIDTH, CM_WIDTH), 0) // CM_HEAD_DIM
    seg_j = lax.broadcasted_iota(jnp.int32, (CM_WIDTH, CM_WIDTH), 1) // CM_HEAD_DIM
    seg_ones = jnp.where(seg_i == seg_j, 1.0, 0.0).astype(BF16)
    vv = v * v
    vv_hi = vv.astype(BF16)
    vv_lo = (vv - vv_hi.astype(F32)).astype(BF16)
    ms = (_dot(vv_hi, seg_ones) + _dot(vv_lo, seg_ones)) * (1.0 / CM_HEAD_DIM)
    vn = (v * lax.rsqrt(ms + EPS) * cmn_ref[...]).astype(BF16)
    head_of_lane = lax.broadcasted_iota(jnp.int32, (1, CM_WIDTH), 1) // CM_HEAD_DIM
    ycs = []
    for c in range(MIX_TILE // CM_CHUNK):
        vc = vn[c * CM_CHUNK:(c + 1) * CM_CHUNK]
        stack = jnp.concatenate([jnp.where(head_of_lane == h, vc, jnp.zeros_like(vc))
                                 for h in range(CM_HEADS)], axis=0)
        vs = _dot(ws_ref[...], stack) + bs_ref[...]
        ycs.append(u[c * CM_CHUNK:(c + 1) * CM_CHUNK] * vs)
    yc = jnp.concatenate(ycs, axis=0)
    mix = mix + _dot(yc.astype(BF16), wo_ref[SSM_CH + DN_WIDTH:, :])
    x1 = _token_tile(x_refs, n_ctx_tiles) + gate1 * mix
    x1_ref[...] = x1
    h2_ref[...] = (_rms(x1) * n2_ref[...] * (1.0 + scale2) + shift2).astype(BF16)


def _mix_out(xs, mods_l, ys, o_ctx, o_lat, z, cu, cv, p, n_ctx_tok, lat_seq):
    n_tok = sum(x.shape[0] for x in xs)
    tm = MIX_TILE
    n_ctx_tiles = n_ctx_tok // tm
    tps = lat_seq // tm
    row = lambda i: (i, 0)
    const = lambda i: (0, 0)
    ctx_row = lambda i: (jnp.minimum(i, n_ctx_tiles - 1), 0)
    lat_row = lambda i: (jnp.maximum(i - n_ctx_tiles, 0), 0)
    ws_cat = jnp.concatenate([p['cm_w_s'][h] for h in range(CM_HEADS)], axis=1).astype(BF16)
    bs_exp = jnp.repeat(p['cm_b_s'].T, CM_HEAD_DIM, axis=1)
    return pl.pallas_call(
        functools.partial(_mix_kernel, n_x=len(xs), n_ctx_tiles=n_ctx_tiles),
        out_shape=[jax.ShapeDtypeStruct((n_tok, D_MODEL), F32),
                   jax.ShapeDtypeStruct((n_tok, D_MODEL), BF16)],
        grid=(n_tok // tm,),
        in_specs=_token_specs(xs, tm, n_ctx_tiles, D_MODEL)
        + [pl.BlockSpec((1, 1, N_MOD * D_MODEL),
                               lambda i: (_mod_row_index(i, n_ctx_tiles, tps), 0, 0)),
                  pl.BlockSpec((2, tm // S5_T, S5_T * 128), lambda i: (0, i, 0)),
                  pl.BlockSpec((tm, DN_WIDTH), ctx_row),
                  pl.BlockSpec((tm, DN_WIDTH), lat_row),
                  pl.BlockSpec((tm, DN_WIDTH), row),
                  pl.BlockSpec((tm, CM_WIDTH), row),
                  pl.BlockSpec((tm, CM_WIDTH), row),
                  pl.BlockSpec((SSM_CH, SSM_CH), const),
                  pl.BlockSpec((1, SSM_CH), const),
                  pl.BlockSpec((1, DN_HEAD_DIM), const),
                  pl.BlockSpec((1, CM_WIDTH), const),
                  pl.BlockSpec((CM_CHUNK, CM_HEADS * CM_CHUNK), const),
                  pl.BlockSpec((CM_CHUNK, CM_WIDTH), const),
                  pl.BlockSpec((D_MODEL, D_MODEL), const),
                  pl.BlockSpec((1, D_MODEL), const)],
        out_specs=[pl.BlockSpec((tm, D_MODEL), row), pl.BlockSpec((tm, D_MODEL), row)],
        scratch_shapes=[pltpu.VMEM((2, tm, 128), F32)],
        compiler_params=_cparams(("parallel",), VMEM_LIMIT),
        name="mix_out",
    )(*xs, mods_l, ys, o_ctx, o_lat, z, cu, cv,
      p['ssm_w_glu'].astype(BF16), p['ssm_b_glu'].reshape(1, SSM_CH),
      p['dn_norm'].reshape(1, DN_HEAD_DIM), p['cm_norm'].reshape(1, CM_WIDTH),
      ws_cat, bs_exp, p['w_out'].astype(BF16), p['norm2'].reshape(1, D_MODEL))


ROUTE_TILE = 512


def _first_max_mask(x, idx, axis):
    m = jnp.max(x, axis=axis, keepdims=True)
    first = jnp.min(jnp.where(x == m, idx, jnp.int32(1 << 20)), axis=axis, keepdims=True)
    return idx == first


def _router_kernel(h_ref, wr_ref, bias_ref, g_ref):
    n = h_ref.shape[0]
    per = N_EXPERTS // N_ROUTE_GROUPS
    logits = _dot_nt(wr_ref[...], h_ref[...])
    scores = _sigmoid(logits)
    choice = scores + bias_ref[...]
    ninf = jnp.float32(-jnp.inf)
    c3 = choice.reshape(N_ROUTE_GROUPS, per, n)
    i3 = lax.broadcasted_iota(jnp.int32, c3.shape, 1)
    top1 = _first_max_mask(c3, i3, 1)
    m1 = jnp.max(c3, axis=1)
    m2 = jnp.max(jnp.where(top1, ninf, c3), axis=1)
    grp = m1 + m2
    gi = lax.broadcasted_iota(jnp.int32, grp.shape, 0)
    gsel = jnp.zeros(grp.shape, jnp.bool_)
    work = grp
    for _ in range(TOPK_ROUTE_GROUPS):
        pick = _first_max_mask(work, gi, 0)
        gsel = gsel | pick
        work = jnp.where(pick, ninf, work)
    gmask = jnp.broadcast_to(gsel[:, None, :], c3.shape).reshape(N_EXPERTS, n)
    work = jnp.where(gmask, choice, ninf)
    ei = lax.broadcasted_iota(jnp.int32, work.shape, 0)
    esel = jnp.zeros(work.shape, jnp.bool_)
    for _ in range(TOP_K):
        pick = _first_max_mask(work, ei, 0)
        esel = esel | pick
        work = jnp.where(pick, ninf, work)
    wts = jnp.where(esel, scores, 0.0)
    g_ref[...] = wts / (jnp.sum(wts, axis=0, keepdims=True) + 1e-20) * ROUTED_SCALE


def _router(h2, router_w, router_bias):
    n_tok = h2.shape[0]
    tm = ROUTE_TILE
    return pl.pallas_call(
        _router_kernel,
        out_shape=jax.ShapeDtypeStruct((N_EXPERTS, n_tok), F32),
        grid=(n_tok // tm,),
        in_specs=[pl.BlockSpec((tm, D_MODEL), lambda i: (i, 0)),
                  pl.BlockSpec((N_EXPERTS, D_MODEL), lambda i: (0, 0)),
                  pl.BlockSpec((N_EXPERTS, 1), lambda i: (0, 0))],
        out_specs=pl.BlockSpec((N_EXPERTS, tm), lambda i: (0, i)),
        compiler_params=_cparams(("parallel",)),
        name="router",
    )(h2, router_w.T.astype(BF16), router_bias.reshape(N_EXPERTS, 1))


MOE_TILE = 1024
MOE_EG = 8


def _moe_kernel(h_ref, x_ref, mod_ref, g_ref, wg_ref, wu_ref, wd_ref, sg_ref, su_ref, sd_ref, nf_ref,
                *out_refs, final):
    j = pl.program_id(1)
    nj = pl.num_programs(1)
    acc_ref = out_refs[-1]
    h = h_ref[...]

    @pl.when(j == 0)
    def _():
        sh = _silu(_dot(h, sg_ref[...])) * _dot(h, su_ref[...])
        acc_ref[...] = _dot(sh.astype(BF16), sd_ref[...])

    g_hi, g_lo = _split(g_ref[...])
    erow = lax.broadcasted_iota(jnp.int32, (2 * MOE_EG, MOE_EG * EXPERT_FF), 0) % MOE_EG
    ecol = lax.broadcasted_iota(jnp.int32, (2 * MOE_EG, MOE_EG * EXPERT_FF), 1) // EXPERT_FF
    spread = jnp.where(erow == ecol, 1.0, 0.0).astype(BF16)
    gexp = _dot_tn(jnp.concatenate([g_hi, g_lo], axis=0), spread)
    hg = _dot(h, wg_ref[...])
    hu = _dot(h, wu_ref[...])
    act = (_silu(hg) * hu * gexp).astype(BF16)
    acc_ref[...] += _dot(act, wd_ref[...])

    @pl.when(j == nj - 1)
    def _():
        gate2 = mod_ref[0][:, 5 * D_MODEL:6 * D_MODEL]
        x2 = x_ref[...] + gate2 * acc_ref[...]
        if final:
            out_refs[0][...] = _rms(x2) * nf_ref[...]
        else:
            out_refs[0][...] = x2


def _expert_cols_kernel(w_ref, o_ref):
    for e in range(w_ref.shape[0]):
        o_ref[:, e * EXPERT_FF:(e + 1) * EXPERT_FF] = w_ref[e].astype(BF16)


def _expert_cols(w):
    n_e, d, f = w.shape
    return pl.pallas_call(
        _expert_cols_kernel,
        out_shape=jax.ShapeDtypeStruct((d, n_e * f), BF16),
        grid=(n_e // MOE_EG,),
        in_specs=[pl.BlockSpec((MOE_EG, d, f), lambda j: (j, 0, 0))],
        out_specs=pl.BlockSpec((d, MOE_EG * f), lambda j: (0, j)),
        compiler_params=_cparams(("parallel",)),
        name="expert_cols",
    )(w)


def _cast_rows_kernel(w_ref, o_ref):
    o_ref[...] = w_ref[...].astype(BF16)


def _cast_rows(w, rows):
    r, c = w.shape
    return pl.pallas_call(
        _cast_rows_kernel,
        out_shape=jax.ShapeDtypeStruct((r, c), BF16),
        grid=(r // rows,),
        in_specs=[pl.BlockSpec((rows, c), lambda j: (j, 0))],
        out_specs=pl.BlockSpec((rows, c), lambda j: (j, 0)),
        compiler_params=_cparams(("parallel",)),
        name="cast_rows",
    )(w)


def _moe(h2, x1, mods_l, gates_t, p, norm_f, final, n_ctx_tok, lat_seq):
    n_tok = h2.shape[0]
    tm = MOE_TILE
    n_ctx_tiles = n_ctx_tok // tm
    tps = lat_seq // tm
    neg = N_EXPERTS // MOE_EG
    ef = MOE_EG * EXPERT_FF
    wg = _expert_cols(p['moe_w_gate'])
    wu = _expert_cols(p['moe_w_up'])
    wd = _cast_rows(p['moe_w_down'].reshape(N_EXPERTS * EXPERT_FF, D_MODEL), 1024)
    row = lambda i, j: (i, 0)
    const = lambda i, j: (0, 0)
    out = pl.pallas_call(
        functools.partial(_moe_kernel, final=final),
        out_shape=jax.ShapeDtypeStruct((n_tok, D_MODEL), F32),
        grid=(n_tok // tm, neg),
        in_specs=[pl.BlockSpec((tm, D_MODEL), row),
                  pl.BlockSpec((tm, D_MODEL), row),
                  pl.BlockSpec((1, 1, N_MOD * D_MODEL),
                               lambda i, j: (_mod_row_index(i, n_ctx_tiles, tps), 0, 0)),
                  pl.BlockSpec((MOE_EG, tm), lambda i, j: (j, i)),
                  pl.BlockSpec((D_MODEL, ef), lambda i, j: (0, j)),
                  pl.BlockSpec((D_MODEL, ef), lambda i, j: (0, j)),
                  pl.BlockSpec((ef, D_MODEL), lambda i, j: (j, 0)),
                  pl.BlockSpec((D_MODEL, SHARED_FF), const),
                  pl.BlockSpec((D_MODEL, SHARED_FF), const),
                  pl.BlockSpec((SHARED_FF, D_MODEL), const),
                  pl.BlockSpec((1, D_MODEL), const)],
        out_specs=pl.BlockSpec((tm, D_MODEL), row),
        scratch_shapes=[pltpu.VMEM((tm, D_MODEL), F32)],
        compiler_params=_cparams(("parallel", "arbitrary"), VMEM_LIMIT),
        name="moe",
    )(h2, x1, mods_l, gates_t, wg, wu, wd,
      p['moe_ws_gate'].astype(BF16), p['moe_ws_up'].astype(BF16), p['moe_ws_down'].astype(BF16),
      norm_f.reshape(1, D_MODEL))
    return out


def kernel(x_prompt, x_sample, state_ssm, state_delta, c, c_ctx, w_ada, b_ada, norm1, norm2, w_in, w_out, ssm_lam_re, ssm_lam_im, ssm_log_dt, ssm_b_re, ssm_b_im, ssm_c_re, ssm_c_im, ssm_d, ssm_w_glu, ssm_b_glu, dn_conv, dn_a_log, dn_dt_bias, dn_norm, cm_norm, cm_w_s, cm_b_s, moe_router, moe_router_bias, moe_w_gate, moe_w_up, moe_w_down, moe_ws_gate, moe_ws_up, moe_ws_down, norm_f):
    n_ctx_seq, ctx_seq, _ = x_prompt.shape
    n_lat_seq, lat_seq, _ = x_sample.shape
    depth = w_ada.shape[0]
    n_ctx_tok = n_ctx_seq * ctx_seq
    n_seq = n_ctx_seq + n_lat_seq
    g = SSM_GROUPS

    xs = [x_prompt.reshape(n_ctx_tok, D_MODEL), x_sample.reshape(n_lat_seq * lat_seq, D_MODEL)]
    cvec = jnp.concatenate([c_ctx[None, :], c, jnp.zeros((8 - 1 - n_lat_seq, D_MODEL), F32)], axis=0)
    mods = _ada_mods(cvec, w_ada, b_ada)

    ssm_states, dn_states = [], []
    y = None
    for l in range(depth):
        p = {'w_out': w_out[l], 'norm2': norm2[l], 'ssm_w_glu': ssm_w_glu[l], 'ssm_b_glu': ssm_b_glu[l],
             'dn_conv': dn_conv[l], 'dn_a_log': dn_a_log[l], 'dn_dt_bias': dn_dt_bias[l],
             'dn_norm': dn_norm[l], 'cm_norm': cm_norm[l], 'cm_w_s': cm_w_s[l], 'cm_b_s': cm_b_s[l],
             'moe_w_gate': moe_w_gate[l], 'moe_w_up': moe_w_up[l], 'moe_w_down': moe_w_down[l],
             'moe_ws_gate': moe_ws_gate[l], 'moe_ws_up': moe_ws_up[l], 'moe_ws_down': moe_ws_down[l]}
        mods_l = mods[l].reshape(8, 1, N_MOD * D_MODEL)
        s_u, qkv, z, cu, cv, ab = _in_projection(xs, mods_l, norm1[l], _cast_rows(w_in[l], 256),
                                                 n_ctx_tok, lat_seq)

        mats = _s5_prep(ssm_lam_re[l], ssm_lam_im[l], ssm_log_dt[l], ssm_b_re[l], ssm_b_im[l],
                        ssm_c_re[l], ssm_c_im[l])
        h0_lat = state_ssm[:, l].transpose(0, 3, 2, 1, 4).reshape(n_lat_seq, g, 256)
        h0 = jnp.concatenate([jnp.zeros((n_ctx_seq, g, 256), F32), h0_lat], axis=0)
        ys, ssm_fin = _s5_mixer(s_u, mats, ssm_d[l], h0, n_ctx_seq, ctx_seq, lat_seq)
        ssm_states.append(ssm_fin[:n_ctx_seq].reshape(n_ctx_seq, g, 2, 2, SSM_STATE).transpose(0, 3, 2, 1, 4))

        o_ctx, o_lat, dn_fin = _deltanet(qkv, ab, p, state_delta[:, l], n_ctx_seq, ctx_seq, n_lat_seq, lat_seq)
        dn_states.append(dn_fin)

        x1, h2 = _mix_out(xs, mods_l, ys, o_ctx, o_lat, z, cu, cv, p, n_ctx_tok, lat_seq)
        gates_t = _router(h2, moe_router[l], moe_router_bias[l])
        x = _moe(h2, x1, mods_l, gates_t, p, norm_f, l == depth - 1, n_ctx_tok, lat_seq)
        xs = [x]

    y_prompt = x[:n_ctx_tok].reshape(x_prompt.shape)
    y_sample = x[n_ctx_tok:].reshape(x_sample.shape)
    new_state_ssm = jnp.stack(ssm_states, axis=1)
    new_state_delta = jnp.stack(dn_states, axis=1)
    return (y_prompt, y_sample, new_state_ssm, new_state_delta)
```

```python
import functools
import math

import jax
import jax.numpy as jnp
from jax import lax
from jax.experimental import pallas as pl
from jax.experimental.pallas import tpu as pltpu

F32 = jnp.float32
BF16 = jnp.bfloat16
HIGHEST = lax.Precision.HIGHEST

D_MODEL = 1024
N_MOD = 6
EPS = 1e-6
SSM_CH = 256
SSM_GROUPS = 16
SSM_GROUP_CH = 16
SSM_STATE = 64
S5_T = 16
S5_W = S5_T * SSM_GROUP_CH
DN_HEADS = 4
DN_HEAD_DIM = 128
DN_WIDTH = 512
DN_CHUNK = 64
CM_HEADS = 4
CM_WIDTH = 256
CM_HEAD_DIM = 64
CM_CHUNK = 128
N_EXPERTS = 64
TOP_K = 6
N_ROUTE_GROUPS = 8
TOPK_ROUTE_GROUPS = 4
EXPERT_FF = 128
SHARED_FF = 256
ROUTED_SCALE = 2.5
MAIN_COLS = SSM_CH + 3 * DN_WIDTH + DN_WIDTH + 2 * CM_WIDTH
VMEM_LIMIT = 56 * 1024 * 1024


def _cparams(sem, vmem=None):
    return pltpu.CompilerParams(dimension_semantics=sem, vmem_limit_bytes=vmem)


def _dot(a, b):
    return jnp.dot(a, b, preferred_element_type=F32)


def _dot_nt(a, b, precision=None):
    return lax.dot_general(a, b, (((1,), (1,)), ((), ())), preferred_element_type=F32,
                           precision=precision)


def _dot_tn(a, b):
    return lax.dot_general(a, b, (((0,), (0,)), ((), ())), preferred_element_type=F32)


def _sigmoid(x):
    return 1.0 / (1.0 + jnp.exp(-x))


def _silu(x):
    return x * _sigmoid(x)


def _gelu(x):
    c = math.sqrt(2.0 / math.pi)
    return 0.5 * x * (1.0 + jnp.tanh(c * (x + 0.044715 * (x * x * x))))


def _softplus(x):
    return jnp.maximum(x, 0.0) + jnp.log(1.0 + jnp.exp(-jnp.abs(x)))


def _rms(x):
    return x * lax.rsqrt(jnp.mean(x * x, axis=-1, keepdims=True) + EPS)


def _mod_row_index(tile, n_ctx_tiles, tiles_per_lat_seq):
    return jnp.where(tile < n_ctx_tiles, 0, 1 + (tile - n_ctx_tiles) // tiles_per_lat_seq)


def _ada_kernel(c_ref, w_ref, b_ref, o_ref):
    h = _silu(c_ref[...])
    o_ref[0] = jnp.dot(h, w_ref[0], preferred_element_type=F32, precision=HIGHEST) + b_ref[0]


def _ada_mods(cvec, w_ada, b_ada):
    depth, d, n = w_ada.shape
    bn = 1536
    return pl.pallas_call(
        _ada_kernel,
        out_shape=jax.ShapeDtypeStruct((depth, 8, n), F32),
        grid=(depth, n // bn),
        in_specs=[pl.BlockSpec((8, d), lambda l, j: (0, 0)),
                  pl.BlockSpec((1, d, bn), lambda l, j: (l, 0, j)),
                  pl.BlockSpec((1, 1, bn), lambda l, j: (l, 0, j))],
        out_specs=pl.BlockSpec((1, 8, bn), lambda l, j: (l, 0, j)),
        compiler_params=_cparams(("parallel", "parallel")),
        name="ada_mods",
    )(cvec, w_ada, b_ada.reshape(depth, 1, n))


def _token_specs(xs, tm, n_ctx_tiles, width):
    if len(xs) == 1:
        return [pl.BlockSpec((tm, width), lambda i, *_: (i, 0))]
    return [pl.BlockSpec((tm, width), lambda i, *_: (jnp.minimum(i, n_ctx_tiles - 1), 0)),
            pl.BlockSpec((tm, width), lambda i, *_: (jnp.maximum(i - n_ctx_tiles, 0), 0))]


def _token_tile(refs, n_ctx_tiles):
    if len(refs) == 1:
        return refs[0][...]
    return jnp.where(pl.program_id(0) < n_ctx_tiles, refs[0][...], refs[1][...])


IN_TILE = 512


def _inproj_kernel(*refs, n_x, n_ctx_tiles, tiles_per_lat_seq, ctx_seq):
    x_refs = refs[:n_x]
    (prev_ref, next_ref, mod_ref, g_ref, w_ref, cw_ref, alog_ref, dtb_ref,
     su_ref, z_ref, cu_ref, cv_ref, qkvn_ref, gcb_ref, gt_ref, scr) = refs[n_x:]
    mod = mod_ref[0]
    shift = mod[:, 0:D_MODEL]
    scale = mod[:, D_MODEL:2 * D_MODEL]
    gain = g_ref[...] * (1.0 + scale)

    def normed(x):
        return (_rms(x) * gain + shift).astype(BF16)

    p = _dot(normed(_token_tile(x_refs, n_ctx_tiles)), w_ref[...])
    c0 = SSM_CH
    c1 = c0 + 3 * DN_WIDTH
    c2 = c1 + DN_WIDTH
    c3 = c2 + 4 * DN_HEADS
    c4 = c3 + CM_WIDTH
    z_ref[...] = p[:, c1:c2]
    cu_ref[...] = p[:, c3:c4]
    cv_ref[...] = p[:, c4:c4 + CM_WIDTH]
    scr[0] = p[:, 0:128]
    scr[1] = p[:, 128:c0]
    n_rows = su_ref.shape[1]
    for half in range(2):
        for t in range(S5_T):
            su_ref[half, :, t * 128:(t + 1) * 128] = scr[half, pl.ds(t, n_rows, stride=S5_T), :]

    x = p[:, c0:c1]
    n = x.shape[0]
    i = pl.program_id(0)
    is_ctx = i < n_ctx_tiles
    lat_pos = (i - n_ctx_tiles) % tiles_per_lat_seq
    has_prev = jnp.logical_not(is_ctx) & (lat_pos > 0)
    has_next = jnp.logical_not(is_ctx) & (lat_pos < tiles_per_lat_seq - 1)
    w_qkv = w_ref[:, c0:c1]
    prev_row = jnp.where(has_prev, _dot(normed(prev_ref[...]), w_qkv)[7:8, :], 0.0)
    next_row = jnp.where(has_next, _dot(normed(next_ref[...]), w_qkv)[0:1, :], 0.0)
    row = lax.broadcasted_iota(jnp.int32, (n, 1), 0)
    seq_first = is_ctx & (row % ctx_seq == 0)
    seq_last = is_ctx & (row % ctx_seq == ctx_seq - 1)
    xp = jnp.where(row == 0, prev_row, pltpu.roll(x, 1, 0))
    xn = jnp.where(row == n - 1, next_row, pltpu.roll(x, n - 1, 0))
    xp = jnp.where(seq_first, 0.0, xp)
    xn = jnp.where(seq_last, 0.0, xn)
    cw = cw_ref[...]
    y = _silu(cw[0:1] * xp + cw[1:2] * x + cw[2:3] * xn)
    for hcol in range(2 * DN_HEADS):
        seg = y[:, hcol * DN_HEAD_DIM:(hcol + 1) * DN_HEAD_DIM]
        nrm = seg * lax.rsqrt(jnp.sum(seg * seg, axis=-1, keepdims=True) + EPS)
        if hcol < DN_HEADS:
            nrm = nrm * (DN_HEAD_DIM ** -0.5)
        qkvn_ref[:, hcol * DN_HEAD_DIM:(hcol + 1) * DN_HEAD_DIM] = nrm.astype(BF16)
    qkvn_ref[:, 2 * DN_WIDTH:] = y[:, 2 * DN_WIDTH:].astype(BF16)

    ab = p[:, c2:c2 + 128]
    lane = lax.broadcasted_iota(jnp.int32, (1, 128), 1)
    glog = jnp.where(lane < 2 * DN_HEADS, -jnp.exp(alog_ref[...]) * _softplus(ab + dtb_ref[...]), 0.0)
    beta = _sigmoid(ab)
    n_ch = n // DN_CHUNK
    wide = jnp.concatenate([glog[c * DN_CHUNK:(c + 1) * DN_CHUNK] for c in range(n_ch)], axis=1)
    ii = lax.broadcasted_iota(jnp.int32, (DN_CHUNK, DN_CHUNK), 0)
    jj = lax.broadcasted_iota(jnp.int32, (DN_CHUNK, DN_CHUNK), 1)
    pre = jnp.where(jj <= ii, 1.0, 0.0).astype(F32)
    suf = jnp.where(jj >= ii, 1.0, 0.0).astype(F32)
    gpre_w = jnp.dot(pre, wide, preferred_element_type=F32, precision=HIGHEST)
    gsuf_w = jnp.dot(suf, wide, preferred_element_type=F32, precision=HIGHEST)
    unwide = lambda a: jnp.concatenate([a[:, c * 128:(c + 1) * 128] for c in range(n_ch)], axis=0)
    gcb = jnp.where(lane < DN_HEADS, unwide(gpre_w), jnp.where(lane < 2 * DN_HEADS, unwide(gsuf_w), beta))
    gcb_ref[...] = gcb
    gcb_t = gcb.T
    for c in range(n_ch):
        gt_ref[c] = gcb_t[0:16, c * DN_CHUNK:(c + 1) * DN_CHUNK]


def _in_projection(xs, mods_l, norm1, w_in_bf, conv_w, a_log, dt_bias, n_ctx_tok, ctx_seq, lat_seq):
    n_tok = sum(x.shape[0] for x in xs)
    tm = IN_TILE
    assert tm % ctx_seq == 0 and lat_seq % tm == 0
    n_ctx_tiles = n_ctx_tok // tm
    tps = lat_seq // tm
    row = lambda i: (i, 0)
    const = lambda i: (0, 0)
    in_cols = w_in_bf.shape[1]
    r8 = tm // 8
    halo_src = xs[-1]
    first = n_ctx_tiles if len(xs) == 2 else 0
    last8 = halo_src.shape[0] // 8 - 1
    prev_spec = pl.BlockSpec((8, D_MODEL), lambda i: (jnp.clip((i - first) * r8 - 1, 0, last8), 0))
    next_spec = pl.BlockSpec((8, D_MODEL), lambda i: (jnp.clip((i - first + 1) * r8, 0, last8), 0))
    w8 = jnp.concatenate([conv_w, jnp.zeros((5, 3 * DN_WIDTH), F32)], axis=0)
    pad = lambda v: jnp.concatenate([v.reshape(1, 2 * DN_HEADS), jnp.zeros((1, 120), F32)], axis=1)
    outs = [(DN_WIDTH, F32), (CM_WIDTH, F32), (CM_WIDTH, F32), (3 * DN_WIDTH, BF16), (128, F32)]
    return pl.pallas_call(
        functools.partial(_inproj_kernel, n_x=len(xs), n_ctx_tiles=n_ctx_tiles, tiles_per_lat_seq=tps,
                          ctx_seq=ctx_seq),
        out_shape=[jax.ShapeDtypeStruct((2, n_tok // S5_T, S5_T * 128), F32)]
        + [jax.ShapeDtypeStruct((n_tok, w), dt) for w, dt in outs]
        + [jax.ShapeDtypeStruct((n_tok // DN_CHUNK, 16, DN_CHUNK), F32)],
        grid=(n_tok // tm,),
        in_specs=_token_specs(xs, tm, n_ctx_tiles, D_MODEL)
        + [prev_spec, next_spec,
           pl.BlockSpec((1, 1, N_MOD * D_MODEL),
                        lambda i: (_mod_row_index(i, n_ctx_tiles, tps), 0, 0)),
           pl.BlockSpec((1, D_MODEL), const),
           pl.BlockSpec((D_MODEL, in_cols), const),
           pl.BlockSpec((8, 3 * DN_WIDTH), const),
           pl.BlockSpec((1, 128), const),
           pl.BlockSpec((1, 128), const)],
        out_specs=[pl.BlockSpec((2, tm // S5_T, S5_T * 128), lambda i: (0, i, 0))]
        + [pl.BlockSpec((tm, w), row) for w, _ in outs]
        + [pl.BlockSpec((tm // DN_CHUNK, 16, DN_CHUNK), lambda i: (i, 0, 0))],
        scratch_shapes=[pltpu.VMEM((2, tm, 128), F32)],
        compiler_params=_cparams(("parallel",), VMEM_LIMIT),
        name="in_projection",
    )(*xs, halo_src, halo_src, mods_l, norm1.reshape(1, D_MODEL), w_in_bf, w8, pad(a_log), pad(dt_bias))


def _s5_prep_kernel(lre_ref, lim_ref, ldt_ref, btre_ref, btim_ref, cre_ref, cim_ref,
                    mt_ref, pt_ref, qt_ref, at_ref):
    t_n, hg, w = S5_T, SSM_GROUP_CH, S5_W
    lre, lim = lre_ref[0], lim_ref[0]
    dt = jnp.exp(ldt_ref[0])
    xr, xi = lre * dt, lim * dt
    tau = lax.broadcasted_iota(jnp.int32, (t_n, 128), 0).astype(F32)
    mag = jnp.exp(tau * xr)
    a_re, a_im = mag * jnp.cos(tau * xi), mag * jnp.sin(tau * xi)
    ar, ai = jnp.exp(xr) * jnp.cos(xi), jnp.exp(xr) * jnp.sin(xi)
    inv = 1.0 / (lre * lre + lim * lim)
    br_, bi_ = lre * inv, -lim * inv
    fr = (ar - 1.0) * br_ - ai * bi_
    fi = (ar - 1.0) * bi_ + ai * br_
    btre, btim = btre_ref[0], btim_ref[0]
    bbr = fr * btre - fi * btim
    bbi = fr * btim + fi * btre
    cre, cim = cre_ref[0], cim_ref[0]

    lane = lax.broadcasted_iota(jnp.int32, (1, 128), 1)
    is_f = lane < SSM_STATE

    def rep_rows(a):
        return jnp.concatenate([jnp.broadcast_to(a[t:t + 1], (hg, 128)) for t in range(t_n)], axis=0)

    def tile_rows(c):
        return jnp.concatenate([c] * t_n, axis=0)

    def rev_rows(a):
        return jnp.concatenate([a[t_n - 1 - t:t_n - t] for t in range(t_n)], axis=0)

    cr_t, ci_t = tile_rows(cre), tile_rows(cim)
    a_re_rep, a_im_rep = rep_rows(a_re), rep_rows(a_im)
    w_re = cr_t * a_re_rep - ci_t * a_im_rep
    w_im = cr_t * a_im_rep + ci_t * a_re_rep
    zero = jnp.zeros_like(w_re)
    e_f = jnp.concatenate([jnp.where(is_f, w_re, zero), jnp.where(is_f, -w_im, zero)], axis=1)
    e_b = jnp.concatenate([jnp.where(is_f, zero, w_re), jnp.where(is_f, zero, -w_im)], axis=1)
    e_b_rev = jnp.concatenate([e_b[(t_n - 1 - t) * hg:(t_n - t) * hg] for t in range(t_n)], axis=0)
    bb_t = tile_rows(jnp.concatenate([bbr, bbi], axis=1))
    row_blk = lax.broadcasted_iota(jnp.int32, (w, w), 0) // hg

    m = jnp.zeros((w, w), F32)
    zrows = lambda n: jnp.zeros((n * hg, w), F32)
    for s in range(t_n):
        down = e_f if s == 0 else jnp.concatenate([zrows(s), e_f[:(t_n - s) * hg]], axis=0)
        k = t_n - 1 - s
        up = e_b_rev if k == 0 else jnp.concatenate([e_b_rev[k * hg:], zrows(k)], axis=0)
        rhs = jnp.where(row_blk == s, bb_t, 0.0)
        m = m + _dot_nt((down + up).astype(BF16), rhs.astype(BF16))
    mt_ref[...] = m.T.reshape(t_n, hg, w).astype(BF16)

    a_re_rev, a_im_rev = rev_rows(a_re), rev_rows(a_im)
    ps_re = rep_rows(jnp.where(is_f, a_re_rev, a_re))
    ps_im = rep_rows(jnp.where(is_f, a_im_rev, a_im))
    bbr_t, bbi_t = tile_rows(bbr), tile_rows(bbi)
    p_re = ps_re * bbr_t - ps_im * bbi_t
    p_im = ps_re * bbi_t + ps_im * bbr_t
    pt_ref[...] = jnp.concatenate([p_re, p_im], axis=1).reshape(t_n, hg, w).astype(BF16)

    a1_re = a_re * ar - a_im * ai
    a1_im = a_re * ai + a_im * ar
    aq_re = rep_rows(jnp.where(is_f, a1_re, rev_rows(a1_re)))
    aq_im = rep_rows(jnp.where(is_f, a1_im, rev_rows(a1_im)))
    q_re = cr_t * aq_re - ci_t * aq_im
    q_im = cr_t * aq_im + ci_t * aq_re
    qt_ref[0] = jnp.concatenate([q_re, -q_im], axis=1).T.astype(BF16)
    at_ref[0] = jnp.concatenate([a1_re[t_n - 1:t_n], a1_im[t_n - 1:t_n]], axis=1)


def _s5_prep(lam_re, lam_im, log_dt, b_re, b_im, c_re, c_im):
    g = SSM_GROUPS
    cat = lambda a: jnp.concatenate([a[0], a[1]], axis=-1)
    lre = cat(lam_re)[:, None, :]
    lim = cat(lam_im)[:, None, :]
    ldt = cat(jnp.broadcast_to(log_dt[..., None], (2, g, SSM_STATE)))[:, None, :]
    btre = cat(jnp.swapaxes(b_re, -1, -2))
    btim = cat(jnp.swapaxes(b_im, -1, -2))
    cre, cim = cat(c_re), cat(c_im)
    vec = pl.BlockSpec((1, 1, 128), lambda i: (i, 0, 0))
    mat = pl.BlockSpec((1, SSM_GROUP_CH, 128), lambda i: (i, 0, 0))
    big = pl.BlockSpec((1, S5_W, S5_W), lambda i: (i, 0, 0))
    gh = g // 2
    arranged = pl.BlockSpec((None, S5_T, None, SSM_GROUP_CH, S5_W), lambda i: (i // gh, 0, i % gh, 0, 0))
    arranged_shape = jax.ShapeDtypeStruct((2, S5_T, gh, SSM_GROUP_CH, S5_W), BF16)
    return pl.pallas_call(
        _s5_prep_kernel,
        out_shape=[arranged_shape, arranged_shape, jax.ShapeDtypeStruct((g, S5_W, S5_W), BF16),
                   jax.ShapeDtypeStruct((g, 1, 256), F32)],
        grid=(g,),
        in_specs=[vec, vec, vec, mat, mat, mat, mat],
        out_specs=[arranged, arranged, big, pl.BlockSpec((1, 1, 256), lambda i: (i, 0, 0))],
        compiler_params=_cparams(("parallel",)),
        name="s5_prep",
    )(lre, lim, ldt, btre, btim, cre, cim)


S5_SPREAD_ROWS = 512


def _s5_spread_kernel(a_ref, o_ref, *, rows_per_group, gh):
    hg, w = SSM_GROUP_CH, S5_W
    n_rows, n_cols = o_ref.shape
    src = lax.broadcasted_iota(jnp.int32, (w, n_cols), 0)
    dst = lax.broadcasted_iota(jnp.int32, (w, n_cols), 1)
    tile = jnp.where((src // hg == dst // 128) & (src % hg == dst % hg), 1.0, 0.0).astype(BF16)
    row = pl.program_id(1) * n_rows + lax.broadcasted_iota(jnp.int32, (n_rows, n_cols), 0)
    col = lax.broadcasted_iota(jnp.int32, (n_rows, n_cols), 1)
    keep = (row // rows_per_group) % gh == (col // hg) % gh
    o_ref[...] = jnp.where(keep, _dot(a_ref[...], tile), 0.0).astype(BF16)


def _s5_spread_cols(a, rows_per_group, gh):
    _, n_rows, w = a.shape
    rb = S5_SPREAD_ROWS
    return pl.pallas_call(
        functools.partial(_s5_spread_kernel, rows_per_group=rows_per_group, gh=gh),
        out_shape=jax.ShapeDtypeStruct((2, n_rows, S5_T * 128), BF16),
        grid=(2, n_rows // rb),
        in_specs=[pl.BlockSpec((None, rb, w), lambda z, i: (z, i, 0))],
        out_specs=pl.BlockSpec((None, rb, S5_T * 128), lambda z, i: (z, i, 0)),
        compiler_params=_cparams(("parallel", "parallel")),
        name="s5_spread_cols",
    )(a)


def _s5_place_kernel(a_ref, o_ref, *, gh):
    a = a_ref[...]
    n_rows, w = a.shape
    grp = ((pl.program_id(1) * n_rows + lax.broadcasted_iota(jnp.int32, (n_rows, w), 0))
           // SSM_GROUP_CH) % gh
    for k in range(gh):
        o_ref[:, k * w:(k + 1) * w] = jnp.where(grp == k, a, jnp.zeros_like(a))


def _s5_place_cols(a, gh):
    _, n_rows, w = a.shape
    rb = S5_SPREAD_ROWS
    return pl.pallas_call(
        functools.partial(_s5_place_kernel, gh=gh),
        out_shape=jax.ShapeDtypeStruct((2, n_rows, gh * w), BF16),
        grid=(2, n_rows // rb),
        in_specs=[pl.BlockSpec((None, rb, w), lambda z, i: (z, i, 0))],
        out_specs=pl.BlockSpec((None, rb, gh * w), lambda z, i: (z, i, 0)),
        compiler_params=_cparams(("parallel", "parallel")),
        name="s5_place_cols",
    )(a)


def _s5_z_kernel(u_ref, pt_ref, z_ref):
    z_ref[0] = _dot(u_ref[0].astype(BF16), pt_ref[0])


def _s5_rec_kernel(z_ref, a_ref, h0_ref, hf_ref, hb_ref, fin_ref, *, n_ctx_seq, ctx_chunks, n_lat_seq,
                   lat_chunks):
    sw = z_ref.shape[-1]
    ng = sw // 256
    lane = lax.broadcasted_iota(jnp.int32, (1, sw), 1)
    is_f = (lane % 128) < SSM_STATE

    def split(row):
        re = jnp.concatenate([row[:, g * 256:g * 256 + 128] for g in range(ng)], axis=1)
        im = jnp.concatenate([row[:, g * 256 + 128:(g + 1) * 256] for g in range(ng)], axis=1)
        return re, im

    def merge(re, im):
        parts = []
        for g in range(ng):
            parts += [re[:, g * 128:(g + 1) * 128], im[:, g * 128:(g + 1) * 128]]
        return jnp.concatenate(parts, axis=1)

    a_re, a_im = split(a_ref[0])

    def run_seq(base, n, seq):
        def step(k, h):
            h_re, h_im = h
            cf = base + k
            cb = base + n - 1 - k
            row = merge(h_re, h_im)
            hf_ref[0, pl.ds(cf, 1), :] = row
            hb_ref[0, pl.ds(cb, 1), :] = row
            z_re, z_im = split(jnp.where(is_f, z_ref[0, pl.ds(cf, 1), :], z_ref[0, pl.ds(cb, 1), :]))
            return (a_re * h_re - a_im * h_im + z_re, a_re * h_im + a_im * h_re + z_im)
        h_re, h_im = lax.fori_loop(0, n, step, split(h0_ref[0, pl.ds(seq, 1), :]))
        fin_ref[0, pl.ds(seq, 1), :] = merge(h_re, h_im)

    def ctx_body(s, carry):
        run_seq(s * ctx_chunks, ctx_chunks, s)
        return carry
    lax.fori_loop(0, n_ctx_seq, ctx_body, 0)

    def lat_body(s, carry):
        run_seq(n_ctx_seq * ctx_chunks + s * lat_chunks, lat_chunks, n_ctx_seq + s)
        return carry
    lax.fori_loop(0, n_lat_seq, lat_body, 0)


def _s5_y_kernel(u_ref, hf_ref, hb_ref, mt_ref, qt_ref, d_ref, y_ref):
    u = u_ref[0]
    lane = lax.broadcasted_iota(jnp.int32, (1, u.shape[-1]), 1)
    is_f = (lane % 128) < SSM_STATE
    h = jnp.where(is_f, hf_ref[0], hb_ref[0]).astype(BF16)
    y_ref[0] = _dot(u.astype(BF16), mt_ref[0]) + _dot(h, qt_ref[0]) + d_ref[0] * u


def _s5_mixer(s_u, mats, ssm_d, h0, n_ctx_seq, ctx_seq, lat_seq):
    mt, pt, qt, at = mats
    g, t_n, w, hg = SSM_GROUPS, S5_T, S5_W, SSM_GROUP_CH
    gh = g // 2
    nc = s_u.shape[1]
    bw = t_n * 128
    sw = gh * w
    n_seq = h0.shape[0]
    ctx_chunks, lat_chunks = ctx_seq // t_n, lat_seq // t_n
    n_lat_seq = n_seq - n_ctx_seq
    u = s_u
    m_big = _s5_spread_cols(mt.reshape(2, bw, w), hg, gh)
    q_big = _s5_spread_cols(qt.reshape(2, sw, w), w, gh)
    p_big = _s5_place_cols(pt.reshape(2, bw, w), gh)
    d_big = jnp.tile(ssm_d.reshape(2, 1, 128), (1, 1, t_n))

    rb = math.gcd(nc, 256)
    u_spec = pl.BlockSpec((1, rb, bw), lambda z, i: (z, i, 0))
    s_spec = pl.BlockSpec((1, rb, sw), lambda z, i: (z, i, 0))
    z = pl.pallas_call(
        _s5_z_kernel,
        out_shape=jax.ShapeDtypeStruct((2, nc, sw), F32),
        grid=(2, nc // rb),
        in_specs=[u_spec, pl.BlockSpec((1, bw, sw), lambda z, i: (z, 0, 0))],
        out_specs=s_spec,
        compiler_params=_cparams(("parallel", "parallel"), VMEM_LIMIT),
        name="s5_chunk_inputs",
    )(u, p_big)

    half = lambda rows: pl.BlockSpec((1, rows, sw), lambda z: (z, 0, 0))
    hf, hb, fin = pl.pallas_call(
        functools.partial(_s5_rec_kernel, n_ctx_seq=n_ctx_seq, ctx_chunks=ctx_chunks,
                          n_lat_seq=n_lat_seq, lat_chunks=lat_chunks),
        out_shape=[jax.ShapeDtypeStruct((2, nc, sw), F32)] * 2 + [jax.ShapeDtypeStruct((2, n_seq, sw), F32)],
        grid=(2,),
        in_specs=[half(nc), half(1), half(n_seq)],
        out_specs=[half(nc), half(nc), half(n_seq)],
        compiler_params=_cparams(("parallel",), VMEM_LIMIT),
        name="s5_recurrence",
    )(z, at.reshape(2, 1, sw), h0.reshape(n_seq, 2, sw).transpose(1, 0, 2))

    y = pl.pallas_call(
        _s5_y_kernel,
        out_shape=jax.ShapeDtypeStruct((2, nc, bw), F32),
        grid=(2, nc // rb),
        in_specs=[u_spec, s_spec, s_spec,
                  pl.BlockSpec((1, bw, bw), lambda z, i: (z, 0, 0)),
                  pl.BlockSpec((1, sw, bw), lambda z, i: (z, 0, 0)),
                  pl.BlockSpec((1, 1, bw), lambda z, i: (z, 0, 0))],
        out_specs=u_spec,
        compiler_params=_cparams(("parallel", "parallel"), VMEM_LIMIT),
        name="s5_outputs",
    )(u, hf, hb, m_big, q_big, d_big)
    return y, fin.transpose(1, 0, 2).reshape(n_seq, g, w)


DN_CHAINS = 2 * DN_HEADS


def _split(x):
    hi = x.astype(BF16)
    lo = (x - hi.astype(F32)).astype(BF16)
    return hi, lo


def _split_lhs(a):
    hi, lo = _split(a)
    hif = hi.astype(F32)
    return jnp.concatenate([hif, hif, lo.astype(F32)], axis=1).astype(BF16)


def _split_rhs(b):
    hi, lo = _split(b)
    return jnp.concatenate([hi, lo, hi], axis=0)


def _dn_prepare(q_ref, k_ref, v_ref, gc_ref, gt_ref, row0, chunk_idx):
    c = DN_CHUNK
    ii = lax.broadcasted_iota(jnp.int32, (c, c), 0)
    jj = lax.broadcasted_iota(jnp.int32, (c, c), 1)
    items = []
    for d in range(2):
        rows = pl.ds(row0[d], c)
        gcol = gc_ref[rows, :]
        grow = gt_ref[chunk_idx[d]]
        incl = (ii >= jj) if d == 0 else (ii <= jj)
        strict = (ii > jj) if d == 0 else (ii < jj)
        for h in range(DN_HEADS):
            ch = d * DN_HEADS + h
            cols = slice(h * DN_HEAD_DIM, (h + 1) * DN_HEAD_DIM)
            q, k, v = q_ref[rows, cols], k_ref[rows, cols], v_ref[rows, cols]
            gc_col = gcol[:, ch:ch + 1]
            beta = gcol[:, DN_CHAINS + ch:DN_CHAINS + ch + 1]
            gc_row = grow[ch:ch + 1, :]
            kf = k.astype(F32)
            kb = kf * beta
            decay = jnp.exp(jnp.where(incl, gc_col - gc_row, -jnp.inf))
            egc = jnp.exp(gc_col)
            g_last = gc_col[c - 1:c] if d == 0 else gc_col[0:1]
            items.append(dict(
                q=q, k=k, kb=kb.astype(BF16), decay=decay, incl=incl, strict=strict,
                rhs=jnp.concatenate([kb * egc, v.astype(F32) * beta], axis=1),
                qg=(q.astype(F32) * egc).astype(BF16),
                kt=(kf * jnp.exp(g_last - gc_col)).astype(BF16),
                gl=jnp.broadcast_to(jnp.exp(g_last), (8, 128))))
    for it in items:
        it['a'] = jnp.where(it['strict'], _dot_nt(it['kb'], it['k']) * it['decay'], 0.0)
        it['qk'] = jnp.where(it['incl'], _dot_nt(it['q'], it['k']) * it['decay'], 0.0).astype(BF16)
    power = 1
    while power < c:
        for it in items:
            it['lhs'] = _split_lhs(it['a'])
        for it in items:
            upd = _dot(it['lhs'], _split_rhs(it['rhs'] if power == 1 else it['r']))
            it['r'] = it['rhs'] - upd if power == 1 else it['r'] + upd
        if 2 * power < c:
            for it in items:
                it['a'] = _dot(it['lhs'], _split_rhs(it['a']))
        power *= 2
    return items


def _dn_scan_kernel(*refs, n_chunks, seq_len, seqs_per_blk, has_s0, write_fin):
    refs = list(refs)
    q_ref, k_ref, v_ref, gc_ref, gt_ref = refs[:5]
    pos = 5
    s0_ref = None
    if has_s0:
        s0_ref = refs[pos]
        pos += 1
    o_ref = refs[pos]
    pos += 1
    fin_ref = None
    if write_fin:
        fin_ref = refs[pos]
        pos += 1
    s_ref, w_s, u_s, qk_s, qg_s, kt_s, gl_s = refs[pos:pos + 7]
    n_steps = seqs_per_blk * n_chunks

    o_ref[...] = jnp.zeros_like(o_ref)
    for r in (s_ref, w_s, u_s, qk_s, qg_s, kt_s, gl_s):
        r[...] = jnp.zeros_like(r)

    def locate(step):
        seq = step // n_chunks
        kk = step % n_chunks
        cf = seq * n_chunks + kk
        cb = seq * n_chunks + (n_chunks - 1 - kk)
        rows = [pl.multiple_of(cf * DN_CHUNK, DN_CHUNK), pl.multiple_of(cb * DN_CHUNK, DN_CHUNK)]
        return seq, kk, rows, [cf, cb]

    def body(t, carry):
        tb = jnp.maximum(t, 0)
        valid = t >= 0
        seq, kk, rows_b, _ = locate(tb)
        first = kk == 0
        states, vnews = [], []
        for ch in range(DN_CHAINS):
            d, h = divmod(ch, DN_HEADS)
            s0 = s0_ref[0, d, h] if has_s0 else jnp.zeros((DN_HEAD_DIM, DN_HEAD_DIM), F32)
            states.append(jnp.where(first, s0, s_ref[ch]))
        for ch in range(DN_CHAINS):
            vnews.append((u_s[ch] - _dot(w_s[ch], states[ch].astype(BF16))).astype(BF16))
        tn = jnp.minimum(t + 1, n_steps - 1)
        _, _, rows_a, chunks_a = locate(tn)
        items = _dn_prepare(q_ref, k_ref, v_ref, gc_ref, gt_ref, rows_a, chunks_a)
        for ch in range(DN_CHAINS):
            d, h = divmod(ch, DN_HEADS)
            sb = states[ch].astype(BF16)
            o = _dot(qg_s[ch], sb) + _dot(qk_s[ch], vnews[ch])
            s_new = states[ch] * gl_s[ch][0:1, 0:1] + _dot_tn(kt_s[ch], vnews[ch])
            s_ref[ch] = s_new
            if write_fin:
                fin_ref[seq, d, h] = s_new
            rows = pl.ds(rows_b[d], DN_CHUNK)
            cols = slice(h * DN_HEAD_DIM, (h + 1) * DN_HEAD_DIM)
            o_ref[rows, cols] = o_ref[rows, cols] + jnp.where(valid, o, 0.0)
        for ch, it in enumerate(items):
            w_s[ch] = it['r'][:, :DN_HEAD_DIM].astype(BF16)
            u_s[ch] = it['r'][:, DN_HEAD_DIM:]
            qk_s[ch] = it['qk']
            qg_s[ch] = it['qg']
            kt_s[ch] = it['kt']
            gl_s[ch] = it['gl']
        return carry

    lax.fori_loop(-1, n_steps, body, 0)


def _dn_scan(qkvn, gcb, gcb_t, seq_len, n_seq, seqs_per_blk, first_tok, s0, write_fin):
    n_chunks = seq_len // DN_CHUNK
    rows_blk = seqs_per_blk * seq_len
    b0 = first_tok // rows_blk
    hd = DN_HEAD_DIM
    st_spec = pl.BlockSpec((seqs_per_blk, 2, DN_HEADS, hd, hd), lambda s: (s, 0, 0, 0, 0))
    in_specs = [pl.BlockSpec((rows_blk, DN_WIDTH), lambda s: (b0 + s, 0)),
                pl.BlockSpec((rows_blk, DN_WIDTH), lambda s: (b0 + s, 1)),
                pl.BlockSpec((rows_blk, DN_WIDTH), lambda s: (b0 + s, 2)),
                pl.BlockSpec((rows_blk, 128), lambda s: (b0 + s, 0)),
                pl.BlockSpec((rows_blk // DN_CHUNK, 16, DN_CHUNK), lambda s: (b0 + s, 0, 0))]
    args = [qkvn, qkvn, qkvn, gcb, gcb_t]
    if s0 is not None:
        assert seqs_per_blk == 1
        in_specs.append(st_spec)
        args.append(s0)
    out_shape = [jax.ShapeDtypeStruct((n_seq * seq_len, DN_WIDTH), F32)]
    out_specs = [pl.BlockSpec((rows_blk, DN_WIDTH), lambda s: (s, 0))]
    if write_fin:
        out_shape.append(jax.ShapeDtypeStruct((n_seq, 2, DN_HEADS, hd, hd), F32))
        out_specs.append(st_spec)
    c = DN_CHUNK
    scratch = [pltpu.VMEM((DN_CHAINS, hd, hd), F32),
               pltpu.VMEM((DN_CHAINS, c, hd), BF16),
               pltpu.VMEM((DN_CHAINS, c, hd), F32),
               pltpu.VMEM((DN_CHAINS, c, c), BF16),
               pltpu.VMEM((DN_CHAINS, c, hd), BF16),
               pltpu.VMEM((DN_CHAINS, c, hd), BF16),
               pltpu.VMEM((DN_CHAINS, 8, 128), F32)]
    return pl.pallas_call(
        functools.partial(_dn_scan_kernel, n_chunks=n_chunks, seq_len=seq_len, seqs_per_blk=seqs_per_blk,
                          has_s0=s0 is not None, write_fin=write_fin),
        out_shape=out_shape,
        grid=(n_seq // seqs_per_blk,),
        in_specs=in_specs,
        out_specs=out_specs,
        scratch_shapes=scratch,
        compiler_params=_cparams(("parallel",), VMEM_LIMIT),
        name="dn_scan_%d" % seq_len,
    )(*args)


def _deltanet(qkvn, gcb, gcb_t, s0_lat, n_ctx_seq, ctx_seq, n_lat_seq, lat_seq):
    n_ctx_tok = n_ctx_seq * ctx_seq
    ctx_blk = math.gcd(n_ctx_seq, 8)
    o_ctx, fin = _dn_scan(qkvn, gcb, gcb_t, ctx_seq, n_ctx_seq, ctx_blk, 0, None, True)
    (o_lat,) = _dn_scan(qkvn, gcb, gcb_t, lat_seq, n_lat_seq, 1, n_ctx_tok, s0_lat, False)
    return o_ctx, o_lat, fin


MIX_TILE = 256


def _mix_kernel(*refs, n_x, n_ctx_tiles):
    x_refs = refs[:n_x]
    (mod_ref, ys_ref, octx_ref, olat_ref, z_ref, cu_ref, cv_ref,
     wglu_ref, bglu_ref, dnn_ref, cmn_ref, ws_ref, bs_ref, wo_ref, n2_ref,
     x1_ref, h2_ref, scr) = refs[n_x:]
    mod = mod_ref[0]
    gate1 = mod[:, 2 * D_MODEL:3 * D_MODEL]
    shift2 = mod[:, 3 * D_MODEL:4 * D_MODEL]
    scale2 = mod[:, 4 * D_MODEL:5 * D_MODEL]
    n_rows = ys_ref.shape[1]
    for half in range(2):
        for t in range(S5_T):
            scr[half, pl.ds(t, n_rows, stride=S5_T), :] = ys_ref[half, :, t * 128:(t + 1) * 128]
    ya = _gelu(jnp.concatenate([scr[0], scr[1]], axis=1))
    ya = ya * _sigmoid(_dot(ya.astype(BF16), wglu_ref[...]) + bglu_ref[...])
    mix = _dot(ya.astype(BF16), wo_ref[0:SSM_CH, :])
    o = jnp.where(pl.program_id(0) < n_ctx_tiles, octx_ref[...], olat_ref[...])
    z = z_ref[...]
    parts = []
    for h in range(DN_HEADS):
        sl = slice(h * DN_HEAD_DIM, (h + 1) * DN_HEAD_DIM)
        parts.append(_rms(o[:, sl]) * dnn_ref[...] * _silu(z[:, sl]))
    yb = jnp.concatenate(parts, axis=1)
    mix = mix + _dot(yb.astype(BF16), wo_ref[SSM_CH:SSM_CH + DN_WIDTH, :])
    u = _gelu(cu_ref[...])
    v = _gelu(cv_ref[...])
    seg_i = lax.broadcasted_iota(jnp.int32, (CM_WIDTH, CM_WIDTH), 0) // CM_HEAD_DIM
    seg_j = lax.broadcasted_iota(jnp.int32, (CM_WIDTH, CM_WIDTH), 1) // CM_HEAD_DIM
    seg_ones = jnp.where(seg_i == seg_j, 1.0, 0.0).astype(BF16)
    vv = v * v
    vv_hi = vv.astype(BF16)
    vv_lo = (vv - vv_hi.astype(F32)).astype(BF16)
    ms = (_dot(vv_hi, seg_ones) + _dot(vv_lo, seg_ones)) * (1.0 / CM_HEAD_DIM)
    vn = (v * lax.rsqrt(ms + EPS) * cmn_ref[...]).astype(BF16)
    head_of_lane = lax.broadcasted_iota(jnp.int32, (1, CM_WIDTH), 1) // CM_HEAD_DIM
    ycs = []
    for c in range(MIX_TILE // CM_CHUNK):
        vc = vn[c * CM_CHUNK:(c + 1) * CM_CHUNK]
        stack = jnp.concatenate([jnp.where(head_of_lane == h, vc, jnp.zeros_like(vc))
                                 for h in range(CM_HEADS)], axis=0)
        vs = _dot(ws_ref[...], stack) + bs_ref[...]
        ycs.append(u[c * CM_CHUNK:(c + 1) * CM_CHUNK] * vs)
    yc = jnp.concatenate(ycs, axis=0)
    mix = mix + _dot(yc.astype(BF16), wo_ref[SSM_CH + DN_WIDTH:, :])
    x1 = _token_tile(x_refs, n_ctx_tiles) + gate1 * mix
    x1_ref[...] = x1
    h2_ref[...] = (_rms(x1) * n2_ref[...] * (1.0 + scale2) + shift2).astype(BF16)


def _mix_out(xs, mods_l, ys, o_ctx, o_lat, z, cu, cv, p, n_ctx_tok, lat_seq):
    n_tok = sum(x.shape[0] for x in xs)
    tm = MIX_TILE
    n_ctx_tiles = n_ctx_tok // tm
    tps = lat_seq // tm
    row = lambda i: (i, 0)
    const = lambda i: (0, 0)
    ctx_row = lambda i: (jnp.minimum(i, n_ctx_tiles - 1), 0)
    lat_row = lambda i: (jnp.maximum(i - n_ctx_tiles, 0), 0)
    ws_cat = jnp.concatenate([p['cm_w_s'][h] for h in range(CM_HEADS)], axis=1).astype(BF16)
    bs_exp = jnp.repeat(p['cm_b_s'].T, CM_HEAD_DIM, axis=1)
    return pl.pallas_call(
        functools.partial(_mix_kernel, n_x=len(xs), n_ctx_tiles=n_ctx_tiles),
        out_shape=[jax.ShapeDtypeStruct((n_tok, D_MODEL), F32),
                   jax.ShapeDtypeStruct((n_tok, D_MODEL), BF16)],
        grid=(n_tok // tm,),
        in_specs=_token_specs(xs, tm, n_ctx_tiles, D_MODEL)
        + [pl.BlockSpec((1, 1, N_MOD * D_MODEL),
                               lambda i: (_mod_row_index(i, n_ctx_tiles, tps), 0, 0)),
                  pl.BlockSpec((2, tm // S5_T, S5_T * 128), lambda i: (0, i, 0)),
                  pl.BlockSpec((tm, DN_WIDTH), ctx_row),
                  pl.BlockSpec((tm, DN_WIDTH), lat_row),
                  pl.BlockSpec((tm, DN_WIDTH), row),
                  pl.BlockSpec((tm, CM_WIDTH), row),
                  pl.BlockSpec((tm, CM_WIDTH), row),
                  pl.BlockSpec((SSM_CH, SSM_CH), const),
                  pl.BlockSpec((1, SSM_CH), const),
                  pl.BlockSpec((1, DN_HEAD_DIM), const),
                  pl.BlockSpec((1, CM_WIDTH), const),
                  pl.BlockSpec((CM_CHUNK, CM_HEADS * CM_CHUNK), const),
                  pl.BlockSpec((CM_CHUNK, CM_WIDTH), const),
                  pl.BlockSpec((D_MODEL, D_MODEL), const),
                  pl.BlockSpec((1, D_MODEL), const)],
        out_specs=[pl.BlockSpec((tm, D_MODEL), row), pl.BlockSpec((tm, D_MODEL), row)],
        scratch_shapes=[pltpu.VMEM((2, tm, 128), F32)],
        compiler_params=_cparams(("parallel",), VMEM_LIMIT),
        name="mix_out",
    )(*xs, mods_l, ys, o_ctx, o_lat, z, cu, cv,
      p['ssm_w_glu'].astype(BF16), p['ssm_b_glu'].reshape(1, SSM_CH),
      p['dn_norm'].reshape(1, DN_HEAD_DIM), p['cm_norm'].reshape(1, CM_WIDTH),
      ws_cat, bs_exp, p['w_out'].astype(BF16), p['norm2'].reshape(1, D_MODEL))


ROUTE_TILE = 512


def _first_max_mask(x, idx, axis):
    m = jnp.max(x, axis=axis, keepdims=True)
    first = jnp.min(jnp.where(x == m, idx, jnp.int32(1 << 20)), axis=axis, keepdims=True)
    return idx == first


def _router_kernel(h_ref, wr_ref, bias_ref, g_ref):
    n = h_ref.shape[0]
    per = N_EXPERTS // N_ROUTE_GROUPS
    logits = _dot_nt(wr_ref[...], h_ref[...])
    scores = _sigmoid(logits)
    choice = scores + bias_ref[...]
    ninf = jnp.float32(-jnp.inf)
    c3 = choice.reshape(N_ROUTE_GROUPS, per, n)
    i3 = lax.broadcasted_iota(jnp.int32, c3.shape, 1)
    top1 = _first_max_mask(c3, i3, 1)
    m1 = jnp.max(c3, axis=1)
    m2 = jnp.max(jnp.where(top1, ninf, c3), axis=1)
    grp = m1 + m2
    gi = lax.broadcasted_iota(jnp.int32, grp.shape, 0)
    gsel = jnp.zeros(grp.shape, jnp.bool_)
    work = grp
    for _ in range(TOPK_ROUTE_GROUPS):
        pick = _first_max_mask(work, gi, 0)
        gsel = gsel | pick
        work = jnp.where(pick, ninf, work)
    gmask = jnp.broadcast_to(gsel[:, None, :], c3.shape).reshape(N_EXPERTS, n)
    work = jnp.where(gmask, choice, ninf)
    ei = lax.broadcasted_iota(jnp.int32, work.shape, 0)
    esel = jnp.zeros(work.shape, jnp.bool_)
    for _ in range(TOP_K):
        pick = _first_max_mask(work, ei, 0)
        esel = esel | pick
        work = jnp.where(pick, ninf, work)
    wts = jnp.where(esel, scores, 0.0)
    g_ref[...] = wts / (jnp.sum(wts, axis=0, keepdims=True) + 1e-20) * ROUTED_SCALE


def _router(h2, router_w, router_bias):
    n_tok = h2.shape[0]
    tm = ROUTE_TILE
    return pl.pallas_call(
        _router_kernel,
        out_shape=jax.ShapeDtypeStruct((N_EXPERTS, n_tok), F32),
        grid=(n_tok // tm,),
        in_specs=[pl.BlockSpec((tm, D_MODEL), lambda i: (i, 0)),
                  pl.BlockSpec((N_EXPERTS, D_MODEL), lambda i: (0, 0)),
                  pl.BlockSpec((N_EXPERTS, 1), lambda i: (0, 0))],
        out_specs=pl.BlockSpec((N_EXPERTS, tm), lambda i: (0, i)),
        compiler_params=_cparams(("parallel",)),
        name="router",
    )(h2, router_w.T.astype(BF16), router_bias.reshape(N_EXPERTS, 1))


MOE_TILE = 1024
MOE_EG = 8


def _moe_kernel(h_ref, x_ref, mod_ref, g_ref, wg_ref, wu_ref, wd_ref, sg_ref, su_ref, sd_ref, nf_ref,
                *out_refs, final):
    j = pl.program_id(1)
    nj = pl.num_programs(1)
    acc_ref = out_refs[-1]
    h = h_ref[...]

    @pl.when(j == 0)
    def _():
        sh = _silu(_dot(h, sg_ref[...])) * _dot(h, su_ref[...])
        acc_ref[...] = _dot(sh.astype(BF16), sd_ref[...])

    hg = _dot(h, wg_ref[...])
    hu = _dot(h, wu_ref[...])
    gates = g_ref[0]
    cols = []
    for e in range(MOE_EG):
        sl = slice(e * EXPERT_FF, (e + 1) * EXPERT_FF)
        cols.append((_silu(hg[:, sl]) * hu[:, sl] * gates[:, e:e + 1]).astype(BF16))
    act = jnp.concatenate(cols, axis=1)
    acc_ref[...] += _dot(act, wd_ref[...])

    @pl.when(j == nj - 1)
    def _():
        gate2 = mod_ref[0][:, 5 * D_MODEL:6 * D_MODEL]
        x2 = x_ref[...] + gate2 * acc_ref[...]
        if final:
            out_refs[0][...] = _rms(x2) * nf_ref[...]
        else:
            out_refs[0][...] = x2


def _expert_cols_kernel(w_ref, o_ref):
    for e in range(w_ref.shape[0]):
        o_ref[:, e * EXPERT_FF:(e + 1) * EXPERT_FF] = w_ref[e].astype(BF16)


def _expert_cols(w, layer):
    _, n_e, d, f = w.shape
    return pl.pallas_call(
        _expert_cols_kernel,
        out_shape=jax.ShapeDtypeStruct((d, n_e * f), BF16),
        grid=(n_e // MOE_EG,),
        in_specs=[pl.BlockSpec((None, MOE_EG, d, f), lambda j: (layer, j, 0, 0))],
        out_specs=pl.BlockSpec((d, MOE_EG * f), lambda j: (0, j)),
        compiler_params=_cparams(("parallel",)),
        name="expert_cols",
    )(w)


def _cast_rows_kernel(w_ref, o_ref):
    o_ref[...] = w_ref[...].astype(BF16)


def _cast_rows(w, layer, rows):
    _, r, c = w.shape
    return pl.pallas_call(
        _cast_rows_kernel,
        out_shape=jax.ShapeDtypeStruct((r, c), BF16),
        grid=(r // rows,),
        in_specs=[pl.BlockSpec((None, rows, c), lambda j: (layer, j, 0))],
        out_specs=pl.BlockSpec((rows, c), lambda j: (j, 0)),
        compiler_params=_cparams(("parallel",)),
        name="cast_rows",
    )(w)


def _moe(h2, x1, mods_l, gates_t, w_gate, w_up, w_down, layer, p, norm_f, final, n_ctx_tok, lat_seq):
    n_tok = h2.shape[0]
    tm = MOE_TILE
    n_ctx_tiles = n_ctx_tok // tm
    tps = lat_seq // tm
    neg = N_EXPERTS // MOE_EG
    ef = MOE_EG * EXPERT_FF
    gates = gates_t.reshape(neg, MOE_EG, n_tok).transpose(0, 2, 1)
    wg = _expert_cols(w_gate, layer)
    wu = _expert_cols(w_up, layer)
    wd = _cast_rows(w_down.reshape(w_down.shape[0], N_EXPERTS * EXPERT_FF, D_MODEL), layer, 1024)
    row = lambda i, j: (i, 0)
    const = lambda i, j: (0, 0)
    out = pl.pallas_call(
        functools.partial(_moe_kernel, final=final),
        out_shape=jax.ShapeDtypeStruct((n_tok, D_MODEL), F32),
        grid=(n_tok // tm, neg),
        in_specs=[pl.BlockSpec((tm, D_MODEL), row),
                  pl.BlockSpec((tm, D_MODEL), row),
                  pl.BlockSpec((1, 1, N_MOD * D_MODEL),
                               lambda i, j: (_mod_row_index(i, n_ctx_tiles, tps), 0, 0)),
                  pl.BlockSpec((1, tm, MOE_EG), lambda i, j: (j, i, 0)),
                  pl.BlockSpec((D_MODEL, ef), lambda i, j: (0, j)),
                  pl.BlockSpec((D_MODEL, ef), lambda i, j: (0, j)),
                  pl.BlockSpec((ef, D_MODEL), lambda i, j: (j, 0)),
                  pl.BlockSpec((D_MODEL, SHARED_FF), const),
                  pl.BlockSpec((D_MODEL, SHARED_FF), const),
                  pl.BlockSpec((SHARED_FF, D_MODEL), const),
                  pl.BlockSpec((1, D_MODEL), const)],
        out_specs=pl.BlockSpec((tm, D_MODEL), row),
        scratch_shapes=[pltpu.VMEM((tm, D_MODEL), F32)],
        compiler_params=_cparams(("parallel", "arbitrary"), VMEM_LIMIT),
        name="moe",
    )(h2, x1, mods_l, gates, wg, wu, wd,
      p['moe_ws_gate'].astype(BF16), p['moe_ws_up'].astype(BF16), p['moe_ws_down'].astype(BF16),
      norm_f.reshape(1, D_MODEL))
    return out


def kernel(x_prompt, x_sample, state_ssm, state_delta, c, c_ctx, w_ada, b_ada, norm1, norm2, w_in, w_out, ssm_lam_re, ssm_lam_im, ssm_log_dt, ssm_b_re, ssm_b_im, ssm_c_re, ssm_c_im, ssm_d, ssm_w_glu, ssm_b_glu, dn_conv, dn_a_log, dn_dt_bias, dn_norm, cm_norm, cm_w_s, cm_b_s, moe_router, moe_router_bias, moe_w_gate, moe_w_up, moe_w_down, moe_ws_gate, moe_ws_up, moe_ws_down, norm_f):
    n_ctx_seq, ctx_seq, _ = x_prompt.shape
    n_lat_seq, lat_seq, _ = x_sample.shape
    depth = w_ada.shape[0]
    n_ctx_tok = n_ctx_seq * ctx_seq
    n_seq = n_ctx_seq + n_lat_seq
    g = SSM_GROUPS

    xs = [x_prompt.reshape(n_ctx_tok, D_MODEL), x_sample.reshape(n_lat_seq * lat_seq, D_MODEL)]
    cvec = jnp.concatenate([c_ctx[None, :], c, jnp.zeros((8 - 1 - n_lat_seq, D_MODEL), F32)], axis=0)
    mods = _ada_mods(cvec, w_ada, b_ada)

    ssm_states, dn_states = [], []
    y = None
    for l in range(depth):
        p = {'w_out': w_out[l], 'norm2': norm2[l], 'ssm_w_glu': ssm_w_glu[l], 'ssm_b_glu': ssm_b_glu[l],
             'dn_conv': dn_conv[l], 'dn_a_log': dn_a_log[l], 'dn_dt_bias': dn_dt_bias[l],
             'dn_norm': dn_norm[l], 'cm_norm': cm_norm[l], 'cm_w_s': cm_w_s[l], 'cm_b_s': cm_b_s[l],
             'moe_ws_gate': moe_ws_gate[l], 'moe_ws_up': moe_ws_up[l], 'moe_ws_down': moe_ws_down[l]}
        mods_l = mods[l].reshape(8, 1, N_MOD * D_MODEL)
        s_u, z, cu, cv, qkvn, gcb, gcb_t = _in_projection(
            xs, mods_l, norm1[l], _cast_rows(w_in, l, 256), dn_conv[l], dn_a_log[l], dn_dt_bias[l],
            n_ctx_tok, ctx_seq, lat_seq)

        mats = _s5_prep(ssm_lam_re[l], ssm_lam_im[l], ssm_log_dt[l], ssm_b_re[l], ssm_b_im[l],
                        ssm_c_re[l], ssm_c_im[l])
        h0_lat = state_ssm[:, l].transpose(0, 3, 2, 1, 4).reshape(n_lat_seq, g, 256)
        h0 = jnp.concatenate([jnp.zeros((n_ctx_seq, g, 256), F32), h0_lat], axis=0)
        ys, ssm_fin = _s5_mixer(s_u, mats, ssm_d[l], h0, n_ctx_seq, ctx_seq, lat_seq)
        ssm_states.append(ssm_fin[:n_ctx_seq].reshape(n_ctx_seq, g, 2, 2, SSM_STATE).transpose(0, 3, 2, 1, 4))

        o_ctx, o_lat, dn_fin = _deltanet(qkvn, gcb, gcb_t, state_delta[:, l], n_ctx_seq, ctx_seq, n_lat_seq,
                                         lat_seq)
        dn_states.append(dn_fin)

        x1, h2 = _mix_out(xs, mods_l, ys, o_ctx, o_lat, z, cu, cv, p, n_ctx_tok, lat_seq)
        gates_t = _router(h2, moe_router[l], moe_router_bias[l])
        x = _moe(h2, x1, mods_l, gates_t, moe_w_gate, moe_w_up, moe_w_down, l, p, norm_f, l == depth - 1,
                 n_ctx_tok, lat_seq)
        xs = [x]

    y_prompt = x[:n_ctx_tok].reshape(x_prompt.shape)
    y_sample = x[n_ctx_tok:].reshape(x_sample.shape)
    new_state_ssm = jnp.stack(ssm_states, axis=1)
    new_state_delta = jnp.stack(dn_states, axis=1)
    return (y_prompt, y_sample, new_state_ssm, new_state_delta)
```

```python
import functools
import math

import jax
import jax.numpy as jnp
from jax import lax
from jax.experimental import pallas as pl
from jax.experimental.pallas import tpu as pltpu

F32 = jnp.float32
BF16 = jnp.bfloat16
HIGHEST = lax.Precision.HIGHEST

D_MODEL = 1024
N_MOD = 6
EPS = 1e-6
SSM_CH = 256
SSM_GROUPS = 16
SSM_GROUP_CH = 16
SSM_STATE = 64
S5_T = 16
S5_W = S5_T * SSM_GROUP_CH
DN_HEADS = 4
DN_HEAD_DIM = 128
DN_WIDTH = 512
DN_CHUNK = 64
CM_HEADS = 4
CM_WIDTH = 256
CM_HEAD_DIM = 64
CM_CHUNK = 128
N_EXPERTS = 64
TOP_K = 6
N_ROUTE_GROUPS = 8
TOPK_ROUTE_GROUPS = 4
EXPERT_FF = 128
SHARED_FF = 256
ROUTED_SCALE = 2.5
MAIN_COLS = SSM_CH + 3 * DN_WIDTH + DN_WIDTH + 2 * CM_WIDTH
VMEM_LIMIT = 56 * 1024 * 1024
VMEM_LIMIT_MAX = 60 * 1024 * 1024


def _cparams(sem, vmem=None):
    return pltpu.CompilerParams(dimension_semantics=sem, vmem_limit_bytes=vmem)


def _dot(a, b):
    return jnp.dot(a, b, preferred_element_type=F32)


def _dot_nt(a, b, precision=None):
    return lax.dot_general(a, b, (((1,), (1,)), ((), ())), preferred_element_type=F32,
                           precision=precision)


def _dot_tn(a, b):
    return lax.dot_general(a, b, (((0,), (0,)), ((), ())), preferred_element_type=F32)


def _sigmoid(x):
    return 1.0 / (1.0 + jnp.exp(-x))


def _silu(x):
    return x * _sigmoid(x)


def _gelu(x):
    c = math.sqrt(2.0 / math.pi)
    return 0.5 * x * (1.0 + jnp.tanh(c * (x + 0.044715 * (x * x * x))))


def _softplus(x):
    return jnp.maximum(x, 0.0) + jnp.log(1.0 + jnp.exp(-jnp.abs(x)))


def _rms(x):
    return x * lax.rsqrt(jnp.mean(x * x, axis=-1, keepdims=True) + EPS)


def _mod_row_index(tile, n_ctx_tiles, tiles_per_lat_seq):
    return jnp.where(tile < n_ctx_tiles, 0, 1 + (tile - n_ctx_tiles) // tiles_per_lat_seq)


def _ada_kernel(c_ref, w_ref, b_ref, o_ref):
    h = _silu(c_ref[...])
    o_ref[0] = jnp.dot(h, w_ref[0], preferred_element_type=F32, precision=HIGHEST) + b_ref[0]


def _ada_mods(cvec, w_ada, b_ada):
    depth, d, n = w_ada.shape
    bn = 1536
    return pl.pallas_call(
        _ada_kernel,
        out_shape=jax.ShapeDtypeStruct((depth, 8, n), F32),
        grid=(depth, n // bn),
        in_specs=[pl.BlockSpec((8, d), lambda l, j: (0, 0)),
                  pl.BlockSpec((1, d, bn), lambda l, j: (l, 0, j)),
                  pl.BlockSpec((1, 1, bn), lambda l, j: (l, 0, j))],
        out_specs=pl.BlockSpec((1, 8, bn), lambda l, j: (l, 0, j)),
        compiler_params=_cparams(("parallel", "parallel")),
        name="ada_mods",
    )(cvec, w_ada, b_ada.reshape(depth, 1, n))


def _token_specs(xs, tm, n_ctx_tiles, width):
    if len(xs) == 1:
        return [pl.BlockSpec((tm, width), lambda i, *_: (i, 0))]
    return [pl.BlockSpec((tm, width), lambda i, *_: (jnp.minimum(i, n_ctx_tiles - 1), 0)),
            pl.BlockSpec((tm, width), lambda i, *_: (jnp.maximum(i - n_ctx_tiles, 0), 0))]


def _token_tile(refs, n_ctx_tiles):
    if len(refs) == 1:
        return refs[0][...]
    return jnp.where(pl.program_id(0) < n_ctx_tiles, refs[0][...], refs[1][...])


IN_TILE = 512


def _inproj_kernel(*refs, n_x, n_ctx_tiles, tiles_per_lat_seq, ctx_seq):
    x_refs = refs[:n_x]
    (prev_ref, next_ref, mod_ref, g_ref, w_ref, cw_ref, alog_ref, dtb_ref,
     su_ref, z_ref, cu_ref, cv_ref, qkvn_ref, gcb_ref, gt_ref, scr) = refs[n_x:]
    mod = mod_ref[0]
    shift = mod[:, 0:D_MODEL]
    scale = mod[:, D_MODEL:2 * D_MODEL]
    gain = g_ref[...] * (1.0 + scale)

    def normed(x):
        return (_rms(x) * gain + shift).astype(BF16)

    c0 = SSM_CH
    c1 = c0 + 3 * DN_WIDTH
    c2 = c1 + DN_WIDTH
    n_ab = 4 * DN_HEADS
    hb = normed(_token_tile(x_refs, n_ctx_tiles))
    x = _dot(hb, w_ref[:, c0:c1])
    tail = _dot(hb, w_ref[:, c2:])
    su = _dot(hb, w_ref[:, 0:c0])
    z_ref[...] = _dot(hb, w_ref[:, c1:c2])
    cu_ref[...] = tail[:, n_ab:n_ab + CM_WIDTH]
    cv_ref[...] = tail[:, n_ab + CM_WIDTH:n_ab + 2 * CM_WIDTH]
    scr[0] = su[:, 0:128]
    scr[1] = su[:, 128:c0]
    n_rows = su_ref.shape[1]
    for half in range(2):
        for t in range(S5_T):
            su_ref[half, :, t * 128:(t + 1) * 128] = scr[half, pl.ds(t, n_rows, stride=S5_T), :]

    n = x.shape[0]
    i = pl.program_id(0)
    is_ctx = i < n_ctx_tiles
    lat_pos = (i - n_ctx_tiles) % tiles_per_lat_seq
    has_prev = jnp.logical_not(is_ctx) & (lat_pos > 0)
    has_next = jnp.logical_not(is_ctx) & (lat_pos < tiles_per_lat_seq - 1)
    w_qkv = w_ref[:, c0:c1]
    prev_row = jnp.where(has_prev, _dot(normed(prev_ref[...]), w_qkv)[7:8, :], 0.0)
    next_row = jnp.where(has_next, _dot(normed(next_ref[...]), w_qkv)[0:1, :], 0.0)
    row = lax.broadcasted_iota(jnp.int32, (n, 1), 0)
    seq_first = is_ctx & (row % ctx_seq == 0)
    seq_last = is_ctx & (row % ctx_seq == ctx_seq - 1)
    xp = jnp.where(row == 0, prev_row, pltpu.roll(x, 1, 0))
    xn = jnp.where(row == n - 1, next_row, pltpu.roll(x, n - 1, 0))
    xp = jnp.where(seq_first, 0.0, xp)
    xn = jnp.where(seq_last, 0.0, xn)
    cw = cw_ref[...]
    y = _silu(cw[0:1] * xp + cw[1:2] * x + cw[2:3] * xn)
    for hcol in range(2 * DN_HEADS):
        seg = y[:, hcol * DN_HEAD_DIM:(hcol + 1) * DN_HEAD_DIM]
        nrm = seg * lax.rsqrt(jnp.sum(seg * seg, axis=-1, keepdims=True) + EPS)
        if hcol < DN_HEADS:
            nrm = nrm * (DN_HEAD_DIM ** -0.5)
        qkvn_ref[:, hcol * DN_HEAD_DIM:(hcol + 1) * DN_HEAD_DIM] = nrm.astype(BF16)
    qkvn_ref[:, 2 * DN_WIDTH:] = y[:, 2 * DN_WIDTH:].astype(BF16)

    ab = tail[:, 0:128]
    lane = lax.broadcasted_iota(jnp.int32, (1, 128), 1)
    glog = jnp.where(lane < 2 * DN_HEADS, -jnp.exp(alog_ref[...]) * _softplus(ab + dtb_ref[...]), 0.0)
    beta = _sigmoid(ab)
    n_ch = n // DN_CHUNK
    wide = jnp.concatenate([glog[c * DN_CHUNK:(c + 1) * DN_CHUNK] for c in range(n_ch)], axis=1)
    ii = lax.broadcasted_iota(jnp.int32, (DN_CHUNK, DN_CHUNK), 0)
    jj = lax.broadcasted_iota(jnp.int32, (DN_CHUNK, DN_CHUNK), 1)
    pre = jnp.where(jj <= ii, 1.0, 0.0).astype(F32)
    suf = jnp.where(jj >= ii, 1.0, 0.0).astype(F32)
    gpre_w = jnp.dot(pre, wide, preferred_element_type=F32, precision=HIGHEST)
    gsuf_w = jnp.dot(suf, wide, preferred_element_type=F32, precision=HIGHEST)
    unwide = lambda a: jnp.concatenate([a[:, c * 128:(c + 1) * 128] for c in range(n_ch)], axis=0)
    gcb = jnp.where(lane < DN_HEADS, unwide(gpre_w), jnp.where(lane < 2 * DN_HEADS, unwide(gsuf_w), beta))
    gcb_ref[...] = gcb
    gcb_t = gcb.T
    for c in range(n_ch):
        gt_ref[c] = gcb_t[0:16, c * DN_CHUNK:(c + 1) * DN_CHUNK]


def _in_projection(xs, mods_l, norm1, w_in_bf, conv_w, a_log, dt_bias, n_ctx_tok, ctx_seq, lat_seq):
    n_tok = sum(x.shape[0] for x in xs)
    tm = IN_TILE
    assert tm % ctx_seq == 0 and lat_seq % tm == 0
    n_ctx_tiles = n_ctx_tok // tm
    tps = lat_seq // tm
    row = lambda i: (i, 0)
    const = lambda i: (0, 0)
    in_cols = w_in_bf.shape[1]
    r8 = tm // 8
    halo_src = xs[-1]
    first = n_ctx_tiles if len(xs) == 2 else 0
    last8 = halo_src.shape[0] // 8 - 1
    prev_spec = pl.BlockSpec((8, D_MODEL), lambda i: (jnp.clip((i - first) * r8 - 1, 0, last8), 0))
    next_spec = pl.BlockSpec((8, D_MODEL), lambda i: (jnp.clip((i - first + 1) * r8, 0, last8), 0))
    w8 = jnp.concatenate([conv_w, jnp.zeros((5, 3 * DN_WIDTH), F32)], axis=0)
    pad = lambda v: jnp.concatenate([v.reshape(1, 2 * DN_HEADS), jnp.zeros((1, 120), F32)], axis=1)
    outs = [(DN_WIDTH, F32), (CM_WIDTH, F32), (CM_WIDTH, F32), (3 * DN_WIDTH, BF16), (128, F32)]
    return pl.pallas_call(
        functools.partial(_inproj_kernel, n_x=len(xs), n_ctx_tiles=n_ctx_tiles, tiles_per_lat_seq=tps,
                          ctx_seq=ctx_seq),
        out_shape=[jax.ShapeDtypeStruct((2, n_tok // S5_T, S5_T * 128), F32)]
        + [jax.ShapeDtypeStruct((n_tok, w), dt) for w, dt in outs]
        + [jax.ShapeDtypeStruct((n_tok // DN_CHUNK, 16, DN_CHUNK), F32)],
        grid=(n_tok // tm,),
        in_specs=_token_specs(xs, tm, n_ctx_tiles, D_MODEL)
        + [prev_spec, next_spec,
           pl.BlockSpec((1, 1, N_MOD * D_MODEL),
                        lambda i: (_mod_row_index(i, n_ctx_tiles, tps), 0, 0)),
           pl.BlockSpec((1, D_MODEL), const),
           pl.BlockSpec((D_MODEL, in_cols), const),
           pl.BlockSpec((8, 3 * DN_WIDTH), const),
           pl.BlockSpec((1, 128), const),
           pl.BlockSpec((1, 128), const)],
        out_specs=[pl.BlockSpec((2, tm // S5_T, S5_T * 128), lambda i: (0, i, 0))]
        + [pl.BlockSpec((tm, w), row) for w, _ in outs]
        + [pl.BlockSpec((tm // DN_CHUNK, 16, DN_CHUNK), lambda i: (i, 0, 0))],
        scratch_shapes=[pltpu.VMEM((2, tm, 128), F32)],
        compiler_params=_cparams(("parallel",), VMEM_LIMIT),
        name="in_projection",
    )(*xs, halo_src, halo_src, mods_l, norm1.reshape(1, D_MODEL), w_in_bf, w8, pad(a_log), pad(dt_bias))


def _s5_prep_kernel(lre_ref, lim_ref, ldt_ref, btre_ref, btim_ref, cre_ref, cim_ref,
                    mt_ref, pt_ref, qt_ref, at_ref):
    t_n, hg, w = S5_T, SSM_GROUP_CH, S5_W
    lre, lim = lre_ref[0], lim_ref[0]
    dt = jnp.exp(ldt_ref[0])
    xr, xi = lre * dt, lim * dt
    tau = lax.broadcasted_iota(jnp.int32, (t_n, 128), 0).astype(F32)
    mag = jnp.exp(tau * xr)
    a_re, a_im = mag * jnp.cos(tau * xi), mag * jnp.sin(tau * xi)
    ar, ai = jnp.exp(xr) * jnp.cos(xi), jnp.exp(xr) * jnp.sin(xi)
    inv = 1.0 / (lre * lre + lim * lim)
    br_, bi_ = lre * inv, -lim * inv
    fr = (ar - 1.0) * br_ - ai * bi_
    fi = (ar - 1.0) * bi_ + ai * br_
    btre, btim = btre_ref[0], btim_ref[0]
    bbr = fr * btre - fi * btim
    bbi = fr * btim + fi * btre
    cre, cim = cre_ref[0], cim_ref[0]

    lane = lax.broadcasted_iota(jnp.int32, (1, 128), 1)
    is_f = lane < SSM_STATE

    def rep_rows(a):
        return jnp.concatenate([jnp.broadcast_to(a[t:t + 1], (hg, 128)) for t in range(t_n)], axis=0)

    def tile_rows(c):
        return jnp.concatenate([c] * t_n, axis=0)

    def rev_rows(a):
        return jnp.concatenate([a[t_n - 1 - t:t_n - t] for t in range(t_n)], axis=0)

    cr_t, ci_t = tile_rows(cre), tile_rows(cim)
    a_re_rep, a_im_rep = rep_rows(a_re), rep_rows(a_im)
    w_re = cr_t * a_re_rep - ci_t * a_im_rep
    w_im = cr_t * a_im_rep + ci_t * a_re_rep
    zero = jnp.zeros_like(w_re)
    e_f = jnp.concatenate([jnp.where(is_f, w_re, zero), jnp.where(is_f, -w_im, zero)], axis=1)
    e_b = jnp.concatenate([jnp.where(is_f, zero, w_re), jnp.where(is_f, zero, -w_im)], axis=1)
    e_b_rev = jnp.concatenate([e_b[(t_n - 1 - t) * hg:(t_n - t) * hg] for t in range(t_n)], axis=0)
    bb_t = tile_rows(jnp.concatenate([bbr, bbi], axis=1))
    row_blk = lax.broadcasted_iota(jnp.int32, (w, w), 0) // hg

    m = jnp.zeros((w, w), F32)
    zrows = lambda n: jnp.zeros((n * hg, w), F32)
    for s in range(t_n):
        down = e_f if s == 0 else jnp.concatenate([zrows(s), e_f[:(t_n - s) * hg]], axis=0)
        k = t_n - 1 - s
        up = e_b_rev if k == 0 else jnp.concatenate([e_b_rev[k * hg:], zrows(k)], axis=0)
        rhs = jnp.where(row_blk == s, bb_t, 0.0)
        m = m + _dot_nt((down + up).astype(BF16), rhs.astype(BF16))
    mt_ref[...] = m.T.reshape(t_n, hg, w).astype(BF16)

    a_re_rev, a_im_rev = rev_rows(a_re), rev_rows(a_im)
    ps_re = rep_rows(jnp.where(is_f, a_re_rev, a_re))
    ps_im = rep_rows(jnp.where(is_f, a_im_rev, a_im))
    bbr_t, bbi_t = tile_rows(bbr), tile_rows(bbi)
    p_re = ps_re * bbr_t - ps_im * bbi_t
    p_im = ps_re * bbi_t + ps_im * bbr_t
    pt_ref[...] = jnp.concatenate([p_re, p_im], axis=1).reshape(t_n, hg, w).astype(BF16)

    a1_re = a_re * ar - a_im * ai
    a1_im = a_re * ai + a_im * ar
    aq_re = rep_rows(jnp.where(is_f, a1_re, rev_rows(a1_re)))
    aq_im = rep_rows(jnp.where(is_f, a1_im, rev_rows(a1_im)))
    q_re = cr_t * aq_re - ci_t * aq_im
    q_im = cr_t * aq_im + ci_t * aq_re
    qt_ref[0] = jnp.concatenate([q_re, -q_im], axis=1).T.astype(BF16)
    at_ref[0] = jnp.concatenate([a1_re[t_n - 1:t_n], a1_im[t_n - 1:t_n]], axis=1)


def _s5_prep(lam_re, lam_im, log_dt, b_re, b_im, c_re, c_im):
    g = SSM_GROUPS
    cat = lambda a: jnp.concatenate([a[0], a[1]], axis=-1)
    lre = cat(lam_re)[:, None, :]
    lim = cat(lam_im)[:, None, :]
    ldt = cat(jnp.broadcast_to(log_dt[..., None], (2, g, SSM_STATE)))[:, None, :]
    btre = cat(jnp.swapaxes(b_re, -1, -2))
    btim = cat(jnp.swapaxes(b_im, -1, -2))
    cre, cim = cat(c_re), cat(c_im)
    vec = pl.BlockSpec((1, 1, 128), lambda i: (i, 0, 0))
    mat = pl.BlockSpec((1, SSM_GROUP_CH, 128), lambda i: (i, 0, 0))
    big = pl.BlockSpec((1, S5_W, S5_W), lambda i: (i, 0, 0))
    gh = g // 2
    arranged = pl.BlockSpec((None, S5_T, None, SSM_GROUP_CH, S5_W), lambda i: (i // gh, 0, i % gh, 0, 0))
    arranged_shape = jax.ShapeDtypeStruct((2, S5_T, gh, SSM_GROUP_CH, S5_W), BF16)
    return pl.pallas_call(
        _s5_prep_kernel,
        out_shape=[arranged_shape, arranged_shape, jax.ShapeDtypeStruct((g, S5_W, S5_W), BF16),
                   jax.ShapeDtypeStruct((g, 1, 256), F32)],
        grid=(g,),
        in_specs=[vec, vec, vec, mat, mat, mat, mat],
        out_specs=[arranged, arranged, big, pl.BlockSpec((1, 1, 256), lambda i: (i, 0, 0))],
        compiler_params=_cparams(("parallel",)),
        name="s5_prep",
    )(lre, lim, ldt, btre, btim, cre, cim)


S5_SPREAD_ROWS = 512


def _s5_spread_kernel(a_ref, o_ref, *, rows_per_group, gh):
    hg, w = SSM_GROUP_CH, S5_W
    n_rows, n_cols = o_ref.shape
    src = lax.broadcasted_iota(jnp.int32, (w, n_cols), 0)
    dst = lax.broadcasted_iota(jnp.int32, (w, n_cols), 1)
    tile = jnp.where((src // hg == dst // 128) & (src % hg == dst % hg), 1.0, 0.0).astype(BF16)
    row = pl.program_id(1) * n_rows + lax.broadcasted_iota(jnp.int32, (n_rows, n_cols), 0)
    col = lax.broadcasted_iota(jnp.int32, (n_rows, n_cols), 1)
    keep = (row // rows_per_group) % gh == (col // hg) % gh
    o_ref[...] = jnp.where(keep, _dot(a_ref[...], tile), 0.0).astype(BF16)


def _s5_spread_cols(a, rows_per_group, gh):
    _, n_rows, w = a.shape
    rb = S5_SPREAD_ROWS
    return pl.pallas_call(
        functools.partial(_s5_spread_kernel, rows_per_group=rows_per_group, gh=gh),
        out_shape=jax.ShapeDtypeStruct((2, n_rows, S5_T * 128), BF16),
        grid=(2, n_rows // rb),
        in_specs=[pl.BlockSpec((None, rb, w), lambda z, i: (z, i, 0))],
        out_specs=pl.BlockSpec((None, rb, S5_T * 128), lambda z, i: (z, i, 0)),
        compiler_params=_cparams(("parallel", "parallel")),
        name="s5_spread_cols",
    )(a)


def _s5_place_kernel(a_ref, o_ref, *, gh):
    a = a_ref[...]
    n_rows, w = a.shape
    grp = ((pl.program_id(1) * n_rows + lax.broadcasted_iota(jnp.int32, (n_rows, w), 0))
           // SSM_GROUP_CH) % gh
    for k in range(gh):
        o_ref[:, k * w:(k + 1) * w] = jnp.where(grp == k, a, jnp.zeros_like(a))


def _s5_place_cols(a, gh):
    _, n_rows, w = a.shape
    rb = S5_SPREAD_ROWS
    return pl.pallas_call(
        functools.partial(_s5_place_kernel, gh=gh),
        out_shape=jax.ShapeDtypeStruct((2, n_rows, gh * w), BF16),
        grid=(2, n_rows // rb),
        in_specs=[pl.BlockSpec((None, rb, w), lambda z, i: (z, i, 0))],
        out_specs=pl.BlockSpec((None, rb, gh * w), lambda z, i: (z, i, 0)),
        compiler_params=_cparams(("parallel", "parallel")),
        name="s5_place_cols",
    )(a)


def _s5_z_kernel(u_ref, pt_ref, z_ref):
    z_ref[0] = _dot(u_ref[0].astype(BF16), pt_ref[0])


def _s5_rec_kernel(z_ref, a_ref, h0_ref, hf_ref, hb_ref, fin_ref, *, n_ctx_seq, ctx_chunks, n_lat_seq,
                   lat_chunks):
    sw = z_ref.shape[-1]
    ng = sw // 256
    lane = lax.broadcasted_iota(jnp.int32, (1, sw), 1)
    is_f = (lane % 128) < SSM_STATE

    def split(row):
        re = jnp.concatenate([row[:, g * 256:g * 256 + 128] for g in range(ng)], axis=1)
        im = jnp.concatenate([row[:, g * 256 + 128:(g + 1) * 256] for g in range(ng)], axis=1)
        return re, im

    def merge(re, im):
        parts = []
        for g in range(ng):
            parts += [re[:, g * 128:(g + 1) * 128], im[:, g * 128:(g + 1) * 128]]
        return jnp.concatenate(parts, axis=1)

    a_re, a_im = split(a_ref[0])

    def run_seq(base, n, seq):
        def step(k, h):
            h_re, h_im = h
            cf = base + k
            cb = base + n - 1 - k
            row = merge(h_re, h_im)
            hf_ref[0, pl.ds(cf, 1), :] = row
            hb_ref[0, pl.ds(cb, 1), :] = row
            z_re, z_im = split(jnp.where(is_f, z_ref[0, pl.ds(cf, 1), :], z_ref[0, pl.ds(cb, 1), :]))
            return (a_re * h_re - a_im * h_im + z_re, a_re * h_im + a_im * h_re + z_im)
        h_re, h_im = lax.fori_loop(0, n, step, split(h0_ref[0, pl.ds(seq, 1), :]))
        fin_ref[0, pl.ds(seq, 1), :] = merge(h_re, h_im)

    def ctx_body(s, carry):
        run_seq(s * ctx_chunks, ctx_chunks, s)
        return carry
    lax.fori_loop(0, n_ctx_seq, ctx_body, 0)

    def lat_body(s, carry):
        run_seq(n_ctx_seq * ctx_chunks + s * lat_chunks, lat_chunks, n_ctx_seq + s)
        return carry
    lax.fori_loop(0, n_lat_seq, lat_body, 0)


def _s5_y_kernel(u_ref, hf_ref, hb_ref, mt_ref, qt_ref, d_ref, y_ref):
    u = u_ref[0]
    lane = lax.broadcasted_iota(jnp.int32, (1, u.shape[-1]), 1)
    is_f = (lane % 128) < SSM_STATE
    h = jnp.where(is_f, hf_ref[0], hb_ref[0]).astype(BF16)
    y_ref[0] = _dot(u.astype(BF16), mt_ref[0]) + _dot(h, qt_ref[0]) + d_ref[0] * u


def _s5_mixer(s_u, mats, ssm_d, h0, n_ctx_seq, ctx_seq, lat_seq):
    mt, pt, qt, at = mats
    g, t_n, w, hg = SSM_GROUPS, S5_T, S5_W, SSM_GROUP_CH
    gh = g // 2
    nc = s_u.shape[1]
    bw = t_n * 128
    sw = gh * w
    n_seq = h0.shape[0]
    ctx_chunks, lat_chunks = ctx_seq // t_n, lat_seq // t_n
    n_lat_seq = n_seq - n_ctx_seq
    u = s_u
    m_big = _s5_spread_cols(mt.reshape(2, bw, w), hg, gh)
    q_big = _s5_spread_cols(qt.reshape(2, sw, w), w, gh)
    p_big = _s5_place_cols(pt.reshape(2, bw, w), gh)
    d_big = jnp.tile(ssm_d.reshape(2, 1, 128), (1, 1, t_n))

    rb = math.gcd(nc, 256)
    u_spec = pl.BlockSpec((1, rb, bw), lambda z, i: (z, i, 0))
    s_spec = pl.BlockSpec((1, rb, sw), lambda z, i: (z, i, 0))
    z = pl.pallas_call(
        _s5_z_kernel,
        out_shape=jax.ShapeDtypeStruct((2, nc, sw), F32),
        grid=(2, nc // rb),
        in_specs=[u_spec, pl.BlockSpec((1, bw, sw), lambda z, i: (z, 0, 0))],
        out_specs=s_spec,
        compiler_params=_cparams(("parallel", "parallel"), VMEM_LIMIT),
        name="s5_chunk_inputs",
    )(u, p_big)

    half = lambda rows: pl.BlockSpec((1, rows, sw), lambda z: (z, 0, 0))
    hf, hb, fin = pl.pallas_call(
        functools.partial(_s5_rec_kernel, n_ctx_seq=n_ctx_seq, ctx_chunks=ctx_chunks,
                          n_lat_seq=n_lat_seq, lat_chunks=lat_chunks),
        out_shape=[jax.ShapeDtypeStruct((2, nc, sw), F32)] * 2 + [jax.ShapeDtypeStruct((2, n_seq, sw), F32)],
        grid=(2,),
        in_specs=[half(nc), half(1), half(n_seq)],
        out_specs=[half(nc), half(nc), half(n_seq)],
        compiler_params=_cparams(("parallel",), VMEM_LIMIT),
        name="s5_recurrence",
    )(z, at.reshape(2, 1, sw), h0.reshape(n_seq, 2, sw).transpose(1, 0, 2))

    y = pl.pallas_call(
        _s5_y_kernel,
        out_shape=jax.ShapeDtypeStruct((2, nc, bw), F32),
        grid=(2, nc // rb),
        in_specs=[u_spec, s_spec, s_spec,
                  pl.BlockSpec((1, bw, bw), lambda z, i: (z, 0, 0)),
                  pl.BlockSpec((1, sw, bw), lambda z, i: (z, 0, 0)),
                  pl.BlockSpec((1, 1, bw), lambda z, i: (z, 0, 0))],
        out_specs=u_spec,
        compiler_params=_cparams(("parallel", "parallel"), VMEM_LIMIT),
        name="s5_outputs",
    )(u, hf, hb, m_big, q_big, d_big)
    return y, fin.transpose(1, 0, 2).reshape(n_seq, g, w)


DN_CHAINS = 2 * DN_HEADS


def _split(x):
    hi = x.astype(BF16)
    lo = (x - hi.astype(F32)).astype(BF16)
    return hi, lo


def _split_lhs(a):
    hi, lo = _split(a)
    hif = hi.astype(F32)
    return jnp.concatenate([hif, hif, lo.astype(F32)], axis=1).astype(BF16)


def _split_rhs(b):
    hi, lo = _split(b)
    return jnp.concatenate([hi, lo, hi], axis=0)


def _dn_prepare(q_ref, k_ref, v_ref, gc_ref, gt_ref, row0, chunk_idx):
    c = DN_CHUNK
    ii = lax.broadcasted_iota(jnp.int32, (c, c), 0)
    jj = lax.broadcasted_iota(jnp.int32, (c, c), 1)
    items = []
    for d in range(2):
        rows = pl.ds(row0[d], c)
        gcol = gc_ref[rows, :]
        grow = gt_ref[chunk_idx[d]]
        incl = (ii >= jj) if d == 0 else (ii <= jj)
        strict = (ii > jj) if d == 0 else (ii < jj)
        for h in range(DN_HEADS):
            ch = d * DN_HEADS + h
            cols = slice(h * DN_HEAD_DIM, (h + 1) * DN_HEAD_DIM)
            q, k, v = q_ref[rows, cols], k_ref[rows, cols], v_ref[rows, cols]
            gc_col = gcol[:, ch:ch + 1]
            beta = gcol[:, DN_CHAINS + ch:DN_CHAINS + ch + 1]
            gc_row = grow[ch:ch + 1, :]
            kf = k.astype(F32)
            kb = kf * beta
            decay = jnp.exp(jnp.where(incl, gc_col - gc_row, -jnp.inf))
            egc = jnp.exp(gc_col)
            g_last = gc_col[c - 1:c] if d == 0 else gc_col[0:1]
            items.append(dict(
                q=q, k=k, kb=kb.astype(BF16), decay=decay, incl=incl, strict=strict,
                rhs=jnp.concatenate([kb * egc, v.astype(F32) * beta], axis=1),
                qg=(q.astype(F32) * egc).astype(BF16),
                kt=(kf * jnp.exp(g_last - gc_col)).astype(BF16),
                gl=jnp.broadcast_to(jnp.exp(g_last), (8, 128))))
    for it in items:
        it['a'] = jnp.where(it['strict'], _dot_nt(it['kb'], it['k']) * it['decay'], 0.0)
        it['qk'] = jnp.where(it['incl'], _dot_nt(it['q'], it['k']) * it['decay'], 0.0).astype(BF16)
    power = 1
    while power < c:
        for it in items:
            it['lhs'] = _split_lhs(it['a'])
        for it in items:
            upd = _dot(it['lhs'], _split_rhs(it['rhs'] if power == 1 else it['r']))
            it['r'] = it['rhs'] - upd if power == 1 else it['r'] + upd
        if 2 * power < c:
            for it in items:
                it['a'] = _dot(it['lhs'], _split_rhs(it['a']))
        power *= 2
    return items


def _dn_scan_kernel(*refs, n_chunks, seq_len, seqs_per_blk, has_s0, write_fin):
    refs = list(refs)
    q_ref, k_ref, v_ref, gc_ref, gt_ref = refs[:5]
    pos = 5
    s0_ref = None
    if has_s0:
        s0_ref = refs[pos]
        pos += 1
    o_ref = refs[pos]
    pos += 1
    fin_ref = None
    if write_fin:
        fin_ref = refs[pos]
        pos += 1
    s_ref, w_s, u_s, qk_s, qg_s, kt_s, gl_s = refs[pos:pos + 7]
    n_steps = seqs_per_blk * n_chunks

    o_ref[...] = jnp.zeros_like(o_ref)
    for r in (s_ref, w_s, u_s, qk_s, qg_s, kt_s, gl_s):
        r[...] = jnp.zeros_like(r)

    def locate(step):
        seq = step // n_chunks
        kk = step % n_chunks
        cf = seq * n_chunks + kk
        cb = seq * n_chunks + (n_chunks - 1 - kk)
        rows = [pl.multiple_of(cf * DN_CHUNK, DN_CHUNK), pl.multiple_of(cb * DN_CHUNK, DN_CHUNK)]
        return seq, kk, rows, [cf, cb]

    def body(t, carry):
        tb = jnp.maximum(t, 0)
        valid = t >= 0
        seq, kk, rows_b, _ = locate(tb)
        first = kk == 0
        states, vnews = [], []
        for ch in range(DN_CHAINS):
            d, h = divmod(ch, DN_HEADS)
            s0 = s0_ref[0, d, h] if has_s0 else jnp.zeros((DN_HEAD_DIM, DN_HEAD_DIM), F32)
            states.append(jnp.where(first, s0, s_ref[ch]))
        for ch in range(DN_CHAINS):
            vnews.append((u_s[ch] - _dot(w_s[ch], states[ch].astype(BF16))).astype(BF16))
        tn = jnp.minimum(t + 1, n_steps - 1)
        _, _, rows_a, chunks_a = locate(tn)
        items = _dn_prepare(q_ref, k_ref, v_ref, gc_ref, gt_ref, rows_a, chunks_a)
        for ch in range(DN_CHAINS):
            d, h = divmod(ch, DN_HEADS)
            sb = states[ch].astype(BF16)
            o = _dot(qg_s[ch], sb) + _dot(qk_s[ch], vnews[ch])
            s_new = states[ch] * gl_s[ch][0:1, 0:1] + _dot_tn(kt_s[ch], vnews[ch])
            s_ref[ch] = s_new
            if write_fin:
                fin_ref[seq, d, h] = s_new
            rows = pl.ds(rows_b[d], DN_CHUNK)
            cols = slice(h * DN_HEAD_DIM, (h + 1) * DN_HEAD_DIM)
            o_ref[rows, cols] = o_ref[rows, cols] + jnp.where(valid, o, 0.0)
        for ch, it in enumerate(items):
            w_s[ch] = it['r'][:, :DN_HEAD_DIM].astype(BF16)
            u_s[ch] = it['r'][:, DN_HEAD_DIM:]
            qk_s[ch] = it['qk']
            qg_s[ch] = it['qg']
            kt_s[ch] = it['kt']
            gl_s[ch] = it['gl']
        return carry

    lax.fori_loop(-1, n_steps, body, 0)


def _dn_scan(qkvn, gcb, gcb_t, seq_len, n_seq, seqs_per_blk, first_tok, s0, write_fin):
    n_chunks = seq_len // DN_CHUNK
    rows_blk = seqs_per_blk * seq_len
    b0 = first_tok // rows_blk
    hd = DN_HEAD_DIM
    st_spec = pl.BlockSpec((seqs_per_blk, 2, DN_HEADS, hd, hd), lambda s: (s, 0, 0, 0, 0))
    in_specs = [pl.BlockSpec((rows_blk, DN_WIDTH), lambda s: (b0 + s, 0)),
                pl.BlockSpec((rows_blk, DN_WIDTH), lambda s: (b0 + s, 1)),
                pl.BlockSpec((rows_blk, DN_WIDTH), lambda s: (b0 + s, 2)),
                pl.BlockSpec((rows_blk, 128), lambda s: (b0 + s, 0)),
                pl.BlockSpec((rows_blk // DN_CHUNK, 16, DN_CHUNK), lambda s: (b0 + s, 0, 0))]
    args = [qkvn, qkvn, qkvn, gcb, gcb_t]
    if s0 is not None:
        assert seqs_per_blk == 1
        in_specs.append(st_spec)
        args.append(s0)
    out_shape = [jax.ShapeDtypeStruct((n_seq * seq_len, DN_WIDTH), F32)]
    out_specs = [pl.BlockSpec((rows_blk, DN_WIDTH), lambda s: (s, 0))]
    if write_fin:
        out_shape.append(jax.ShapeDtypeStruct((n_seq, 2, DN_HEADS, hd, hd), F32))
        out_specs.append(st_spec)
    c = DN_CHUNK
    scratch = [pltpu.VMEM((DN_CHAINS, hd, hd), F32),
               pltpu.VMEM((DN_CHAINS, c, hd), BF16),
               pltpu.VMEM((DN_CHAINS, c, hd), F32),
               pltpu.VMEM((DN_CHAINS, c, c), BF16),
               pltpu.VMEM((DN_CHAINS, c, hd), BF16),
               pltpu.VMEM((DN_CHAINS, c, hd), BF16),
               pltpu.VMEM((DN_CHAINS, 8, 128), F32)]
    return pl.pallas_call(
        functools.partial(_dn_scan_kernel, n_chunks=n_chunks, seq_len=seq_len, seqs_per_blk=seqs_per_blk,
                          has_s0=s0 is not None, write_fin=write_fin),
        out_shape=out_shape,
        grid=(n_seq // seqs_per_blk,),
        in_specs=in_specs,
        out_specs=out_specs,
        scratch_shapes=scratch,
        compiler_params=_cparams(("parallel",), VMEM_LIMIT),
        name="dn_scan_%d" % seq_len,
    )(*args)


def _deltanet(qkvn, gcb, gcb_t, s0_lat, n_ctx_seq, ctx_seq, n_lat_seq, lat_seq):
    n_ctx_tok = n_ctx_seq * ctx_seq
    ctx_blk = math.gcd(n_ctx_seq, 8)
    o_ctx, fin = _dn_scan(qkvn, gcb, gcb_t, ctx_seq, n_ctx_seq, ctx_blk, 0, None, True)
    (o_lat,) = _dn_scan(qkvn, gcb, gcb_t, lat_seq, n_lat_seq, 1, n_ctx_tok, s0_lat, False)
    return o_ctx, o_lat, fin


MIX_TILE = 256


def _mix_kernel(*refs, n_x, n_ctx_tiles):
    x_refs = refs[:n_x]
    (mod_ref, ys_ref, octx_ref, olat_ref, z_ref, cu_ref, cv_ref,
     wglu_ref, bglu_ref, dnn_ref, cmn_ref, ws_ref, bs_ref, wo_ref, n2_ref,
     x1_ref, h2_ref, scr) = refs[n_x:]
    mod = mod_ref[0]
    gate1 = mod[:, 2 * D_MODEL:3 * D_MODEL]
    shift2 = mod[:, 3 * D_MODEL:4 * D_MODEL]
    scale2 = mod[:, 4 * D_MODEL:5 * D_MODEL]
    n_rows = ys_ref.shape[1]
    for half in range(2):
        for t in range(S5_T):
            scr[half, pl.ds(t, n_rows, stride=S5_T), :] = ys_ref[half, :, t * 128:(t + 1) * 128]
    ya = _gelu(jnp.concatenate([scr[0], scr[1]], axis=1))
    ya = ya * _sigmoid(_dot(ya.astype(BF16), wglu_ref[...]) + bglu_ref[...])
    mix = _dot(ya.astype(BF16), wo_ref[0:SSM_CH, :])
    o = jnp.where(pl.program_id(0) < n_ctx_tiles, octx_ref[...], olat_ref[...])
    z = z_ref[...]
    parts = []
    for h in range(DN_HEADS):
        sl = slice(h * DN_HEAD_DIM, (h + 1) * DN_HEAD_DIM)
        parts.append(_rms(o[:, sl]) * dnn_ref[...] * _silu(z[:, sl]))
    yb = jnp.concatenate(parts, axis=1)
    mix = mix + _dot(yb.astype(BF16), wo_ref[SSM_CH:SSM_CH + DN_WIDTH, :])
    u = _gelu(cu_ref[...])
    v = _gelu(cv_ref[...])
    seg_i = lax.broadcasted_iota(jnp.int32, (CM_WIDTH, CM_WIDTH), 0) // CM_HEAD_DIM
    seg_j = lax.broadcasted_iota(jnp.int32, (CM_WIDTH, CM_WIDTH), 1) // CM_HEAD_DIM
    seg_ones = jnp.where(seg_i == seg_j, 1.0, 0.0).astype(BF16)
    vv = v * v
    vv_hi = vv.astype(BF16)
    vv_lo = (vv - vv_hi.astype(F32)).astype(BF16)
    ms = (_dot(vv_hi, seg_ones) + _dot(vv_lo, seg_ones)) * (1.0 / CM_HEAD_DIM)
    vn = (v * lax.rsqrt(ms + EPS) * cmn_ref[...]).astype(BF16)
    head_of_lane = lax.broadcasted_iota(jnp.int32, (1, CM_WIDTH), 1) // CM_HEAD_DIM
    ycs = []
    for c in range(MIX_TILE // CM_CHUNK):
        vc = vn[c * CM_CHUNK:(c + 1) * CM_CHUNK]
        stack = jnp.concatenate([jnp.where(head_of_lane == h, vc, jnp.zeros_like(vc))
                                 for h in range(CM_HEADS)], axis=0)
        vs = _dot(ws_ref[...], stack) + bs_ref[...]
        ycs.append(u[c * CM_CHUNK:(c + 1) * CM_CHUNK] * vs)
    yc = jnp.concatenate(ycs, axis=0)
    mix = mix + _dot(yc.astype(BF16), wo_ref[SSM_CH + DN_WIDTH:, :])
    x1 = _token_tile(x_refs, n_ctx_tiles) + gate1 * mix
    x1_ref[...] = x1
    h2_ref[...] = (_rms(x1) * n2_ref[...] * (1.0 + scale2) + shift2).astype(BF16)


def _mix_out(xs, mods_l, ys, o_ctx, o_lat, z, cu, cv, p, n_ctx_tok, lat_seq):
    n_tok = sum(x.shape[0] for x in xs)
    tm = MIX_TILE
    n_ctx_tiles = n_ctx_tok // tm
    tps = lat_seq // tm
    row = lambda i: (i, 0)
    const = lambda i: (0, 0)
    ctx_row = lambda i: (jnp.minimum(i, n_ctx_tiles - 1), 0)
    lat_row = lambda i: (jnp.maximum(i - n_ctx_tiles, 0), 0)
    ws_cat = jnp.concatenate([p['cm_w_s'][h] for h in range(CM_HEADS)], axis=1).astype(BF16)
    bs_exp = jnp.repeat(p['cm_b_s'].T, CM_HEAD_DIM, axis=1)
    return pl.pallas_call(
        functools.partial(_mix_kernel, n_x=len(xs), n_ctx_tiles=n_ctx_tiles),
        out_shape=[jax.ShapeDtypeStruct((n_tok, D_MODEL), F32),
                   jax.ShapeDtypeStruct((n_tok, D_MODEL), BF16)],
        grid=(n_tok // tm,),
        in_specs=_token_specs(xs, tm, n_ctx_tiles, D_MODEL)
        + [pl.BlockSpec((1, 1, N_MOD * D_MODEL),
                               lambda i: (_mod_row_index(i, n_ctx_tiles, tps), 0, 0)),
                  pl.BlockSpec((2, tm // S5_T, S5_T * 128), lambda i: (0, i, 0)),
                  pl.BlockSpec((tm, DN_WIDTH), ctx_row),
                  pl.BlockSpec((tm, DN_WIDTH), lat_row),
                  pl.BlockSpec((tm, DN_WIDTH), row),
                  pl.BlockSpec((tm, CM_WIDTH), row),
                  pl.BlockSpec((tm, CM_WIDTH), row),
                  pl.BlockSpec((SSM_CH, SSM_CH), const),
                  pl.BlockSpec((1, SSM_CH), const),
                  pl.BlockSpec((1, DN_HEAD_DIM), const),
                  pl.BlockSpec((1, CM_WIDTH), const),
                  pl.BlockSpec((CM_CHUNK, CM_HEADS * CM_CHUNK), const),
                  pl.BlockSpec((CM_CHUNK, CM_WIDTH), const),
                  pl.BlockSpec((D_MODEL, D_MODEL), const),
                  pl.BlockSpec((1, D_MODEL), const)],
        out_specs=[pl.BlockSpec((tm, D_MODEL), row), pl.BlockSpec((tm, D_MODEL), row)],
        scratch_shapes=[pltpu.VMEM((2, tm, 128), F32)],
        compiler_params=_cparams(("parallel",), VMEM_LIMIT),
        name="mix_out",
    )(*xs, mods_l, ys, o_ctx, o_lat, z, cu, cv,
      p['ssm_w_glu'].astype(BF16), p['ssm_b_glu'].reshape(1, SSM_CH),
      p['dn_norm'].reshape(1, DN_HEAD_DIM), p['cm_norm'].reshape(1, CM_WIDTH),
      ws_cat, bs_exp, p['w_out'].astype(BF16), p['norm2'].reshape(1, D_MODEL))


ROUTE_TILE = 512


def _first_max_mask(x, idx, axis):
    m = jnp.max(x, axis=axis, keepdims=True)
    first = jnp.min(jnp.where(x == m, idx, jnp.int32(1 << 20)), axis=axis, keepdims=True)
    return idx == first


def _router_kernel(h_ref, wr_ref, bias_ref, g_ref):
    n = h_ref.shape[0]
    per = N_EXPERTS // N_ROUTE_GROUPS
    logits = _dot_nt(wr_ref[...], h_ref[...])
    scores = _sigmoid(logits)
    choice = scores + bias_ref[...]
    ninf = jnp.float32(-jnp.inf)
    c3 = choice.reshape(N_ROUTE_GROUPS, per, n)
    i3 = lax.broadcasted_iota(jnp.int32, c3.shape, 1)
    top1 = _first_max_mask(c3, i3, 1)
    m1 = jnp.max(c3, axis=1)
    m2 = jnp.max(jnp.where(top1, ninf, c3), axis=1)
    grp = m1 + m2
    gi = lax.broadcasted_iota(jnp.int32, grp.shape, 0)
    gsel = jnp.zeros(grp.shape, jnp.bool_)
    work = grp
    for _ in range(TOPK_ROUTE_GROUPS):
        pick = _first_max_mask(work, gi, 0)
        gsel = gsel | pick
        work = jnp.where(pick, ninf, work)
    gmask = jnp.broadcast_to(gsel[:, None, :], c3.shape).reshape(N_EXPERTS, n)
    work = jnp.where(gmask, choice, ninf)
    ei = lax.broadcasted_iota(jnp.int32, work.shape, 0)
    esel = jnp.zeros(work.shape, jnp.bool_)
    for _ in range(TOP_K):
        pick = _first_max_mask(work, ei, 0)
        esel = esel | pick
        work = jnp.where(pick, ninf, work)
    wts = jnp.where(esel, scores, 0.0)
    gates_t = wts / (jnp.sum(wts, axis=0, keepdims=True) + 1e-20) * ROUTED_SCALE
    gates = gates_t.T
    for j in range(N_EXPERTS // MOE_EG):
        g_ref[j] = gates[:, j * MOE_EG:(j + 1) * MOE_EG]


def _router(h2, router_w, router_bias):
    n_tok = h2.shape[0]
    tm = ROUTE_TILE
    neg = N_EXPERTS // MOE_EG
    return pl.pallas_call(
        _router_kernel,
        out_shape=jax.ShapeDtypeStruct((neg, n_tok, MOE_EG), F32),
        grid=(n_tok // tm,),
        in_specs=[pl.BlockSpec((tm, D_MODEL), lambda i: (i, 0)),
                  pl.BlockSpec((N_EXPERTS, D_MODEL), lambda i: (0, 0)),
                  pl.BlockSpec((N_EXPERTS, 1), lambda i: (0, 0))],
        out_specs=pl.BlockSpec((neg, tm, MOE_EG), lambda i: (0, i, 0)),
        compiler_params=_cparams(("parallel",)),
        name="router",
    )(h2, router_w.T.astype(BF16), router_bias.reshape(N_EXPERTS, 1))


MOE_TILE = 1024
MOE_EG = 8


def _moe_kernel(h_ref, x_ref, mod_ref, g_ref, wg_ref, wu_ref, wd_ref, sg_ref, su_ref, sd_ref, nf_ref,
                *out_refs, final, n_ctx_tiles):
    j = pl.program_id(1)
    nj = pl.num_programs(1)
    acc_ref = out_refs[-1]
    h = h_ref[...]

    @pl.when(j == 0)
    def _():
        sh = _silu(_dot(h, sg_ref[...])) * _dot(h, su_ref[...])
        acc_ref[...] = _dot(sh.astype(BF16), sd_ref[...])

    hg = _dot(h, wg_ref[...])
    hu = _dot(h, wu_ref[...])
    gates = g_ref[0]
    cols = []
    for e in range(MOE_EG):
        sl = slice(e * EXPERT_FF, (e + 1) * EXPERT_FF)
        cols.append((_silu(hg[:, sl]) * hu[:, sl] * gates[:, e:e + 1]).astype(BF16))
    act = jnp.concatenate(cols, axis=1)
    acc_ref[...] += _dot(act, wd_ref[...])

    @pl.when(j == nj - 1)
    def _():
        gate2 = mod_ref[0][:, 5 * D_MODEL:6 * D_MODEL]
        x2 = x_ref[...] + gate2 * acc_ref[...]
        if final:
            y = _rms(x2) * nf_ref[...]
            is_ctx = pl.program_id(0) < n_ctx_tiles

            @pl.when(is_ctx)
            def _():
                out_refs[0][...] = y

            @pl.when(jnp.logical_not(is_ctx))
            def _():
                out_refs[1][...] = y
        else:
            out_refs[0][...] = x2


def _expert_cols_kernel(w_ref, o_ref):
    for e in range(w_ref.shape[0]):
        o_ref[:, e * EXPERT_FF:(e + 1) * EXPERT_FF] = w_ref[e].astype(BF16)


def _expert_cols(w, layer):
    _, n_e, d, f = w.shape
    return pl.pallas_call(
        _expert_cols_kernel,
        out_shape=jax.ShapeDtypeStruct((d, n_e * f), BF16),
        grid=(n_e // MOE_EG,),
        in_specs=[pl.BlockSpec((None, MOE_EG, d, f), lambda j: (layer, j, 0, 0))],
        out_specs=pl.BlockSpec((d, MOE_EG * f), lambda j: (0, j)),
        compiler_params=_cparams(("parallel",)),
        name="expert_cols",
    )(w)


def _cast_rows_kernel(w_ref, o_ref):
    o_ref[...] = w_ref[...].astype(BF16)


def _cast_rows(w, layer, rows):
    _, r, c = w.shape
    return pl.pallas_call(
        _cast_rows_kernel,
        out_shape=jax.ShapeDtypeStruct((r, c), BF16),
        grid=(r // rows,),
        in_specs=[pl.BlockSpec((None, rows, c), lambda j: (layer, j, 0))],
        out_specs=pl.BlockSpec((rows, c), lambda j: (j, 0)),
        compiler_params=_cparams(("parallel",)),
        name="cast_rows",
    )(w)


def _moe(h2, x1, mods_l, gates_t, w_gate, w_up, w_down, layer, p, norm_f, final, n_ctx_tok, lat_seq):
    n_tok = h2.shape[0]
    tm = MOE_TILE
    n_ctx_tiles = n_ctx_tok // tm
    tps = lat_seq // tm
    neg = N_EXPERTS // MOE_EG
    ef = MOE_EG * EXPERT_FF
    wg = _expert_cols(w_gate, layer)
    wu = _expert_cols(w_up, layer)
    wd = _cast_rows(w_down.reshape(w_down.shape[0], N_EXPERTS * EXPERT_FF, D_MODEL), layer, 1024)
    row = lambda i, j: (i, 0)
    const = lambda i, j: (0, 0)
    if final:
        out_shape = [jax.ShapeDtypeStruct((n_ctx_tok, D_MODEL), F32),
                     jax.ShapeDtypeStruct((n_tok - n_ctx_tok, D_MODEL), F32)]
        out_specs = [pl.BlockSpec((tm, D_MODEL), lambda i, j: (jnp.minimum(i, n_ctx_tiles - 1), 0)),
                     pl.BlockSpec((tm, D_MODEL), lambda i, j: (jnp.maximum(i - n_ctx_tiles, 0), 0))]
    else:
        out_shape = jax.ShapeDtypeStruct((n_tok, D_MODEL), F32)
        out_specs = pl.BlockSpec((tm, D_MODEL), row)
    out = pl.pallas_call(
        functools.partial(_moe_kernel, final=final, n_ctx_tiles=n_ctx_tiles),
        out_shape=out_shape,
        grid=(n_tok // tm, neg),
        in_specs=[pl.BlockSpec((tm, D_MODEL), row),
                  pl.BlockSpec((tm, D_MODEL), row),
                  pl.BlockSpec((1, 1, N_MOD * D_MODEL),
                               lambda i, j: (_mod_row_index(i, n_ctx_tiles, tps), 0, 0)),
                  pl.BlockSpec((1, tm, MOE_EG), lambda i, j: (j, i, 0)),
                  pl.BlockSpec((D_MODEL, ef), lambda i, j: (0, j)),
                  pl.BlockSpec((D_MODEL, ef), lambda i, j: (0, j)),
                  pl.BlockSpec((ef, D_MODEL), lambda i, j: (j, 0)),
                  pl.BlockSpec((D_MODEL, SHARED_FF), const),
                  pl.BlockSpec((D_MODEL, SHARED_FF), const),
                  pl.BlockSpec((SHARED_FF, D_MODEL), const),
                  pl.BlockSpec((1, D_MODEL), const)],
        out_specs=out_specs,
        scratch_shapes=[pltpu.VMEM((tm, D_MODEL), F32)],
        compiler_params=_cparams(("arbitrary", "arbitrary"), VMEM_LIMIT_MAX if final else VMEM_LIMIT),
        name="moe",
    )(h2, x1, mods_l, gates_t, wg, wu, wd,
      p['moe_ws_gate'].astype(BF16), p['moe_ws_up'].astype(BF16), p['moe_ws_down'].astype(BF16),
      norm_f.reshape(1, D_MODEL))
    return out


def kernel(x_prompt, x_sample, state_ssm, state_delta, c, c_ctx, w_ada, b_ada, norm1, norm2, w_in, w_out, ssm_lam_re, ssm_lam_im, ssm_log_dt, ssm_b_re, ssm_b_im, ssm_c_re, ssm_c_im, ssm_d, ssm_w_glu, ssm_b_glu, dn_conv, dn_a_log, dn_dt_bias, dn_norm, cm_norm, cm_w_s, cm_b_s, moe_router, moe_router_bias, moe_w_gate, moe_w_up, moe_w_down, moe_ws_gate, moe_ws_up, moe_ws_down, norm_f):
    n_ctx_seq, ctx_seq, _ = x_prompt.shape
    n_lat_seq, lat_seq, _ = x_sample.shape
    depth = w_ada.shape[0]
    n_ctx_tok = n_ctx_seq * ctx_seq
    n_seq = n_ctx_seq + n_lat_seq
    g = SSM_GROUPS

    xs = [x_prompt.reshape(n_ctx_tok, D_MODEL), x_sample.reshape(n_lat_seq * lat_seq, D_MODEL)]
    cvec = jnp.concatenate([c_ctx[None, :], c, jnp.zeros((8 - 1 - n_lat_seq, D_MODEL), F32)], axis=0)
    mods = _ada_mods(cvec, w_ada, b_ada)

    ssm_states, dn_states = [], []
    y = None
    for l in range(depth):
        p = {'w_out': w_out[l], 'norm2': norm2[l], 'ssm_w_glu': ssm_w_glu[l], 'ssm_b_glu': ssm_b_glu[l],
             'dn_conv': dn_conv[l], 'dn_a_log': dn_a_log[l], 'dn_dt_bias': dn_dt_bias[l],
             'dn_norm': dn_norm[l], 'cm_norm': cm_norm[l], 'cm_w_s': cm_w_s[l], 'cm_b_s': cm_b_s[l],
             'moe_ws_gate': moe_ws_gate[l], 'moe_ws_up': moe_ws_up[l], 'moe_ws_down': moe_ws_down[l]}
        mods_l = mods[l].reshape(8, 1, N_MOD * D_MODEL)
        s_u, z, cu, cv, qkvn, gcb, gcb_t = _in_projection(
            xs, mods_l, norm1[l], _cast_rows(w_in, l, 256), dn_conv[l], dn_a_log[l], dn_dt_bias[l],
            n_ctx_tok, ctx_seq, lat_seq)

        mats = _s5_prep(ssm_lam_re[l], ssm_lam_im[l], ssm_log_dt[l], ssm_b_re[l], ssm_b_im[l],
                        ssm_c_re[l], ssm_c_im[l])
        h0_lat = state_ssm[:, l].transpose(0, 3, 2, 1, 4).reshape(n_lat_seq, g, 256)
        h0 = jnp.concatenate([jnp.zeros((n_ctx_seq, g, 256), F32), h0_lat], axis=0)
        ys, ssm_fin = _s5_mixer(s_u, mats, ssm_d[l], h0, n_ctx_seq, ctx_seq, lat_seq)
        ssm_states.append(ssm_fin[:n_ctx_seq].reshape(n_ctx_seq, g, 2, 2, SSM_STATE).transpose(0, 3, 2, 1, 4))

        o_ctx, o_lat, dn_fin = _deltanet(qkvn, gcb, gcb_t, state_delta[:, l], n_ctx_seq, ctx_seq, n_lat_seq,
                                         lat_seq)
        dn_states.append(dn_fin)

        x1, h2 = _mix_out(xs, mods_l, ys, o_ctx, o_lat, z, cu, cv, p, n_ctx_tok, lat_seq)
        gates_t = _router(h2, moe_router[l], moe_router_bias[l])
        x = _moe(h2, x1, mods_l, gates_t, moe_w_gate, moe_w_up, moe_w_down, l, p, norm_f, l == depth - 1,
                 n_ctx_tok, lat_seq)
        xs = list(x) if isinstance(x, (list, tuple)) else [x]

    y_prompt = xs[0].reshape(x_prompt.shape)
    y_sample = xs[1].reshape(x_sample.shape)
    new_state_ssm = jnp.stack(ssm_states, axis=1)
    new_state_delta = jnp.stack(dn_states, axis=1)
    return (y_prompt, y_sample, new_state_ssm, new_state_delta)
```

```python
import functools
import math

import jax
import jax.numpy as jnp
from jax import lax
from jax.experimental import pallas as pl
from jax.experimental.pallas import tpu as pltpu

F32 = jnp.float32
BF16 = jnp.bfloat16
HIGHEST = lax.Precision.HIGHEST

D_MODEL = 1024
N_MOD = 6
EPS = 1e-6
SSM_CH = 256
SSM_GROUPS = 16
SSM_GROUP_CH = 16
SSM_STATE = 64
S5_T = 16
S5_W = S5_T * SSM_GROUP_CH
DN_HEADS = 4
DN_HEAD_DIM = 128
DN_WIDTH = 512
DN_CHUNK = 64
CM_HEADS = 4
CM_WIDTH = 256
CM_HEAD_DIM = 64
CM_CHUNK = 128
N_EXPERTS = 64
TOP_K = 6
N_ROUTE_GROUPS = 8
TOPK_ROUTE_GROUPS = 4
EXPERT_FF = 128
SHARED_FF = 256
ROUTED_SCALE = 2.5
MAIN_COLS = SSM_CH + 3 * DN_WIDTH + DN_WIDTH + 2 * CM_WIDTH
VMEM_LIMIT = 56 * 1024 * 1024
VMEM_LIMIT_MAX = 60 * 1024 * 1024


def _cparams(sem, vmem=None):
    return pltpu.CompilerParams(dimension_semantics=sem, vmem_limit_bytes=vmem)


def _dot(a, b):
    return jnp.dot(a, b, preferred_element_type=F32)


def _dot_nt(a, b, precision=None):
    return lax.dot_general(a, b, (((1,), (1,)), ((), ())), preferred_element_type=F32,
                           precision=precision)


def _dot_tn(a, b):
    return lax.dot_general(a, b, (((0,), (0,)), ((), ())), preferred_element_type=F32)


def _sigmoid(x):
    return 1.0 / (1.0 + jnp.exp(-x))


def _silu(x):
    return x * _sigmoid(x)


def _gelu(x):
    c = math.sqrt(2.0 / math.pi)
    return 0.5 * x * (1.0 + jnp.tanh(c * (x + 0.044715 * (x * x * x))))


def _softplus(x):
    return jnp.maximum(x, 0.0) + jnp.log(1.0 + jnp.exp(-jnp.abs(x)))


def _rms(x):
    return x * lax.rsqrt(jnp.mean(x * x, axis=-1, keepdims=True) + EPS)


def _mod_row_index(tile, n_ctx_tiles, tiles_per_lat_seq):
    return jnp.where(tile < n_ctx_tiles, 0, 1 + (tile - n_ctx_tiles) // tiles_per_lat_seq)


def _ada_kernel(c_ref, w_ref, b_ref, o_ref):
    h = _silu(c_ref[...])
    o_ref[0] = jnp.dot(h, w_ref[0], preferred_element_type=F32, precision=HIGHEST) + b_ref[0]


def _ada_mods(cvec, w_ada, b_ada):
    depth, d, n = w_ada.shape
    bn = 1536
    return pl.pallas_call(
        _ada_kernel,
        out_shape=jax.ShapeDtypeStruct((depth, 8, n), F32),
        grid=(depth, n // bn),
        in_specs=[pl.BlockSpec((8, d), lambda l, j: (0, 0)),
                  pl.BlockSpec((1, d, bn), lambda l, j: (l, 0, j)),
                  pl.BlockSpec((1, 1, bn), lambda l, j: (l, 0, j))],
        out_specs=pl.BlockSpec((1, 8, bn), lambda l, j: (l, 0, j)),
        compiler_params=_cparams(("parallel", "parallel")),
        name="ada_mods",
    )(cvec, w_ada, b_ada.reshape(depth, 1, n))


def _token_specs(xs, tm, n_ctx_tiles, width):
    if len(xs) == 1:
        return [pl.BlockSpec((tm, width), lambda i, *_: (i, 0))]
    return [pl.BlockSpec((tm, width), lambda i, *_: (jnp.minimum(i, n_ctx_tiles - 1), 0)),
            pl.BlockSpec((tm, width), lambda i, *_: (jnp.maximum(i - n_ctx_tiles, 0), 0))]


def _token_tile(refs, n_ctx_tiles, tile=None):
    if len(refs) == 1:
        return refs[0][...]
    tile = pl.program_id(0) if tile is None else tile
    return jnp.where(tile < n_ctx_tiles, refs[0][...], refs[1][...])


IN_TILE = 512


def _inproj_kernel(*refs, n_x, n_ctx_tiles, tiles_per_lat_seq, ctx_seq):
    x_refs = refs[:n_x]
    (prev_ref, next_ref, mod_ref, g_ref, w_ref, cw_ref, alog_ref, dtb_ref,
     su_ref, z_ref, cu_ref, cv_ref, qkvn_ref, gcb_ref, gt_ref, scr) = refs[n_x:]
    mod = mod_ref[0]
    shift = mod[:, 0:D_MODEL]
    scale = mod[:, D_MODEL:2 * D_MODEL]
    gain = g_ref[...] * (1.0 + scale)

    def normed(v):
        return (_rms(v) * gain + shift).astype(BF16)

    c0 = SSM_CH
    c1 = c0 + 3 * DN_WIDTH
    c2 = c1 + DN_WIDTH
    n_ab = 4 * DN_HEADS
    hb = normed(_token_tile(x_refs, n_ctx_tiles))
    x = _dot(hb, w_ref[:, c0:c1])
    tail = _dot(hb, w_ref[:, c2:])
    su = _dot(hb, w_ref[:, 0:c0])
    z_ref[...] = _dot(hb, w_ref[:, c1:c2])
    cu_ref[...] = tail[:, n_ab:n_ab + CM_WIDTH]
    cv_ref[...] = tail[:, n_ab + CM_WIDTH:n_ab + 2 * CM_WIDTH]
    scr[0] = su[:, 0:128]
    scr[1] = su[:, 128:c0]
    n_rows = su_ref.shape[1]
    for half in range(2):
        for t in range(S5_T):
            su_ref[half, :, t * 128:(t + 1) * 128] = scr[half, pl.ds(t, n_rows, stride=S5_T), :]

    n = x.shape[0]
    i = pl.program_id(0)
    is_ctx = i < n_ctx_tiles
    lat_pos = (i - n_ctx_tiles) % tiles_per_lat_seq
    has_prev = jnp.logical_not(is_ctx) & (lat_pos > 0)
    has_next = jnp.logical_not(is_ctx) & (lat_pos < tiles_per_lat_seq - 1)
    w_qkv = w_ref[:, c0:c1]
    prev_row = jnp.where(has_prev, _dot(normed(prev_ref[...]), w_qkv)[7:8, :], 0.0)
    next_row = jnp.where(has_next, _dot(normed(next_ref[...]), w_qkv)[0:1, :], 0.0)
    row = lax.broadcasted_iota(jnp.int32, (n, 1), 0)
    seq_first = is_ctx & (row % ctx_seq == 0)
    seq_last = is_ctx & (row % ctx_seq == ctx_seq - 1)
    xp = jnp.where(row == 0, prev_row, pltpu.roll(x, 1, 0))
    xn = jnp.where(row == n - 1, next_row, pltpu.roll(x, n - 1, 0))
    xp = jnp.where(seq_first, 0.0, xp)
    xn = jnp.where(seq_last, 0.0, xn)
    cw = cw_ref[...]
    y = _silu(cw[0:1] * xp + cw[1:2] * x + cw[2:3] * xn)
    for hcol in range(2 * DN_HEADS):
        seg = y[:, hcol * DN_HEAD_DIM:(hcol + 1) * DN_HEAD_DIM]
        nrm = seg * lax.rsqrt(jnp.sum(seg * seg, axis=-1, keepdims=True) + EPS)
        if hcol < DN_HEADS:
            nrm = nrm * (DN_HEAD_DIM ** -0.5)
        qkvn_ref[:, hcol * DN_HEAD_DIM:(hcol + 1) * DN_HEAD_DIM] = nrm.astype(BF16)
    qkvn_ref[:, 2 * DN_WIDTH:] = y[:, 2 * DN_WIDTH:].astype(BF16)

    ab = tail[:, 0:128]
    lane = lax.broadcasted_iota(jnp.int32, (1, 128), 1)
    glog = jnp.where(lane < 2 * DN_HEADS, -jnp.exp(alog_ref[...]) * _softplus(ab + dtb_ref[...]), 0.0)
    beta = _sigmoid(ab)
    n_ch = n // DN_CHUNK
    wide = jnp.concatenate([glog[c * DN_CHUNK:(c + 1) * DN_CHUNK] for c in range(n_ch)], axis=1)
    ii = lax.broadcasted_iota(jnp.int32, (DN_CHUNK, DN_CHUNK), 0)
    jj = lax.broadcasted_iota(jnp.int32, (DN_CHUNK, DN_CHUNK), 1)
    pre = jnp.where(jj <= ii, 1.0, 0.0).astype(F32)
    suf = jnp.where(jj >= ii, 1.0, 0.0).astype(F32)
    gpre_w = jnp.dot(pre, wide, preferred_element_type=F32, precision=HIGHEST)
    gsuf_w = jnp.dot(suf, wide, preferred_element_type=F32, precision=HIGHEST)
    unwide = lambda a: jnp.concatenate([a[:, c * 128:(c + 1) * 128] for c in range(n_ch)], axis=0)
    gcb = jnp.where(lane < DN_HEADS, unwide(gpre_w), jnp.where(lane < 2 * DN_HEADS, unwide(gsuf_w), beta))
    gcb_ref[...] = gcb
    gcb_t = gcb.T
    for c in range(n_ch):
        gt_ref[c] = gcb_t[0:16, c * DN_CHUNK:(c + 1) * DN_CHUNK]


def _in_projection(xs, mods_l, norm1, w_in_bf, conv_w, a_log, dt_bias, n_ctx_tok, ctx_seq, lat_seq):
    n_tok = sum(x.shape[0] for x in xs)
    tm = IN_TILE
    assert tm % ctx_seq == 0 and lat_seq % tm == 0
    n_ctx_tiles = n_ctx_tok // tm
    tps = lat_seq // tm
    row = lambda i: (i, 0)
    const = lambda i: (0, 0)
    in_cols = w_in_bf.shape[1]
    r8 = tm // 8
    halo_src = xs[-1]
    first = n_ctx_tiles if len(xs) == 2 else 0
    last8 = halo_src.shape[0] // 8 - 1
    prev_spec = pl.BlockSpec((8, D_MODEL), lambda i: (jnp.clip((i - first) * r8 - 1, 0, last8), 0))
    next_spec = pl.BlockSpec((8, D_MODEL), lambda i: (jnp.clip((i - first + 1) * r8, 0, last8), 0))
    w8 = jnp.concatenate([conv_w, jnp.zeros((5, 3 * DN_WIDTH), F32)], axis=0)
    pad = lambda v: jnp.concatenate([v.reshape(1, 2 * DN_HEADS), jnp.zeros((1, 120), F32)], axis=1)
    outs = [(DN_WIDTH, F32), (CM_WIDTH, F32), (CM_WIDTH, F32), (3 * DN_WIDTH, BF16), (128, F32)]
    return pl.pallas_call(
        functools.partial(_inproj_kernel, n_x=len(xs), n_ctx_tiles=n_ctx_tiles, tiles_per_lat_seq=tps,
                          ctx_seq=ctx_seq),
        out_shape=[jax.ShapeDtypeStruct((2, n_tok // S5_T, S5_T * 128), F32)]
        + [jax.ShapeDtypeStruct((n_tok, w), dt) for w, dt in outs]
        + [jax.ShapeDtypeStruct((n_tok // DN_CHUNK, 16, DN_CHUNK), F32)],
        grid=(n_tok // tm,),
        in_specs=_token_specs(xs, tm, n_ctx_tiles, D_MODEL)
        + [prev_spec, next_spec,
           pl.BlockSpec((1, 1, N_MOD * D_MODEL),
                        lambda i: (_mod_row_index(i, n_ctx_tiles, tps), 0, 0)),
           pl.BlockSpec((1, D_MODEL), const),
           pl.BlockSpec((D_MODEL, in_cols), const),
           pl.BlockSpec((8, 3 * DN_WIDTH), const),
           pl.BlockSpec((1, 128), const),
           pl.BlockSpec((1, 128), const)],
        out_specs=[pl.BlockSpec((2, tm // S5_T, S5_T * 128), lambda i: (0, i, 0))]
        + [pl.BlockSpec((tm, w), row) for w, _ in outs]
        + [pl.BlockSpec((tm // DN_CHUNK, 16, DN_CHUNK), lambda i: (i, 0, 0))],
        scratch_shapes=[pltpu.VMEM((2, tm, 128), F32)],
        compiler_params=_cparams(("parallel",), VMEM_LIMIT),
        name="in_projection",
    )(*xs, halo_src, halo_src, mods_l, norm1.reshape(1, D_MODEL), w_in_bf, w8, pad(a_log), pad(dt_bias))


def _s5_prep_kernel(lre_ref, lim_ref, ldt_ref, btre_ref, btim_ref, cre_ref, cim_ref,
                    mt_ref, pt_ref, qt_ref, at_ref):
    t_n, hg, w = S5_T, SSM_GROUP_CH, S5_W
    lre, lim = lre_ref[0], lim_ref[0]
    dt = jnp.exp(ldt_ref[0])
    xr, xi = lre * dt, lim * dt
    tau = lax.broadcasted_iota(jnp.int32, (t_n, 128), 0).astype(F32)
    mag = jnp.exp(tau * xr)
    a_re, a_im = mag * jnp.cos(tau * xi), mag * jnp.sin(tau * xi)
    ar, ai = jnp.exp(xr) * jnp.cos(xi), jnp.exp(xr) * jnp.sin(xi)
    inv = 1.0 / (lre * lre + lim * lim)
    br_, bi_ = lre * inv, -lim * inv
    fr = (ar - 1.0) * br_ - ai * bi_
    fi = (ar - 1.0) * bi_ + ai * br_
    btre, btim = btre_ref[0], btim_ref[0]
    bbr = fr * btre - fi * btim
    bbi = fr * btim + fi * btre
    cre, cim = cre_ref[0], cim_ref[0]

    lane = lax.broadcasted_iota(jnp.int32, (1, 128), 1)
    is_f = lane < SSM_STATE

    def rep_rows(a):
        return jnp.concatenate([jnp.broadcast_to(a[t:t + 1], (hg, 128)) for t in range(t_n)], axis=0)

    def tile_rows(c):
        return jnp.concatenate([c] * t_n, axis=0)

    def rev_rows(a):
        return jnp.concatenate([a[t_n - 1 - t:t_n - t] for t in range(t_n)], axis=0)

    cr_t, ci_t = tile_rows(cre), tile_rows(cim)
    a_re_rep, a_im_rep = rep_rows(a_re), rep_rows(a_im)
    w_re = cr_t * a_re_rep - ci_t * a_im_rep
    w_im = cr_t * a_im_rep + ci_t * a_re_rep
    zero = jnp.zeros_like(w_re)
    e_f = jnp.concatenate([jnp.where(is_f, w_re, zero), jnp.where(is_f, -w_im, zero)], axis=1)
    e_b = jnp.concatenate([jnp.where(is_f, zero, w_re), jnp.where(is_f, zero, -w_im)], axis=1)
    e_b_rev = jnp.concatenate([e_b[(t_n - 1 - t) * hg:(t_n - t) * hg] for t in range(t_n)], axis=0)
    bb_t = tile_rows(jnp.concatenate([bbr, bbi], axis=1))
    row_blk = lax.broadcasted_iota(jnp.int32, (w, w), 0) // hg

    m = jnp.zeros((w, w), F32)
    zrows = lambda n: jnp.zeros((n * hg, w), F32)
    for s in range(t_n):
        down = e_f if s == 0 else jnp.concatenate([zrows(s), e_f[:(t_n - s) * hg]], axis=0)
        k = t_n - 1 - s
        up = e_b_rev if k == 0 else jnp.concatenate([e_b_rev[k * hg:], zrows(k)], axis=0)
        rhs = jnp.where(row_blk == s, bb_t, 0.0)
        m = m + _dot_nt((down + up).astype(BF16), rhs.astype(BF16))
    mt_ref[...] = m.T.reshape(t_n, hg, w).astype(BF16)

    a_re_rev, a_im_rev = rev_rows(a_re), rev_rows(a_im)
    ps_re = rep_rows(jnp.where(is_f, a_re_rev, a_re))
    ps_im = rep_rows(jnp.where(is_f, a_im_rev, a_im))
    bbr_t, bbi_t = tile_rows(bbr), tile_rows(bbi)
    p_re = ps_re * bbr_t - ps_im * bbi_t
    p_im = ps_re * bbi_t + ps_im * bbr_t
    pt_ref[...] = jnp.concatenate([p_re, p_im], axis=1).reshape(t_n, hg, w).astype(BF16)

    a1_re = a_re * ar - a_im * ai
    a1_im = a_re * ai + a_im * ar
    aq_re = rep_rows(jnp.where(is_f, a1_re, rev_rows(a1_re)))
    aq_im = rep_rows(jnp.where(is_f, a1_im, rev_rows(a1_im)))
    q_re = cr_t * aq_re - ci_t * aq_im
    q_im = cr_t * aq_im + ci_t * aq_re
    qt_ref[0] = jnp.concatenate([q_re, -q_im], axis=1).T.astype(BF16)
    at_ref[0] = jnp.concatenate([a1_re[t_n - 1:t_n], a1_im[t_n - 1:t_n]], axis=1)


def _s5_prep(lam_re, lam_im, log_dt, b_re, b_im, c_re, c_im):
    g = SSM_GROUPS
    cat = lambda a: jnp.concatenate([a[0], a[1]], axis=-1)
    lre = cat(lam_re)[:, None, :]
    lim = cat(lam_im)[:, None, :]
    ldt = cat(jnp.broadcast_to(log_dt[..., None], (2, g, SSM_STATE)))[:, None, :]
    btre = cat(jnp.swapaxes(b_re, -1, -2))
    btim = cat(jnp.swapaxes(b_im, -1, -2))
    cre, cim = cat(c_re), cat(c_im)
    vec = pl.BlockSpec((1, 1, 128), lambda i: (i, 0, 0))
    mat = pl.BlockSpec((1, SSM_GROUP_CH, 128), lambda i: (i, 0, 0))
    big = pl.BlockSpec((1, S5_W, S5_W), lambda i: (i, 0, 0))
    gh = g // 2
    arranged = pl.BlockSpec((None, S5_T, None, SSM_GROUP_CH, S5_W), lambda i: (i // gh, 0, i % gh, 0, 0))
    arranged_shape = jax.ShapeDtypeStruct((2, S5_T, gh, SSM_GROUP_CH, S5_W), BF16)
    return pl.pallas_call(
        _s5_prep_kernel,
        out_shape=[arranged_shape, arranged_shape, jax.ShapeDtypeStruct((g, S5_W, S5_W), BF16),
                   jax.ShapeDtypeStruct((g, 1, 256), F32)],
        grid=(g,),
        in_specs=[vec, vec, vec, mat, mat, mat, mat],
        out_specs=[arranged, arranged, big, pl.BlockSpec((1, 1, 256), lambda i: (i, 0, 0))],
        compiler_params=_cparams(("parallel",)),
        name="s5_prep",
    )(lre, lim, ldt, btre, btim, cre, cim)


S5_SPREAD_ROWS = 512


def _s5_spread_kernel(a_ref, o_ref, *, rows_per_group, gh):
    hg, w = SSM_GROUP_CH, S5_W
    n_rows, n_cols = o_ref.shape
    src = lax.broadcasted_iota(jnp.int32, (w, n_cols), 0)
    dst = lax.broadcasted_iota(jnp.int32, (w, n_cols), 1)
    tile = jnp.where((src // hg == dst // 128) & (src % hg == dst % hg), 1.0, 0.0).astype(BF16)
    row = pl.program_id(1) * n_rows + lax.broadcasted_iota(jnp.int32, (n_rows, n_cols), 0)
    col = lax.broadcasted_iota(jnp.int32, (n_rows, n_cols), 1)
    keep = (row // rows_per_group) % gh == (col // hg) % gh
    o_ref[...] = jnp.where(keep, _dot(a_ref[...], tile), 0.0).astype(BF16)


def _s5_spread_cols(a, rows_per_group, gh):
    _, n_rows, w = a.shape
    rb = S5_SPREAD_ROWS
    return pl.pallas_call(
        functools.partial(_s5_spread_kernel, rows_per_group=rows_per_group, gh=gh),
        out_shape=jax.ShapeDtypeStruct((2, n_rows, S5_T * 128), BF16),
        grid=(2, n_rows // rb),
        in_specs=[pl.BlockSpec((None, rb, w), lambda z, i: (z, i, 0))],
        out_specs=pl.BlockSpec((None, rb, S5_T * 128), lambda z, i: (z, i, 0)),
        compiler_params=_cparams(("parallel", "parallel")),
        name="s5_spread_cols",
    )(a)


def _s5_place_kernel(a_ref, o_ref, *, gh):
    a = a_ref[...]
    n_rows, w = a.shape
    grp = ((pl.program_id(1) * n_rows + lax.broadcasted_iota(jnp.int32, (n_rows, w), 0))
           // SSM_GROUP_CH) % gh
    for k in range(gh):
        o_ref[:, k * w:(k + 1) * w] = jnp.where(grp == k, a, jnp.zeros_like(a))


def _s5_place_cols(a, gh):
    _, n_rows, w = a.shape
    rb = S5_SPREAD_ROWS
    return pl.pallas_call(
        functools.partial(_s5_place_kernel, gh=gh),
        out_shape=jax.ShapeDtypeStruct((2, n_rows, gh * w), BF16),
        grid=(2, n_rows // rb),
        in_specs=[pl.BlockSpec((None, rb, w), lambda z, i: (z, i, 0))],
        out_specs=pl.BlockSpec((None, rb, gh * w), lambda z, i: (z, i, 0)),
        compiler_params=_cparams(("parallel", "parallel")),
        name="s5_place_cols",
    )(a)


def _s5_z_kernel(u_ref, pt_ref, z_ref):
    z_ref[0] = _dot(u_ref[0].astype(BF16), pt_ref[0])


def _s5_rec_kernel(z_ref, a_ref, h0_ref, hf_ref, hb_ref, fin_ref, *, n_ctx_seq, ctx_chunks, n_lat_seq,
                   lat_chunks):
    sw = z_ref.shape[-1]
    ng = sw // 256
    lane = lax.broadcasted_iota(jnp.int32, (1, sw), 1)
    is_f = (lane % 128) < SSM_STATE

    def split(row):
        re = jnp.concatenate([row[:, g * 256:g * 256 + 128] for g in range(ng)], axis=1)
        im = jnp.concatenate([row[:, g * 256 + 128:(g + 1) * 256] for g in range(ng)], axis=1)
        return re, im

    def merge(re, im):
        parts = []
        for g in range(ng):
            parts += [re[:, g * 128:(g + 1) * 128], im[:, g * 128:(g + 1) * 128]]
        return jnp.concatenate(parts, axis=1)

    a_re, a_im = split(a_ref[0])

    def run_seq(base, n, seq):
        def step(k, h):
            h_re, h_im = h
            cf = base + k
            cb = base + n - 1 - k
            row = merge(h_re, h_im)
            hf_ref[0, pl.ds(cf, 1), :] = row
            hb_ref[0, pl.ds(cb, 1), :] = row
            z_re, z_im = split(jnp.where(is_f, z_ref[0, pl.ds(cf, 1), :], z_ref[0, pl.ds(cb, 1), :]))
            return (a_re * h_re - a_im * h_im + z_re, a_re * h_im + a_im * h_re + z_im)
        h_re, h_im = lax.fori_loop(0, n, step, split(h0_ref[0, pl.ds(seq, 1), :]))
        fin_ref[0, pl.ds(seq, 1), :] = merge(h_re, h_im)

    def ctx_body(s, carry):
        run_seq(s * ctx_chunks, ctx_chunks, s)
        return carry
    lax.fori_loop(0, n_ctx_seq, ctx_body, 0)

    def lat_body(s, carry):
        run_seq(n_ctx_seq * ctx_chunks + s * lat_chunks, lat_chunks, n_ctx_seq + s)
        return carry
    lax.fori_loop(0, n_lat_seq, lat_body, 0)


def _s5_y_kernel(u_ref, hf_ref, hb_ref, mt_ref, qt_ref, d_ref, y_ref):
    u = u_ref[0]
    lane = lax.broadcasted_iota(jnp.int32, (1, u.shape[-1]), 1)
    is_f = (lane % 128) < SSM_STATE
    h = jnp.where(is_f, hf_ref[0], hb_ref[0]).astype(BF16)
    y_ref[0] = _dot(u.astype(BF16), mt_ref[0]) + _dot(h, qt_ref[0]) + d_ref[0] * u


def _s5_mixer(s_u, mats, ssm_d, h0, n_ctx_seq, ctx_seq, lat_seq):
    mt, pt, qt, at = mats
    g, t_n, w, hg = SSM_GROUPS, S5_T, S5_W, SSM_GROUP_CH
    gh = g // 2
    nc = s_u.shape[1]
    bw = t_n * 128
    sw = gh * w
    n_seq = h0.shape[0]
    ctx_chunks, lat_chunks = ctx_seq // t_n, lat_seq // t_n
    n_lat_seq = n_seq - n_ctx_seq
    u = s_u
    m_big = _s5_spread_cols(mt.reshape(2, bw, w), hg, gh)
    q_big = _s5_spread_cols(qt.reshape(2, sw, w), w, gh)
    p_big = _s5_place_cols(pt.reshape(2, bw, w), gh)
    d_big = jnp.tile(ssm_d.reshape(2, 1, 128), (1, 1, t_n))

    rb = math.gcd(nc, 256)
    u_spec = pl.BlockSpec((1, rb, bw), lambda z, i: (z, i, 0))
    s_spec = pl.BlockSpec((1, rb, sw), lambda z, i: (z, i, 0))
    z = pl.pallas_call(
        _s5_z_kernel,
        out_shape=jax.ShapeDtypeStruct((2, nc, sw), F32),
        grid=(2, nc // rb),
        in_specs=[u_spec, pl.BlockSpec((1, bw, sw), lambda z, i: (z, 0, 0))],
        out_specs=s_spec,
        compiler_params=_cparams(("parallel", "parallel"), VMEM_LIMIT),
        name="s5_chunk_inputs",
    )(u, p_big)

    half = lambda rows: pl.BlockSpec((1, rows, sw), lambda z: (z, 0, 0))
    hf, hb, fin = pl.pallas_call(
        functools.partial(_s5_rec_kernel, n_ctx_seq=n_ctx_seq, ctx_chunks=ctx_chunks,
                          n_lat_seq=n_lat_seq, lat_chunks=lat_chunks),
        out_shape=[jax.ShapeDtypeStruct((2, nc, sw), F32)] * 2 + [jax.ShapeDtypeStruct((2, n_seq, sw), F32)],
        grid=(2,),
        in_specs=[half(nc), half(1), half(n_seq)],
        out_specs=[half(nc), half(nc), half(n_seq)],
        compiler_params=_cparams(("parallel",), VMEM_LIMIT),
        name="s5_recurrence",
    )(z, at.reshape(2, 1, sw), h0.reshape(n_seq, 2, sw).transpose(1, 0, 2))

    y = pl.pallas_call(
        _s5_y_kernel,
        out_shape=jax.ShapeDtypeStruct((2, nc, bw), F32),
        grid=(2, nc // rb),
        in_specs=[u_spec, s_spec, s_spec,
                  pl.BlockSpec((1, bw, bw), lambda z, i: (z, 0, 0)),
                  pl.BlockSpec((1, sw, bw), lambda z, i: (z, 0, 0)),
                  pl.BlockSpec((1, 1, bw), lambda z, i: (z, 0, 0))],
        out_specs=u_spec,
        compiler_params=_cparams(("parallel", "parallel"), VMEM_LIMIT),
        name="s5_outputs",
    )(u, hf, hb, m_big, q_big, d_big)
    return y, fin.transpose(1, 0, 2).reshape(n_seq, g, w)


DN_CHAINS = 2 * DN_HEADS


def _split(x):
    hi = x.astype(BF16)
    lo = (x - hi.astype(F32)).astype(BF16)
    return hi, lo


def _split_lhs(a):
    hi, lo = _split(a)
    hif = hi.astype(F32)
    return jnp.concatenate([hif, hif, lo.astype(F32)], axis=1).astype(BF16)


def _split_rhs(b):
    hi, lo = _split(b)
    return jnp.concatenate([hi, lo, hi], axis=0)


def _dn_prepare(q_ref, k_ref, v_ref, gc_ref, gt_ref, row0, chunk_idx):
    c = DN_CHUNK
    ii = lax.broadcasted_iota(jnp.int32, (c, c), 0)
    jj = lax.broadcasted_iota(jnp.int32, (c, c), 1)
    items = []
    for d in range(2):
        rows = pl.ds(row0[d], c)
        gcol = gc_ref[rows, :]
        grow = gt_ref[chunk_idx[d]]
        incl = (ii >= jj) if d == 0 else (ii <= jj)
        strict = (ii > jj) if d == 0 else (ii < jj)
        for h in range(DN_HEADS):
            ch = d * DN_HEADS + h
            cols = slice(h * DN_HEAD_DIM, (h + 1) * DN_HEAD_DIM)
            q, k, v = q_ref[rows, cols], k_ref[rows, cols], v_ref[rows, cols]
            gc_col = gcol[:, ch:ch + 1]
            beta = gcol[:, DN_CHAINS + ch:DN_CHAINS + ch + 1]
            gc_row = grow[ch:ch + 1, :]
            kf = k.astype(F32)
            kb = kf * beta
            decay = jnp.exp(jnp.where(incl, gc_col - gc_row, -jnp.inf))
            egc = jnp.exp(gc_col)
            g_last = gc_col[c - 1:c] if d == 0 else gc_col[0:1]
            items.append(dict(
                q=q, k=k, kb=kb.astype(BF16), decay=decay, incl=incl, strict=strict,
                rhs=jnp.concatenate([kb * egc, v.astype(F32) * beta], axis=1),
                qg=(q.astype(F32) * egc).astype(BF16),
                kt=(kf * jnp.exp(g_last - gc_col)).astype(BF16),
                gl=jnp.broadcast_to(jnp.exp(g_last), (8, 128))))
    for it in items:
        it['a'] = jnp.where(it['strict'], _dot_nt(it['kb'], it['k']) * it['decay'], 0.0)
        it['qk'] = jnp.where(it['incl'], _dot_nt(it['q'], it['k']) * it['decay'], 0.0).astype(BF16)
    power = 1
    while power < c:
        for it in items:
            it['lhs'] = _split_lhs(it['a'])
            if 2 * power < c:
                it['a'] = _dot(it['lhs'], _split_rhs(it['a']))
        for it in items:
            upd = _dot(it['lhs'], _split_rhs(it['rhs'] if power == 1 else it['r']))
            it['r'] = it['rhs'] - upd if power == 1 else it['r'] + upd
        power *= 2
    return items


def _dn_scan_kernel(*refs, n_chunks, seq_len, seqs_per_blk, has_s0, write_fin):
    refs = list(refs)
    q_ref, k_ref, v_ref, gc_ref, gt_ref = refs[:5]
    pos = 5
    s0_ref = None
    if has_s0:
        s0_ref = refs[pos]
        pos += 1
    o_ref = refs[pos]
    pos += 1
    fin_ref = None
    if write_fin:
        fin_ref = refs[pos]
        pos += 1
    s_ref, w_s, u_s, qk_s, qg_s, kt_s, gl_s = refs[pos:pos + 7]
    n_steps = seqs_per_blk * n_chunks

    o_ref[...] = jnp.zeros_like(o_ref)
    for r in (s_ref, w_s, u_s, qk_s, qg_s, kt_s, gl_s):
        r[...] = jnp.zeros_like(r)

    def locate(step):
        seq = step // n_chunks
        kk = step % n_chunks
        cf = seq * n_chunks + kk
        cb = seq * n_chunks + (n_chunks - 1 - kk)
        rows = [pl.multiple_of(cf * DN_CHUNK, DN_CHUNK), pl.multiple_of(cb * DN_CHUNK, DN_CHUNK)]
        return seq, kk, rows, [cf, cb]

    def body(t, carry):
        tb = jnp.maximum(t, 0)
        valid = t >= 0
        seq, kk, rows_b, _ = locate(tb)
        first = kk == 0
        vnews = []
        for ch in range(DN_CHAINS):
            d, h = divmod(ch, DN_HEADS)
            s0 = s0_ref[0, d, h] if has_s0 else jnp.zeros((DN_HEAD_DIM, DN_HEAD_DIM), F32)
            s_ref[ch] = jnp.where(first, s0, s_ref[ch])
        for ch in range(DN_CHAINS):
            vnews.append((u_s[ch] - _dot(w_s[ch], s_ref[ch].astype(BF16))).astype(BF16))
        tn = jnp.minimum(t + 1, n_steps - 1)
        _, _, rows_a, chunks_a = locate(tn)
        items = _dn_prepare(q_ref, k_ref, v_ref, gc_ref, gt_ref, rows_a, chunks_a)
        for ch in range(DN_CHAINS):
            d, h = divmod(ch, DN_HEADS)
            state = s_ref[ch]
            o = _dot(qg_s[ch], state.astype(BF16)) + _dot(qk_s[ch], vnews[ch])
            s_new = state * gl_s[ch][0:1, 0:1] + _dot_tn(kt_s[ch], vnews[ch])
            s_ref[ch] = s_new
            if write_fin:
                fin_ref[seq, d, h] = s_new
            rows = pl.ds(rows_b[d], DN_CHUNK)
            cols = slice(h * DN_HEAD_DIM, (h + 1) * DN_HEAD_DIM)
            o_ref[rows, cols] = o_ref[rows, cols] + jnp.where(valid, o, 0.0)
        for ch, it in enumerate(items):
            w_s[ch] = it['r'][:, :DN_HEAD_DIM].astype(BF16)
            u_s[ch] = it['r'][:, DN_HEAD_DIM:]
            qk_s[ch] = it['qk']
            qg_s[ch] = it['qg']
            kt_s[ch] = it['kt']
            gl_s[ch] = it['gl']
        return carry

    lax.fori_loop(-1, n_steps, body, 0)


def _dn_scan(qkvn, gcb, gcb_t, seq_len, n_seq, seqs_per_blk, first_tok, s0, write_fin):
    n_chunks = seq_len // DN_CHUNK
    rows_blk = seqs_per_blk * seq_len
    b0 = first_tok // rows_blk
    hd = DN_HEAD_DIM
    st_spec = pl.BlockSpec((seqs_per_blk, 2, DN_HEADS, hd, hd), lambda s: (s, 0, 0, 0, 0))
    in_specs = [pl.BlockSpec((rows_blk, DN_WIDTH), lambda s: (b0 + s, 0)),
                pl.BlockSpec((rows_blk, DN_WIDTH), lambda s: (b0 + s, 1)),
                pl.BlockSpec((rows_blk, DN_WIDTH), lambda s: (b0 + s, 2)),
                pl.BlockSpec((rows_blk, 128), lambda s: (b0 + s, 0)),
                pl.BlockSpec((rows_blk // DN_CHUNK, 16, DN_CHUNK), lambda s: (b0 + s, 0, 0))]
    args = [qkvn, qkvn, qkvn, gcb, gcb_t]
    if s0 is not None:
        assert seqs_per_blk == 1
        in_specs.append(st_spec)
        args.append(s0)
    out_shape = [jax.ShapeDtypeStruct((n_seq * seq_len, DN_WIDTH), F32)]
    out_specs = [pl.BlockSpec((rows_blk, DN_WIDTH), lambda s: (s, 0))]
    if write_fin:
        out_shape.append(jax.ShapeDtypeStruct((n_seq, 2, DN_HEADS, hd, hd), F32))
        out_specs.append(st_spec)
    c = DN_CHUNK
    scratch = [pltpu.VMEM((DN_CHAINS, hd, hd), F32),
               pltpu.VMEM((DN_CHAINS, c, hd), BF16),
               pltpu.VMEM((DN_CHAINS, c, hd), F32),
               pltpu.VMEM((DN_CHAINS, c, c), BF16),
               pltpu.VMEM((DN_CHAINS, c, hd), BF16),
               pltpu.VMEM((DN_CHAINS, c, hd), BF16),
               pltpu.VMEM((DN_CHAINS, 8, 128), F32)]
    return pl.pallas_call(
        functools.partial(_dn_scan_kernel, n_chunks=n_chunks, seq_len=seq_len, seqs_per_blk=seqs_per_blk,
                          has_s0=s0 is not None, write_fin=write_fin),
        out_shape=out_shape,
        grid=(n_seq // seqs_per_blk,),
        in_specs=in_specs,
        out_specs=out_specs,
        scratch_shapes=scratch,
        compiler_params=_cparams(("parallel",), VMEM_LIMIT),
        name="dn_scan_%d" % seq_len,
    )(*args)


def _deltanet(qkvn, gcb, gcb_t, s0_lat, n_ctx_seq, ctx_seq, n_lat_seq, lat_seq):
    n_ctx_tok = n_ctx_seq * ctx_seq
    ctx_blk = math.gcd(n_ctx_seq, 8)
    o_ctx, fin = _dn_scan(qkvn, gcb, gcb_t, ctx_seq, n_ctx_seq, ctx_blk, 0, None, True)
    (o_lat,) = _dn_scan(qkvn, gcb, gcb_t, lat_seq, n_lat_seq, 1, n_ctx_tok, s0_lat, False)
    return o_ctx, o_lat, fin


MIX_TILE = 256


def _mix_kernel(*refs, n_x, n_ctx_tiles):
    x_refs = refs[:n_x]
    (mod_ref, ys_ref, octx_ref, olat_ref, z_ref, cu_ref, cv_ref,
     wglu_ref, bglu_ref, dnn_ref, cmn_ref, ws_ref, bs_ref, wo_ref, n2_ref,
     x1_ref, h2_ref, scr) = refs[n_x:]
    mod = mod_ref[0]
    gate1 = mod[:, 2 * D_MODEL:3 * D_MODEL]
    shift2 = mod[:, 3 * D_MODEL:4 * D_MODEL]
    scale2 = mod[:, 4 * D_MODEL:5 * D_MODEL]
    n_rows = ys_ref.shape[1]
    for half in range(2):
        for t in range(S5_T):
            scr[half, pl.ds(t, n_rows, stride=S5_T), :] = ys_ref[half, :, t * 128:(t + 1) * 128]
    ya = _gelu(jnp.concatenate([scr[0], scr[1]], axis=1))
    ya = ya * _sigmoid(_dot(ya.astype(BF16), wglu_ref[...]) + bglu_ref[...])
    mix = _dot(ya.astype(BF16), wo_ref[0:SSM_CH, :])
    o = jnp.where(pl.program_id(0) < n_ctx_tiles, octx_ref[...], olat_ref[...])
    z = z_ref[...]
    parts = []
    for h in range(DN_HEADS):
        sl = slice(h * DN_HEAD_DIM, (h + 1) * DN_HEAD_DIM)
        parts.append(_rms(o[:, sl]) * dnn_ref[...] * _silu(z[:, sl]))
    yb = jnp.concatenate(parts, axis=1)
    mix = mix + _dot(yb.astype(BF16), wo_ref[SSM_CH:SSM_CH + DN_WIDTH, :])
    u = _gelu(cu_ref[...])
    v = _gelu(cv_ref[...])
    seg_i = lax.broadcasted_iota(jnp.int32, (CM_WIDTH, CM_WIDTH), 0) // CM_HEAD_DIM
    seg_j = lax.broadcasted_iota(jnp.int32, (CM_WIDTH, CM_WIDTH), 1) // CM_HEAD_DIM
    seg_ones = jnp.where(seg_i == seg_j, 1.0, 0.0).astype(BF16)
    vv = v * v
    vv_hi = vv.astype(BF16)
    vv_lo = (vv - vv_hi.astype(F32)).astype(BF16)
    ms = (_dot(vv_hi, seg_ones) + _dot(vv_lo, seg_ones)) * (1.0 / CM_HEAD_DIM)
    vn = (v * lax.rsqrt(ms + EPS) * cmn_ref[...]).astype(BF16)
    head_of_lane = lax.broadcasted_iota(jnp.int32, (1, CM_WIDTH), 1) // CM_HEAD_DIM
    ycs = []
    for c in range(MIX_TILE // CM_CHUNK):
        vc = vn[c * CM_CHUNK:(c + 1) * CM_CHUNK]
        stack = jnp.concatenate([jnp.where(head_of_lane == h, vc, jnp.zeros_like(vc))
                                 for h in range(CM_HEADS)], axis=0)
        vs = _dot(ws_ref[...], stack) + bs_ref[...]
        ycs.append(u[c * CM_CHUNK:(c + 1) * CM_CHUNK] * vs)
    yc = jnp.concatenate(ycs, axis=0)
    mix = mix + _dot(yc.astype(BF16), wo_ref[SSM_CH + DN_WIDTH:, :])
    x1 = _token_tile(x_refs, n_ctx_tiles) + gate1 * mix
    x1_ref[...] = x1
    h2_ref[...] = (_rms(x1) * n2_ref[...] * (1.0 + scale2) + shift2).astype(BF16)


def _mix_out(xs, mods_l, ys, o_ctx, o_lat, z, cu, cv, p, n_ctx_tok, lat_seq):
    n_tok = sum(x.shape[0] for x in xs)
    tm = MIX_TILE
    n_ctx_tiles = n_ctx_tok // tm
    tps = lat_seq // tm
    row = lambda i: (i, 0)
    const = lambda i: (0, 0)
    ctx_row = lambda i: (jnp.minimum(i, n_ctx_tiles - 1), 0)
    lat_row = lambda i: (jnp.maximum(i - n_ctx_tiles, 0), 0)
    ws_cat = jnp.concatenate([p['cm_w_s'][h] for h in range(CM_HEADS)], axis=1).astype(BF16)
    bs_exp = jnp.repeat(p['cm_b_s'].T, CM_HEAD_DIM, axis=1)
    return pl.pallas_call(
        functools.partial(_mix_kernel, n_x=len(xs), n_ctx_tiles=n_ctx_tiles),
        out_shape=[jax.ShapeDtypeStruct((n_tok, D_MODEL), F32),
                   jax.ShapeDtypeStruct((n_tok, D_MODEL), BF16)],
        grid=(n_tok // tm,),
        in_specs=_token_specs(xs, tm, n_ctx_tiles, D_MODEL)
        + [pl.BlockSpec((1, 1, N_MOD * D_MODEL),
                               lambda i: (_mod_row_index(i, n_ctx_tiles, tps), 0, 0)),
                  pl.BlockSpec((2, tm // S5_T, S5_T * 128), lambda i: (0, i, 0)),
                  pl.BlockSpec((tm, DN_WIDTH), ctx_row),
                  pl.BlockSpec((tm, DN_WIDTH), lat_row),
                  pl.BlockSpec((tm, DN_WIDTH), row),
                  pl.BlockSpec((tm, CM_WIDTH), row),
                  pl.BlockSpec((tm, CM_WIDTH), row),
                  pl.BlockSpec((SSM_CH, SSM_CH), const),
                  pl.BlockSpec((1, SSM_CH), const),
                  pl.BlockSpec((1, DN_HEAD_DIM), const),
                  pl.BlockSpec((1, CM_WIDTH), const),
                  pl.BlockSpec((CM_CHUNK, CM_HEADS * CM_CHUNK), const),
                  pl.BlockSpec((CM_CHUNK, CM_WIDTH), const),
                  pl.BlockSpec((D_MODEL, D_MODEL), const),
                  pl.BlockSpec((1, D_MODEL), const)],
        out_specs=[pl.BlockSpec((tm, D_MODEL), row), pl.BlockSpec((tm, D_MODEL), row)],
        scratch_shapes=[pltpu.VMEM((2, tm, 128), F32)],
        compiler_params=_cparams(("parallel",), VMEM_LIMIT),
        name="mix_out",
    )(*xs, mods_l, ys, o_ctx, o_lat, z, cu, cv,
      p['ssm_w_glu'].astype(BF16), p['ssm_b_glu'].reshape(1, SSM_CH),
      p['dn_norm'].reshape(1, DN_HEAD_DIM), p['cm_norm'].reshape(1, CM_WIDTH),
      ws_cat, bs_exp, p['w_out'].astype(BF16), p['norm2'].reshape(1, D_MODEL))


ROUTE_TILE = 512


def _first_max_mask(x, idx, axis):
    m = jnp.max(x, axis=axis, keepdims=True)
    first = jnp.min(jnp.where(x == m, idx, jnp.int32(1 << 20)), axis=axis, keepdims=True)
    return idx == first


def _router_kernel(h_ref, wr_ref, bias_ref, g_ref):
    n = h_ref.shape[0]
    per = N_EXPERTS // N_ROUTE_GROUPS
    logits = _dot_nt(wr_ref[...], h_ref[...])
    scores = _sigmoid(logits)
    choice = scores + bias_ref[...]
    ninf = jnp.float32(-jnp.inf)
    c3 = choice.reshape(N_ROUTE_GROUPS, per, n)
    i3 = lax.broadcasted_iota(jnp.int32, c3.shape, 1)
    top1 = _first_max_mask(c3, i3, 1)
    m1 = jnp.max(c3, axis=1)
    m2 = jnp.max(jnp.where(top1, ninf, c3), axis=1)
    grp = m1 + m2
    gi = lax.broadcasted_iota(jnp.int32, grp.shape, 0)
    gsel = jnp.zeros(grp.shape, jnp.bool_)
    work = grp
    for _ in range(TOPK_ROUTE_GROUPS):
        pick = _first_max_mask(work, gi, 0)
        gsel = gsel | pick
        work = jnp.where(pick, ninf, work)
    gmask = jnp.broadcast_to(gsel[:, None, :], c3.shape).reshape(N_EXPERTS, n)
    work = jnp.where(gmask, choice, ninf)
    ei = lax.broadcasted_iota(jnp.int32, work.shape, 0)
    esel = jnp.zeros(work.shape, jnp.bool_)
    for _ in range(TOP_K):
        pick = _first_max_mask(work, ei, 0)
        esel = esel | pick
        work = jnp.where(pick, ninf, work)
    wts = jnp.where(esel, scores, 0.0)
    gates_t = wts / (jnp.sum(wts, axis=0, keepdims=True) + 1e-20) * ROUTED_SCALE
    gates = gates_t.T
    for j in range(N_EXPERTS // MOE_EG):
        g_ref[j] = gates[:, j * MOE_EG:(j + 1) * MOE_EG]


def _router(h2, router_w, router_bias):
    n_tok = h2.shape[0]
    tm = ROUTE_TILE
    neg = N_EXPERTS // MOE_EG
    return pl.pallas_call(
        _router_kernel,
        out_shape=jax.ShapeDtypeStruct((neg, n_tok, MOE_EG), F32),
        grid=(n_tok // tm,),
        in_specs=[pl.BlockSpec((tm, D_MODEL), lambda i: (i, 0)),
                  pl.BlockSpec((N_EXPERTS, D_MODEL), lambda i: (0, 0)),
                  pl.BlockSpec((N_EXPERTS, 1), lambda i: (0, 0))],
        out_specs=pl.BlockSpec((neg, tm, MOE_EG), lambda i: (0, i, 0)),
        compiler_params=_cparams(("parallel",)),
        name="router",
    )(h2, router_w.T.astype(BF16), router_bias.reshape(N_EXPERTS, 1))


MOE_TILE = 1024
MOE_EG = 8


def _moe_kernel(h_ref, x_ref, mod_ref, g_ref, wg_ref, wu_ref, wd_ref, sg_ref, su_ref, sd_ref, nf_ref,
                *out_refs, final, n_ctx_tiles):
    j = pl.program_id(1)
    nj = pl.num_programs(1)
    acc_ref = out_refs[-1]
    h = h_ref[...]

    @pl.when(j == 0)
    def _():
        sh = _silu(_dot(h, sg_ref[...])) * _dot(h, su_ref[...])
        acc_ref[...] = _dot(sh.astype(BF16), sd_ref[...])

    hg = _dot(h, wg_ref[...])
    hu = _dot(h, wu_ref[...])
    gates = g_ref[0]
    cols = []
    for e in range(MOE_EG):
        sl = slice(e * EXPERT_FF, (e + 1) * EXPERT_FF)
        cols.append((_silu(hg[:, sl]) * hu[:, sl] * gates[:, e:e + 1]).astype(BF16))
    act = jnp.concatenate(cols, axis=1)
    acc_ref[...] += _dot(act, wd_ref[...])

    @pl.when(j == nj - 1)
    def _():
        gate2 = mod_ref[0][:, 5 * D_MODEL:6 * D_MODEL]
        x2 = x_ref[...] + gate2 * acc_ref[...]
        if final:
            y = _rms(x2) * nf_ref[...]
            is_ctx = pl.program_id(0) < n_ctx_tiles

            @pl.when(is_ctx)
            def _():
                out_refs[0][...] = y

            @pl.when(jnp.logical_not(is_ctx))
            def _():
                out_refs[1][...] = y
        else:
            out_refs[0][...] = x2


def _expert_cols_kernel(w_ref, o_ref):
    for e in range(w_ref.shape[0]):
        o_ref[:, e * EXPERT_FF:(e + 1) * EXPERT_FF] = w_ref[e].astype(BF16)


def _expert_cols(w, layer):
    _, n_e, d, f = w.shape
    return pl.pallas_call(
        _expert_cols_kernel,
        out_shape=jax.ShapeDtypeStruct((d, n_e * f), BF16),
        grid=(n_e // MOE_EG,),
        in_specs=[pl.BlockSpec((None, MOE_EG, d, f), lambda j: (layer, j, 0, 0))],
        out_specs=pl.BlockSpec((d, MOE_EG * f), lambda j: (0, j)),
        compiler_params=_cparams(("parallel",)),
        name="expert_cols",
    )(w)


def _cast_rows_kernel(w_ref, o_ref):
    o_ref[...] = w_ref[...].astype(BF16)


def _cast_rows(w, layer, rows):
    _, r, c = w.shape
    return pl.pallas_call(
        _cast_rows_kernel,
        out_shape=jax.ShapeDtypeStruct((r, c), BF16),
        grid=(r // rows,),
        in_specs=[pl.BlockSpec((None, rows, c), lambda j: (layer, j, 0))],
        out_specs=pl.BlockSpec((rows, c), lambda j: (j, 0)),
        compiler_params=_cparams(("parallel",)),
        name="cast_rows",
    )(w)


def _moe(h2, x1, mods_l, gates_t, w_gate, w_up, w_down, layer, p, norm_f, final, n_ctx_tok, lat_seq):
    n_tok = h2.shape[0]
    tm = MOE_TILE
    n_ctx_tiles = n_ctx_tok // tm
    tps = lat_seq // tm
    neg = N_EXPERTS // MOE_EG
    ef = MOE_EG * EXPERT_FF
    wg = _expert_cols(w_gate, layer)
    wu = _expert_cols(w_up, layer)
    wd = _cast_rows(w_down.reshape(w_down.shape[0], N_EXPERTS * EXPERT_FF, D_MODEL), layer, 1024)
    row = lambda i, j: (i, 0)
    const = lambda i, j: (0, 0)
    if final:
        out_shape = [jax.ShapeDtypeStruct((n_ctx_tok, D_MODEL), F32),
                     jax.ShapeDtypeStruct((n_tok - n_ctx_tok, D_MODEL), F32)]
        out_specs = [pl.BlockSpec((tm, D_MODEL), lambda i, j: (jnp.minimum(i, n_ctx_tiles - 1), 0)),
                     pl.BlockSpec((tm, D_MODEL), lambda i, j: (jnp.maximum(i - n_ctx_tiles, 0), 0))]
    else:
        out_shape = jax.ShapeDtypeStruct((n_tok, D_MODEL), F32)
        out_specs = pl.BlockSpec((tm, D_MODEL), row)
    out = pl.pallas_call(
        functools.partial(_moe_kernel, final=final, n_ctx_tiles=n_ctx_tiles),
        out_shape=out_shape,
        grid=(n_tok // tm, neg),
        in_specs=[pl.BlockSpec((tm, D_MODEL), row),
                  pl.BlockSpec((tm, D_MODEL), row),
                  pl.BlockSpec((1, 1, N_MOD * D_MODEL),
                               lambda i, j: (_mod_row_index(i, n_ctx_tiles, tps), 0, 0)),
                  pl.BlockSpec((1, tm, MOE_EG), lambda i, j: (j, i, 0)),
                  pl.BlockSpec((D_MODEL, ef), lambda i, j: (0, j)),
                  pl.BlockSpec((D_MODEL, ef), lambda i, j: (0, j)),
                  pl.BlockSpec((ef, D_MODEL), lambda i, j: (j, 0)),
                  pl.BlockSpec((D_MODEL, SHARED_FF), const),
                  pl.BlockSpec((D_MODEL, SHARED_FF), const),
                  pl.BlockSpec((SHARED_FF, D_MODEL), const),
                  pl.BlockSpec((1, D_MODEL), const)],
        out_specs=out_specs,
        scratch_shapes=[pltpu.VMEM((tm, D_MODEL), F32)],
        compiler_params=_cparams(("arbitrary", "arbitrary"), VMEM_LIMIT_MAX if final else VMEM_LIMIT),
        name="moe",
    )(h2, x1, mods_l, gates_t, wg, wu, wd,
      p['moe_ws_gate'].astype(BF16), p['moe_ws_up'].astype(BF16), p['moe_ws_down'].astype(BF16),
      norm_f.reshape(1, D_MODEL))
    return out


def kernel(x_prompt, x_sample, state_ssm, state_delta, c, c_ctx, w_ada, b_ada, norm1, norm2, w_in, w_out, ssm_lam_re, ssm_lam_im, ssm_log_dt, ssm_b_re, ssm_b_im, ssm_c_re, ssm_c_im, ssm_d, ssm_w_glu, ssm_b_glu, dn_conv, dn_a_log, dn_dt_bias, dn_norm, cm_norm, cm_w_s, cm_b_s, moe_router, moe_router_bias, moe_w_gate, moe_w_up, moe_w_down, moe_ws_gate, moe_ws_up, moe_ws_down, norm_f):
    n_ctx_seq, ctx_seq, _ = x_prompt.shape
    n_lat_seq, lat_seq, _ = x_sample.shape
    depth = w_ada.shape[0]
    n_ctx_tok = n_ctx_seq * ctx_seq
    n_seq = n_ctx_seq + n_lat_seq
    g = SSM_GROUPS

    xs = [x_prompt.reshape(n_ctx_tok, D_MODEL), x_sample.reshape(n_lat_seq * lat_seq, D_MODEL)]
    cvec = jnp.concatenate([c_ctx[None, :], c, jnp.zeros((8 - 1 - n_lat_seq, D_MODEL), F32)], axis=0)
    mods = _ada_mods(cvec, w_ada, b_ada)

    ssm_states, dn_states = [], []
    y = None
    for l in range(depth):
        p = {'w_out': w_out[l], 'norm2': norm2[l], 'ssm_w_glu': ssm_w_glu[l], 'ssm_b_glu': ssm_b_glu[l],
             'dn_conv': dn_conv[l], 'dn_a_log': dn_a_log[l], 'dn_dt_bias': dn_dt_bias[l],
             'dn_norm': dn_norm[l], 'cm_norm': cm_norm[l], 'cm_w_s': cm_w_s[l], 'cm_b_s': cm_b_s[l],
             'moe_ws_gate': moe_ws_gate[l], 'moe_ws_up': moe_ws_up[l], 'moe_ws_down': moe_ws_down[l]}
        mods_l = mods[l].reshape(8, 1, N_MOD * D_MODEL)
        s_u, z, cu, cv, qkvn, gcb, gcb_t = _in_projection(
            xs, mods_l, norm1[l], _cast_rows(w_in, l, 256), dn_conv[l], dn_a_log[l], dn_dt_bias[l],
            n_ctx_tok, ctx_seq, lat_seq)

        mats = _s5_prep(ssm_lam_re[l], ssm_lam_im[l], ssm_log_dt[l], ssm_b_re[l], ssm_b_im[l],
                        ssm_c_re[l], ssm_c_im[l])
        h0_lat = state_ssm[:, l].transpose(0, 3, 2, 1, 4).reshape(n_lat_seq, g, 256)
        h0 = jnp.concatenate([jnp.zeros((n_ctx_seq, g, 256), F32), h0_lat], axis=0)
        ys, ssm_fin = _s5_mixer(s_u, mats, ssm_d[l], h0, n_ctx_seq, ctx_seq, lat_seq)
        ssm_states.append(ssm_fin[:n_ctx_seq].reshape(n_ctx_seq, g, 2, 2, SSM_STATE).transpose(0, 3, 2, 1, 4))

        o_ctx, o_lat, dn_fin = _deltanet(qkvn, gcb, gcb_t, state_delta[:, l], n_ctx_seq, ctx_seq, n_lat_seq,
                                         lat_seq)
        dn_states.append(dn_fin)

        x1, h2 = _mix_out(xs, mods_l, ys, o_ctx, o_lat, z, cu, cv, p, n_ctx_tok, lat_seq)
        gates_t = _router(h2, moe_router[l], moe_router_bias[l])
        x = _moe(h2, x1, mods_l, gates_t, moe_w_gate, moe_w_up, moe_w_down, l, p, norm_f, l == depth - 1,
                 n_ctx_tok, lat_seq)
        xs = list(x) if isinstance(x, (list, tuple)) else [x]

    y_prompt = xs[0].reshape(x_prompt.shape)
    y_sample = xs[1].reshape(x_sample.shape)
    new_state_ssm = jnp.stack(ssm_states, axis=1)
    new_state_delta = jnp.stack(dn_states, axis=1)
    return (y_prompt, y_sample, new_state_ssm, new_state_delta)
```

```python
import functools
import math

import jax
import jax.numpy as jnp
from jax import lax
from jax.experimental import pallas as pl
from jax.experimental.pallas import tpu as pltpu

F32 = jnp.float32
BF16 = jnp.bfloat16
HIGHEST = lax.Precision.HIGHEST

D_MODEL = 1024
N_MOD = 6
EPS = 1e-6
SSM_CH = 256
SSM_GROUPS = 16
SSM_GROUP_CH = 16
SSM_STATE = 64
S5_T = 16
S5_W = S5_T * SSM_GROUP_CH
DN_HEADS = 4
DN_HEAD_DIM = 128
DN_WIDTH = 512
DN_CHUNK = 64
CM_HEADS = 4
CM_WIDTH = 256
CM_HEAD_DIM = 64
CM_CHUNK = 128
N_EXPERTS = 64
TOP_K = 6
N_ROUTE_GROUPS = 8
TOPK_ROUTE_GROUPS = 4
EXPERT_FF = 128
SHARED_FF = 256
ROUTED_SCALE = 2.5
MAIN_COLS = SSM_CH + 3 * DN_WIDTH + DN_WIDTH + 2 * CM_WIDTH
VMEM_LIMIT = 56 * 1024 * 1024
VMEM_LIMIT_MAX = 60 * 1024 * 1024


def _cparams(sem, vmem=None):
    return pltpu.CompilerParams(dimension_semantics=sem, vmem_limit_bytes=vmem)


def _dot(a, b):
    return jnp.dot(a, b, preferred_element_type=F32)


def _dot_nt(a, b, precision=None):
    return lax.dot_general(a, b, (((1,), (1,)), ((), ())), preferred_element_type=F32,
                           precision=precision)


def _dot_tn(a, b):
    return lax.dot_general(a, b, (((0,), (0,)), ((), ())), preferred_element_type=F32)


def _sigmoid(x):
    return 1.0 / (1.0 + jnp.exp(-x))


def _silu(x):
    return x * _sigmoid(x)


def _gelu(x):
    c = math.sqrt(2.0 / math.pi)
    return 0.5 * x * (1.0 + jnp.tanh(c * (x + 0.044715 * (x * x * x))))


def _softplus(x):
    return jnp.maximum(x, 0.0) + jnp.log(1.0 + jnp.exp(-jnp.abs(x)))


def _rms(x):
    return x * lax.rsqrt(jnp.mean(x * x, axis=-1, keepdims=True) + EPS)


def _mod_row_index(tile, n_ctx_tiles, tiles_per_lat_seq):
    return jnp.where(tile < n_ctx_tiles, 0, 1 + (tile - n_ctx_tiles) // tiles_per_lat_seq)


def _ada_kernel(c_ref, w_ref, b_ref, o_ref):
    h = _silu(c_ref[...])
    o_ref[0] = jnp.dot(h, w_ref[0], preferred_element_type=F32, precision=HIGHEST) + b_ref[0]


def _ada_mods(cvec, w_ada, b_ada):
    depth, d, n = w_ada.shape
    bn = 1536
    return pl.pallas_call(
        _ada_kernel,
        out_shape=jax.ShapeDtypeStruct((depth, 8, n), F32),
        grid=(depth, n // bn),
        in_specs=[pl.BlockSpec((8, d), lambda l, j: (0, 0)),
                  pl.BlockSpec((1, d, bn), lambda l, j: (l, 0, j)),
                  pl.BlockSpec((1, 1, bn), lambda l, j: (l, 0, j))],
        out_specs=pl.BlockSpec((1, 8, bn), lambda l, j: (l, 0, j)),
        compiler_params=_cparams(("parallel", "parallel")),
        name="ada_mods",
    )(cvec, w_ada, b_ada.reshape(depth, 1, n))


def _token_specs(xs, tm, n_ctx_tiles, width):
    if len(xs) == 1:
        return [pl.BlockSpec((tm, width), lambda i, *_: (i, 0))]
    return [pl.BlockSpec((tm, width), lambda i, *_: (jnp.minimum(i, n_ctx_tiles - 1), 0)),
            pl.BlockSpec((tm, width), lambda i, *_: (jnp.maximum(i - n_ctx_tiles, 0), 0))]


def _token_tile(refs, n_ctx_tiles, tile=None):
    if len(refs) == 1:
        return refs[0][...]
    tile = pl.program_id(0) if tile is None else tile
    return jnp.where(tile < n_ctx_tiles, refs[0][...], refs[1][...])


IN_TILE = 1024
IN_PARTS = 4


def _inproj_kernel(*refs, n_x, n_ctx_tiles, tiles_per_lat_seq, ctx_seq):
    x_refs = refs[:n_x]
    (prev_ref, next_ref, mod_ref, g_ref, w_ref, cw_ref, alog_ref, dtb_ref,
     su_ref, z_ref, cu_ref, cv_ref, qkvn_ref, gcb_ref, gt_ref, scr) = refs[n_x:]
    mod = mod_ref[0]
    shift = mod[:, 0:D_MODEL]
    scale = mod[:, D_MODEL:2 * D_MODEL]
    gain = g_ref[...] * (1.0 + scale)

    c0 = SSM_CH
    c1 = c0 + 3 * DN_WIDTH
    c2 = c1 + DN_WIDTH
    n_ab = 4 * DN_HEADS
    def normed_f32(v):
        return _rms(v) * gain + shift

    hf = normed_f32(_token_tile(x_refs, n_ctx_tiles))
    n = hf.shape[0]
    rq = n // IN_PARTS
    w_qkv = w_ref[:, c0:c1]
    edges = [normed_f32(prev_ref[...]), normed_f32(next_ref[...])]
    for b in range(1, IN_PARTS):
        edges += [hf[b * rq - 8:b * rq], hf[b * rq:b * rq + 8]]
    x_edge = _dot(jnp.concatenate(edges, axis=0).astype(BF16), w_qkv)
    xs_, tails = [], []
    for r in range(IN_PARTS):
        rows = slice(r * rq, (r + 1) * rq)
        hb = hf[rows].astype(BF16)
        xs_.append(_dot(hb, w_qkv))
        tail = _dot(hb, w_ref[:, c2:])
        tails.append(tail)
        su = _dot(hb, w_ref[:, 0:c0])
        z_ref[rows, :] = _dot(hb, w_ref[:, c1:c2])
        cu_ref[rows, :] = tail[:, n_ab:n_ab + CM_WIDTH]
        cv_ref[rows, :] = tail[:, n_ab + CM_WIDTH:n_ab + 2 * CM_WIDTH]
        scr[0, rows, :] = su[:, 0:128]
        scr[1, rows, :] = su[:, 128:c0]
    n_rows = su_ref.shape[1]
    for half in range(2):
        for t in range(S5_T):
            su_ref[half, :, t * 128:(t + 1) * 128] = scr[half, pl.ds(t, n_rows, stride=S5_T), :]

    i = pl.program_id(0)
    is_ctx = i < n_ctx_tiles
    lat_pos = (i - n_ctx_tiles) % tiles_per_lat_seq
    has_prev = jnp.logical_not(is_ctx) & (lat_pos > 0)
    has_next = jnp.logical_not(is_ctx) & (lat_pos < tiles_per_lat_seq - 1)
    cw = cw_ref[...]
    lane = lax.broadcasted_iota(jnp.int32, (1, 128), 1)
    ii = lax.broadcasted_iota(jnp.int32, (DN_CHUNK, DN_CHUNK), 0)
    jj = lax.broadcasted_iota(jnp.int32, (DN_CHUNK, DN_CHUNK), 1)
    pre = jnp.where(jj <= ii, 1.0, 0.0).astype(F32)
    suf = jnp.where(jj >= ii, 1.0, 0.0).astype(F32)
    n_ch = rq // DN_CHUNK
    for r in range(IN_PARTS):
        rows = slice(r * rq, (r + 1) * rq)
        x = xs_[r]
        if r == 0:
            prev_row = jnp.where(has_prev, x_edge[7:8, :], 0.0)
        else:
            prev_row = x_edge[16 * r + 7:16 * r + 8, :]
        if r == IN_PARTS - 1:
            next_row = jnp.where(has_next, x_edge[8:9, :], 0.0)
        else:
            next_row = x_edge[16 * (r + 1) + 8:16 * (r + 1) + 9, :]
        local = lax.broadcasted_iota(jnp.int32, (rq, 1), 0)
        row = r * rq + local
        seq_first = is_ctx & (row % ctx_seq == 0)
        seq_last = is_ctx & (row % ctx_seq == ctx_seq - 1)
        xp = jnp.where(local == 0, prev_row, pltpu.roll(x, 1, 0))
        xn = jnp.where(local == rq - 1, next_row, pltpu.roll(x, rq - 1, 0))
        xp = jnp.where(seq_first, 0.0, xp)
        xn = jnp.where(seq_last, 0.0, xn)
        y = _silu(cw[0:1] * xp + cw[1:2] * x + cw[2:3] * xn)
        for hcol in range(2 * DN_HEADS):
            seg = y[:, hcol * DN_HEAD_DIM:(hcol + 1) * DN_HEAD_DIM]
            nrm = seg * lax.rsqrt(jnp.sum(seg * seg, axis=-1, keepdims=True) + EPS)
            if hcol < DN_HEADS:
                nrm = nrm * (DN_HEAD_DIM ** -0.5)
            qkvn_ref[rows, hcol * DN_HEAD_DIM:(hcol + 1) * DN_HEAD_DIM] = nrm.astype(BF16)
        qkvn_ref[rows, 2 * DN_WIDTH:] = y[:, 2 * DN_WIDTH:].astype(BF16)
        ab = tails[r][:, 0:128]
        glog = jnp.where(lane < 2 * DN_HEADS, -jnp.exp(alog_ref[...]) * _softplus(ab + dtb_ref[...]), 0.0)
        beta = _sigmoid(ab)
        wide = jnp.concatenate([glog[c * DN_CHUNK:(c + 1) * DN_CHUNK] for c in range(n_ch)], axis=1)
        gpre_w = jnp.dot(pre, wide, preferred_element_type=F32, precision=HIGHEST)
        gsuf_w = jnp.dot(suf, wide, preferred_element_type=F32, precision=HIGHEST)
        unwide = lambda a: jnp.concatenate([a[:, c * 128:(c + 1) * 128] for c in range(n_ch)], axis=0)
        gcb = jnp.where(lane < DN_HEADS, unwide(gpre_w),
                        jnp.where(lane < 2 * DN_HEADS, unwide(gsuf_w), beta))
        gcb_ref[rows, :] = gcb
        gcb_t = gcb.T
        for c in range(n_ch):
            gt_ref[r * n_ch + c] = gcb_t[0:16, c * DN_CHUNK:(c + 1) * DN_CHUNK]


def _in_projection(xs, mods_l, norm1, w_in_bf, conv_w, a_log, dt_bias, n_ctx_tok, ctx_seq, lat_seq):
    n_tok = sum(x.shape[0] for x in xs)
    tm = IN_TILE
    assert tm % ctx_seq == 0 and lat_seq % tm == 0
    n_ctx_tiles = n_ctx_tok // tm
    tps = lat_seq // tm
    row = lambda i: (i, 0)
    const = lambda i: (0, 0)
    in_cols = w_in_bf.shape[1]
    r8 = tm // 8
    halo_src = xs[-1]
    first = n_ctx_tiles if len(xs) == 2 else 0
    last8 = halo_src.shape[0] // 8 - 1
    prev_spec = pl.BlockSpec((8, D_MODEL), lambda i: (jnp.clip((i - first) * r8 - 1, 0, last8), 0))
    next_spec = pl.BlockSpec((8, D_MODEL), lambda i: (jnp.clip((i - first + 1) * r8, 0, last8), 0))
    w8 = jnp.concatenate([conv_w, jnp.zeros((5, 3 * DN_WIDTH), F32)], axis=0)
    pad = lambda v: jnp.concatenate([v.reshape(1, 2 * DN_HEADS), jnp.zeros((1, 120), F32)], axis=1)
    outs = [(DN_WIDTH, F32), (CM_WIDTH, F32), (CM_WIDTH, F32), (3 * DN_WIDTH, BF16), (128, F32)]
    return pl.pallas_call(
        functools.partial(_inproj_kernel, n_x=len(xs), n_ctx_tiles=n_ctx_tiles, tiles_per_lat_seq=tps,
                          ctx_seq=ctx_seq),
        out_shape=[jax.ShapeDtypeStruct((2, n_tok // S5_T, S5_T * 128), F32)]
        + [jax.ShapeDtypeStruct((n_tok, w), dt) for w, dt in outs]
        + [jax.ShapeDtypeStruct((n_tok // DN_CHUNK, 16, DN_CHUNK), F32)],
        grid=(n_tok // tm,),
        in_specs=_token_specs(xs, tm, n_ctx_tiles, D_MODEL)
        + [prev_spec, next_spec,
           pl.BlockSpec((1, 1, N_MOD * D_MODEL),
                        lambda i: (_mod_row_index(i, n_ctx_tiles, tps), 0, 0)),
           pl.BlockSpec((1, D_MODEL), const),
           pl.BlockSpec((D_MODEL, in_cols), const),
           pl.BlockSpec((8, 3 * DN_WIDTH), const),
           pl.BlockSpec((1, 128), const),
           pl.BlockSpec((1, 128), const)],
        out_specs=[pl.BlockSpec((2, tm // S5_T, S5_T * 128), lambda i: (0, i, 0))]
        + [pl.BlockSpec((tm, w), row) for w, _ in outs]
        + [pl.BlockSpec((tm // DN_CHUNK, 16, DN_CHUNK), lambda i: (i, 0, 0))],
        scratch_shapes=[pltpu.VMEM((2, tm, 128), F32)],
        compiler_params=_cparams(("parallel",), VMEM_LIMIT),
        name="in_projection",
    )(*xs, halo_src, halo_src, mods_l, norm1.reshape(1, D_MODEL), w_in_bf, w8, pad(a_log), pad(dt_bias))


def _s5_prep_kernel(lre_ref, lim_ref, ldt_ref, btre_ref, btim_ref, cre_ref, cim_ref,
                    mt_ref, pt_ref, qt_ref, at_ref):
    t_n, hg, w = S5_T, SSM_GROUP_CH, S5_W
    lre, lim = lre_ref[0], lim_ref[0]
    dt = jnp.exp(ldt_ref[0])
    xr, xi = lre * dt, lim * dt
    tau = lax.broadcasted_iota(jnp.int32, (t_n, 128), 0).astype(F32)
    mag = jnp.exp(tau * xr)
    a_re, a_im = mag * jnp.cos(tau * xi), mag * jnp.sin(tau * xi)
    ar, ai = jnp.exp(xr) * jnp.cos(xi), jnp.exp(xr) * jnp.sin(xi)
    inv = 1.0 / (lre * lre + lim * lim)
    br_, bi_ = lre * inv, -lim * inv
    fr = (ar - 1.0) * br_ - ai * bi_
    fi = (ar - 1.0) * bi_ + ai * br_
    btre, btim = btre_ref[0], btim_ref[0]
    bbr = fr * btre - fi * btim
    bbi = fr * btim + fi * btre
    cre, cim = cre_ref[0], cim_ref[0]

    lane = lax.broadcasted_iota(jnp.int32, (1, 128), 1)
    is_f = lane < SSM_STATE

    def rep_rows(a):
        return jnp.concatenate([jnp.broadcast_to(a[t:t + 1], (hg, 128)) for t in range(t_n)], axis=0)

    def tile_rows(c):
        return jnp.concatenate([c] * t_n, axis=0)

    def rev_rows(a):
        return jnp.concatenate([a[t_n - 1 - t:t_n - t] for t in range(t_n)], axis=0)

    cr_t, ci_t = tile_rows(cre), tile_rows(cim)
    a_re_rep, a_im_rep = rep_rows(a_re), rep_rows(a_im)
    w_re = cr_t * a_re_rep - ci_t * a_im_rep
    w_im = cr_t * a_im_rep + ci_t * a_re_rep
    zero = jnp.zeros_like(w_re)
    e_f = jnp.concatenate([jnp.where(is_f, w_re, zero), jnp.where(is_f, -w_im, zero)], axis=1)
    e_b = jnp.concatenate([jnp.where(is_f, zero, w_re), jnp.where(is_f, zero, -w_im)], axis=1)
    e_b_rev = jnp.concatenate([e_b[(t_n - 1 - t) * hg:(t_n - t) * hg] for t in range(t_n)], axis=0)
    bb_t = tile_rows(jnp.concatenate([bbr, bbi], axis=1))
    row_blk = lax.broadcasted_iota(jnp.int32, (w, w), 0) // hg

    m = jnp.zeros((w, w), F32)
    zrows = lambda n: jnp.zeros((n * hg, w), F32)
    for s in range(t_n):
        down = e_f if s == 0 else jnp.concatenate([zrows(s), e_f[:(t_n - s) * hg]], axis=0)
        k = t_n - 1 - s
        up = e_b_rev if k == 0 else jnp.concatenate([e_b_rev[k * hg:], zrows(k)], axis=0)
        rhs = jnp.where(row_blk == s, bb_t, 0.0)
        m = m + _dot_nt((down + up).astype(BF16), rhs.astype(BF16))
    mt_ref[...] = m.T.reshape(t_n, hg, w).astype(BF16)

    a_re_rev, a_im_rev = rev_rows(a_re), rev_rows(a_im)
    ps_re = rep_rows(jnp.where(is_f, a_re_rev, a_re))
    ps_im = rep_rows(jnp.where(is_f, a_im_rev, a_im))
    bbr_t, bbi_t = tile_rows(bbr), tile_rows(bbi)
    p_re = ps_re * bbr_t - ps_im * bbi_t
    p_im = ps_re * bbi_t + ps_im * bbr_t
    pt_ref[...] = jnp.concatenate([p_re, p_im], axis=1).reshape(t_n, hg, w).astype(BF16)

    a1_re = a_re * ar - a_im * ai
    a1_im = a_re * ai + a_im * ar
    aq_re = rep_rows(jnp.where(is_f, a1_re, rev_rows(a1_re)))
    aq_im = rep_rows(jnp.where(is_f, a1_im, rev_rows(a1_im)))
    q_re = cr_t * aq_re - ci_t * aq_im
    q_im = cr_t * aq_im + ci_t * aq_re
    qt_ref[0] = jnp.concatenate([q_re, -q_im], axis=1).T.astype(BF16)
    at_ref[0] = jnp.concatenate([a1_re[t_n - 1:t_n], a1_im[t_n - 1:t_n]], axis=1)


def _s5_prep(lam_re, lam_im, log_dt, b_re, b_im, c_re, c_im):
    g = SSM_GROUPS
    cat = lambda a: jnp.concatenate([a[0], a[1]], axis=-1)
    lre = cat(lam_re)[:, None, :]
    lim = cat(lam_im)[:, None, :]
    ldt = cat(jnp.broadcast_to(log_dt[..., None], (2, g, SSM_STATE)))[:, None, :]
    btre = cat(jnp.swapaxes(b_re, -1, -2))
    btim = cat(jnp.swapaxes(b_im, -1, -2))
    cre, cim = cat(c_re), cat(c_im)
    vec = pl.BlockSpec((1, 1, 128), lambda i: (i, 0, 0))
    mat = pl.BlockSpec((1, SSM_GROUP_CH, 128), lambda i: (i, 0, 0))
    big = pl.BlockSpec((1, S5_W, S5_W), lambda i: (i, 0, 0))
    gh = g // 2
    arranged = pl.BlockSpec((None, S5_T, None, SSM_GROUP_CH, S5_W), lambda i: (i // gh, 0, i % gh, 0, 0))
    arranged_shape = jax.ShapeDtypeStruct((2, S5_T, gh, SSM_GROUP_CH, S5_W), BF16)
    return pl.pallas_call(
        _s5_prep_kernel,
        out_shape=[arranged_shape, arranged_shape, jax.ShapeDtypeStruct((g, S5_W, S5_W), BF16),
                   jax.ShapeDtypeStruct((g, 1, 256), F32)],
        grid=(g,),
        in_specs=[vec, vec, vec, mat, mat, mat, mat],
        out_specs=[arranged, arranged, big, pl.BlockSpec((1, 1, 256), lambda i: (i, 0, 0))],
        compiler_params=_cparams(("parallel",)),
        name="s5_prep",
    )(lre, lim, ldt, btre, btim, cre, cim)


S5_SPREAD_ROWS = 512


def _s5_spread_kernel(a_ref, o_ref, *, rows_per_group, gh):
    hg, w = SSM_GROUP_CH, S5_W
    n_rows, n_cols = o_ref.shape
    src = lax.broadcasted_iota(jnp.int32, (w, n_cols), 0)
    dst = lax.broadcasted_iota(jnp.int32, (w, n_cols), 1)
    tile = jnp.where((src // hg == dst // 128) & (src % hg == dst % hg), 1.0, 0.0).astype(BF16)
    row = pl.program_id(1) * n_rows + lax.broadcasted_iota(jnp.int32, (n_rows, n_cols), 0)
    col = lax.broadcasted_iota(jnp.int32, (n_rows, n_cols), 1)
    keep = (row // rows_per_group) % gh == (col // hg) % gh
    o_ref[...] = jnp.where(keep, _dot(a_ref[...], tile), 0.0).astype(BF16)


def _s5_spread_cols(a, rows_per_group, gh):
    _, n_rows, w = a.shape
    rb = S5_SPREAD_ROWS
    return pl.pallas_call(
        functools.partial(_s5_spread_kernel, rows_per_group=rows_per_group, gh=gh),
        out_shape=jax.ShapeDtypeStruct((2, n_rows, S5_T * 128), BF16),
        grid=(2, n_rows // rb),
        in_specs=[pl.BlockSpec((None, rb, w), lambda z, i: (z, i, 0))],
        out_specs=pl.BlockSpec((None, rb, S5_T * 128), lambda z, i: (z, i, 0)),
        compiler_params=_cparams(("parallel", "parallel")),
        name="s5_spread_cols",
    )(a)


def _s5_place_kernel(a_ref, o_ref, *, gh):
    a = a_ref[...]
    n_rows, w = a.shape
    grp = ((pl.program_id(1) * n_rows + lax.broadcasted_iota(jnp.int32, (n_rows, w), 0))
           // SSM_GROUP_CH) % gh
    for k in range(gh):
        o_ref[:, k * w:(k + 1) * w] = jnp.where(grp == k, a, jnp.zeros_like(a))


def _s5_place_cols(a, gh):
    _, n_rows, w = a.shape
    rb = S5_SPREAD_ROWS
    return pl.pallas_call(
        functools.partial(_s5_place_kernel, gh=gh),
        out_shape=jax.ShapeDtypeStruct((2, n_rows, gh * w), BF16),
        grid=(2, n_rows // rb),
        in_specs=[pl.BlockSpec((None, rb, w), lambda z, i: (z, i, 0))],
        out_specs=pl.BlockSpec((None, rb, gh * w), lambda z, i: (z, i, 0)),
        compiler_params=_cparams(("parallel", "parallel")),
        name="s5_place_cols",
    )(a)


def _s5_z_kernel(u_ref, pt_ref, z_ref):
    z_ref[0] = _dot(u_ref[0].astype(BF16), pt_ref[0])


def _s5_rec_kernel(z_ref, a_ref, h0_ref, hf_ref, hb_ref, fin_ref, *, n_ctx_seq, ctx_chunks, n_lat_seq,
                   lat_chunks):
    sw = z_ref.shape[-1]
    ng = sw // 256
    lane = lax.broadcasted_iota(jnp.int32, (1, sw), 1)
    is_f = (lane % 128) < SSM_STATE

    def split(row):
        re = jnp.concatenate([row[:, g * 256:g * 256 + 128] for g in range(ng)], axis=1)
        im = jnp.concatenate([row[:, g * 256 + 128:(g + 1) * 256] for g in range(ng)], axis=1)
        return re, im

    def merge(re, im):
        parts = []
        for g in range(ng):
            parts += [re[:, g * 128:(g + 1) * 128], im[:, g * 128:(g + 1) * 128]]
        return jnp.concatenate(parts, axis=1)

    a_re, a_im = split(a_ref[0])

    def run_seq(base, n, seq):
        def step(k, h):
            h_re, h_im = h
            cf = base + k
            cb = base + n - 1 - k
            row = merge(h_re, h_im)
            hf_ref[0, pl.ds(cf, 1), :] = row
            hb_ref[0, pl.ds(cb, 1), :] = row
            z_re, z_im = split(jnp.where(is_f, z_ref[0, pl.ds(cf, 1), :], z_ref[0, pl.ds(cb, 1), :]))
            return (a_re * h_re - a_im * h_im + z_re, a_re * h_im + a_im * h_re + z_im)
        h_re, h_im = lax.fori_loop(0, n, step, split(h0_ref[0, pl.ds(seq, 1), :]))
        fin_ref[0, pl.ds(seq, 1), :] = merge(h_re, h_im)

    def ctx_body(s, carry):
        run_seq(s * ctx_chunks, ctx_chunks, s)
        return carry
    lax.fori_loop(0, n_ctx_seq, ctx_body, 0)

    def lat_body(s, carry):
        run_seq(n_ctx_seq * ctx_chunks + s * lat_chunks, lat_chunks, n_ctx_seq + s)
        return carry
    lax.fori_loop(0, n_lat_seq, lat_body, 0)


def _s5_y_kernel(u_ref, hf_ref, hb_ref, mt_ref, qt_ref, d_ref, y_ref):
    u = u_ref[0]
    lane = lax.broadcasted_iota(jnp.int32, (1, u.shape[-1]), 1)
    is_f = (lane % 128) < SSM_STATE
    h = jnp.where(is_f, hf_ref[0], hb_ref[0]).astype(BF16)
    y_ref[0] = _dot(u.astype(BF16), mt_ref[0]) + _dot(h, qt_ref[0]) + d_ref[0] * u


def _s5_mixer(s_u, mats, ssm_d, h0, n_ctx_seq, ctx_seq, lat_seq):
    mt, pt, qt, at = mats
    g, t_n, w, hg = SSM_GROUPS, S5_T, S5_W, SSM_GROUP_CH
    gh = g // 2
    nc = s_u.shape[1]
    bw = t_n * 128
    sw = gh * w
    n_seq = h0.shape[0]
    ctx_chunks, lat_chunks = ctx_seq // t_n, lat_seq // t_n
    n_lat_seq = n_seq - n_ctx_seq
    u = s_u
    m_big = _s5_spread_cols(mt.reshape(2, bw, w), hg, gh)
    q_big = _s5_spread_cols(qt.reshape(2, sw, w), w, gh)
    p_big = _s5_place_cols(pt.reshape(2, bw, w), gh)
    d_big = jnp.tile(ssm_d.reshape(2, 1, 128), (1, 1, t_n))

    rb = math.gcd(nc, 256)
    u_spec = pl.BlockSpec((1, rb, bw), lambda z, i: (z, i, 0))
    s_spec = pl.BlockSpec((1, rb, sw), lambda z, i: (z, i, 0))
    z = pl.pallas_call(
        _s5_z_kernel,
        out_shape=jax.ShapeDtypeStruct((2, nc, sw), F32),
        grid=(2, nc // rb),
        in_specs=[u_spec, pl.BlockSpec((1, bw, sw), lambda z, i: (z, 0, 0))],
        out_specs=s_spec,
        compiler_params=_cparams(("parallel", "parallel"), VMEM_LIMIT),
        name="s5_chunk_inputs",
    )(u, p_big)

    half = lambda rows: pl.BlockSpec((1, rows, sw), lambda z: (z, 0, 0))
    hf, hb, fin = pl.pallas_call(
        functools.partial(_s5_rec_kernel, n_ctx_seq=n_ctx_seq, ctx_chunks=ctx_chunks,
                          n_lat_seq=n_lat_seq, lat_chunks=lat_chunks),
        out_shape=[jax.ShapeDtypeStruct((2, nc, sw), F32)] * 2 + [jax.ShapeDtypeStruct((2, n_seq, sw), F32)],
        grid=(2,),
        in_specs=[half(nc), half(1), half(n_seq)],
        out_specs=[half(nc), half(nc), half(n_seq)],
        compiler_params=_cparams(("parallel",), VMEM_LIMIT),
        name="s5_recurrence",
    )(z, at.reshape(2, 1, sw), h0.reshape(n_seq, 2, sw).transpose(1, 0, 2))

    y = pl.pallas_call(
        _s5_y_kernel,
        out_shape=jax.ShapeDtypeStruct((2, nc, bw), F32),
        grid=(2, nc // rb),
        in_specs=[u_spec, s_spec, s_spec,
                  pl.BlockSpec((1, bw, bw), lambda z, i: (z, 0, 0)),
                  pl.BlockSpec((1, sw, bw), lambda z, i: (z, 0, 0)),
                  pl.BlockSpec((1, 1, bw), lambda z, i: (z, 0, 0))],
        out_specs=u_spec,
        compiler_params=_cparams(("parallel", "parallel"), VMEM_LIMIT),
        name="s5_outputs",
    )(u, hf, hb, m_big, q_big, d_big)
    return y, fin.transpose(1, 0, 2).reshape(n_seq, g, w)


DN_CHAINS = 2 * DN_HEADS


def _split(x):
    hi = x.astype(BF16)
    lo = (x - hi.astype(F32)).astype(BF16)
    return hi, lo


def _split_lhs(a):
    hi, lo = _split(a)
    hif = hi.astype(F32)
    return jnp.concatenate([hif, hif, lo.astype(F32)], axis=1).astype(BF16)


def _split_rhs(b):
    hi, lo = _split(b)
    return jnp.concatenate([hi, lo, hi], axis=0)


def _dn_prepare(q_ref, k_ref, v_ref, gc_ref, gt_ref, row0, chunk_idx):
    c = DN_CHUNK
    ii = lax.broadcasted_iota(jnp.int32, (c, c), 0)
    jj = lax.broadcasted_iota(jnp.int32, (c, c), 1)
    items = []
    for d in range(2):
        rows = pl.ds(row0[d], c)
        gcol = gc_ref[rows, :]
        grow = gt_ref[chunk_idx[d]]
        incl = (ii >= jj) if d == 0 else (ii <= jj)
        strict = (ii > jj) if d == 0 else (ii < jj)
        for h in range(DN_HEADS):
            ch = d * DN_HEADS + h
            cols = slice(h * DN_HEAD_DIM, (h + 1) * DN_HEAD_DIM)
            q, k, v = q_ref[rows, cols], k_ref[rows, cols], v_ref[rows, cols]
            gc_col = gcol[:, ch:ch + 1]
            beta = gcol[:, DN_CHAINS + ch:DN_CHAINS + ch + 1]
            gc_row = grow[ch:ch + 1, :]
            kf = k.astype(F32)
            kb = kf * beta
            decay = jnp.exp(jnp.where(incl, gc_col - gc_row, -jnp.inf))
            egc = jnp.exp(gc_col)
            g_last = gc_col[c - 1:c] if d == 0 else gc_col[0:1]
            items.append(dict(
                q=q, k=k, kb=kb.astype(BF16), decay=decay, incl=incl, strict=strict,
                rhs=jnp.concatenate([kb * egc, v.astype(F32) * beta], axis=1),
                qg=(q.astype(F32) * egc).astype(BF16),
                kt=(kf * jnp.exp(g_last - gc_col)).astype(BF16),
                gl=jnp.broadcast_to(jnp.exp(g_last), (8, 128))))
    for it in items:
        it['a'] = jnp.where(it['strict'], _dot_nt(it['kb'], it['k']) * it['decay'], 0.0)
        it['qk'] = jnp.where(it['incl'], _dot_nt(it['q'], it['k']) * it['decay'], 0.0).astype(BF16)
    power = 1
    while power < c:
        for it in items:
            it['lhs'] = _split_lhs(it['a'])
            if 2 * power < c:
                it['a'] = _dot(it['lhs'], _split_rhs(it['a']))
        for it in items:
            upd = _dot(it['lhs'], _split_rhs(it['rhs'] if power == 1 else it['r']))
            it['r'] = it['rhs'] - upd if power == 1 else it['r'] + upd
        power *= 2
    return items


def _dn_scan_kernel(*refs, n_chunks, seq_len, seqs_per_blk, has_s0, write_fin):
    refs = list(refs)
    q_ref, k_ref, v_ref, gc_ref, gt_ref = refs[:5]
    pos = 5
    s0_ref = None
    if has_s0:
        s0_ref = refs[pos]
        pos += 1
    o_ref = refs[pos]
    pos += 1
    fin_ref = None
    if write_fin:
        fin_ref = refs[pos]
        pos += 1
    s_ref, w_s, u_s, qk_s, qg_s, kt_s, gl_s = refs[pos:pos + 7]
    n_steps = seqs_per_blk * n_chunks

    o_ref[...] = jnp.zeros_like(o_ref)
    for r in (s_ref, w_s, u_s, qk_s, qg_s, kt_s, gl_s):
        r[...] = jnp.zeros_like(r)

    def locate(step):
        seq = step // n_chunks
        kk = step % n_chunks
        cf = seq * n_chunks + kk
        cb = seq * n_chunks + (n_chunks - 1 - kk)
        rows = [pl.multiple_of(cf * DN_CHUNK, DN_CHUNK), pl.multiple_of(cb * DN_CHUNK, DN_CHUNK)]
        return seq, kk, rows, [cf, cb]

    def body(t, carry):
        tb = jnp.maximum(t, 0)
        valid = t >= 0
        seq, kk, rows_b, _ = locate(tb)
        first = kk == 0
        vnews = []
        for ch in range(DN_CHAINS):
            d, h = divmod(ch, DN_HEADS)
            s0 = s0_ref[0, d, h] if has_s0 else jnp.zeros((DN_HEAD_DIM, DN_HEAD_DIM), F32)
            s_ref[ch] = jnp.where(first, s0, s_ref[ch])
        for ch in range(DN_CHAINS):
            vnews.append((u_s[ch] - _dot(w_s[ch], s_ref[ch].astype(BF16))).astype(BF16))
        tn = jnp.minimum(t + 1, n_steps - 1)
        _, _, rows_a, chunks_a = locate(tn)
        items = _dn_prepare(q_ref, k_ref, v_ref, gc_ref, gt_ref, rows_a, chunks_a)
        for ch in range(DN_CHAINS):
            d, h = divmod(ch, DN_HEADS)
            state = s_ref[ch]
            o = _dot(qg_s[ch], state.astype(BF16)) + _dot(qk_s[ch], vnews[ch])
            s_new = state * gl_s[ch][0:1, 0:1] + _dot_tn(kt_s[ch], vnews[ch])
            s_ref[ch] = s_new
            if write_fin:
                fin_ref[seq, d, h] = s_new
            rows = pl.ds(rows_b[d], DN_CHUNK)
            cols = slice(h * DN_HEAD_DIM, (h + 1) * DN_HEAD_DIM)
            o_ref[rows, cols] = o_ref[rows, cols] + jnp.where(valid, o, 0.0)
        for ch, it in enumerate(items):
            w_s[ch] = it['r'][:, :DN_HEAD_DIM].astype(BF16)
            u_s[ch] = it['r'][:, DN_HEAD_DIM:]
            qk_s[ch] = it['qk']
            qg_s[ch] = it['qg']
            kt_s[ch] = it['kt']
            gl_s[ch] = it['gl']
        return carry

    lax.fori_loop(-1, n_steps, body, 0)


def _dn_scan(qkvn, gcb, gcb_t, seq_len, n_seq, seqs_per_blk, first_tok, s0, write_fin):
    n_chunks = seq_len // DN_CHUNK
    rows_blk = seqs_per_blk * seq_len
    b0 = first_tok // rows_blk
    hd = DN_HEAD_DIM
    st_spec = pl.BlockSpec((seqs_per_blk, 2, DN_HEADS, hd, hd), lambda s: (s, 0, 0, 0, 0))
    in_specs = [pl.BlockSpec((rows_blk, DN_WIDTH), lambda s: (b0 + s, 0)),
                pl.BlockSpec((rows_blk, DN_WIDTH), lambda s: (b0 + s, 1)),
                pl.BlockSpec((rows_blk, DN_WIDTH), lambda s: (b0 + s, 2)),
                pl.BlockSpec((rows_blk, 128), lambda s: (b0 + s, 0)),
                pl.BlockSpec((rows_blk // DN_CHUNK, 16, DN_CHUNK), lambda s: (b0 + s, 0, 0))]
    args = [qkvn, qkvn, qkvn, gcb, gcb_t]
    if s0 is not None:
        assert seqs_per_blk == 1
        in_specs.append(st_spec)
        args.append(s0)
    out_shape = [jax.ShapeDtypeStruct((n_seq * seq_len, DN_WIDTH), F32)]
    out_specs = [pl.BlockSpec((rows_blk, DN_WIDTH), lambda s: (s, 0))]
    if write_fin:
        out_shape.append(jax.ShapeDtypeStruct((n_seq, 2, DN_HEADS, hd, hd), F32))
        out_specs.append(st_spec)
    c = DN_CHUNK
    scratch = [pltpu.VMEM((DN_CHAINS, hd, hd), F32),
               pltpu.VMEM((DN_CHAINS, c, hd), BF16),
               pltpu.VMEM((DN_CHAINS, c, hd), F32),
               pltpu.VMEM((DN_CHAINS, c, c), BF16),
               pltpu.VMEM((DN_CHAINS, c, hd), BF16),
               pltpu.VMEM((DN_CHAINS, c, hd), BF16),
               pltpu.VMEM((DN_CHAINS, 8, 128), F32)]
    return pl.pallas_call(
        functools.partial(_dn_scan_kernel, n_chunks=n_chunks, seq_len=seq_len, seqs_per_blk=seqs_per_blk,
                          has_s0=s0 is not None, write_fin=write_fin),
        out_shape=out_shape,
        grid=(n_seq // seqs_per_blk,),
        in_specs=in_specs,
        out_specs=out_specs,
        scratch_shapes=scratch,
        compiler_params=_cparams(("parallel",), VMEM_LIMIT),
        name="dn_scan_%d" % seq_len,
    )(*args)


def _deltanet(qkvn, gcb, gcb_t, s0_lat, n_ctx_seq, ctx_seq, n_lat_seq, lat_seq):
    n_ctx_tok = n_ctx_seq * ctx_seq
    ctx_blk = math.gcd(n_ctx_seq, 8)
    o_ctx, fin = _dn_scan(qkvn, gcb, gcb_t, ctx_seq, n_ctx_seq, ctx_blk, 0, None, True)
    (o_lat,) = _dn_scan(qkvn, gcb, gcb_t, lat_seq, n_lat_seq, 1, n_ctx_tok, s0_lat, False)
    return o_ctx, o_lat, fin


MIX_TILE = 256


def _mix_kernel(*refs, n_x, n_ctx_tiles):
    x_refs = refs[:n_x]
    (mod_ref, ys_ref, octx_ref, olat_ref, z_ref, cu_ref, cv_ref,
     wglu_ref, bglu_ref, dnn_ref, cmn_ref, ws_ref, bs_ref, wo_ref, n2_ref,
     x1_ref, h2_ref, scr) = refs[n_x:]
    mod = mod_ref[0]
    gate1 = mod[:, 2 * D_MODEL:3 * D_MODEL]
    shift2 = mod[:, 3 * D_MODEL:4 * D_MODEL]
    scale2 = mod[:, 4 * D_MODEL:5 * D_MODEL]
    n_rows = ys_ref.shape[1]
    for half in range(2):
        for t in range(S5_T):
            scr[half, pl.ds(t, n_rows, stride=S5_T), :] = ys_ref[half, :, t * 128:(t + 1) * 128]
    ya = _gelu(jnp.concatenate([scr[0], scr[1]], axis=1))
    ya = ya * _sigmoid(_dot(ya.astype(BF16), wglu_ref[...]) + bglu_ref[...])
    mix = _dot(ya.astype(BF16), wo_ref[0:SSM_CH, :])
    o = jnp.where(pl.program_id(0) < n_ctx_tiles, octx_ref[...], olat_ref[...])
    z = z_ref[...]
    parts = []
    for h in range(DN_HEADS):
        sl = slice(h * DN_HEAD_DIM, (h + 1) * DN_HEAD_DIM)
        parts.append(_rms(o[:, sl]) * dnn_ref[...] * _silu(z[:, sl]))
    yb = jnp.concatenate(parts, axis=1)
    mix = mix + _dot(yb.astype(BF16), wo_ref[SSM_CH:SSM_CH + DN_WIDTH, :])
    u = _gelu(cu_ref[...])
    v = _gelu(cv_ref[...])
    seg_i = lax.broadcasted_iota(jnp.int32, (CM_WIDTH, CM_WIDTH), 0) // CM_HEAD_DIM
    seg_j = lax.broadcasted_iota(jnp.int32, (CM_WIDTH, CM_WIDTH), 1) // CM_HEAD_DIM
    seg_ones = jnp.where(seg_i == seg_j, 1.0, 0.0).astype(BF16)
    vv = v * v
    vv_hi = vv.astype(BF16)
    vv_lo = (vv - vv_hi.astype(F32)).astype(BF16)
    ms = (_dot(vv_hi, seg_ones) + _dot(vv_lo, seg_ones)) * (1.0 / CM_HEAD_DIM)
    vn = (v * lax.rsqrt(ms + EPS) * cmn_ref[...]).astype(BF16)
    head_of_lane = lax.broadcasted_iota(jnp.int32, (1, CM_WIDTH), 1) // CM_HEAD_DIM
    ycs = []
    for c in range(MIX_TILE // CM_CHUNK):
        vc = vn[c * CM_CHUNK:(c + 1) * CM_CHUNK]
        stack = jnp.concatenate([jnp.where(head_of_lane == h, vc, jnp.zeros_like(vc))
                                 for h in range(CM_HEADS)], axis=0)
        vs = _dot(ws_ref[...], stack) + bs_ref[...]
        ycs.append(u[c * CM_CHUNK:(c + 1) * CM_CHUNK] * vs)
    yc = jnp.concatenate(ycs, axis=0)
    mix = mix + _dot(yc.astype(BF16), wo_ref[SSM_CH + DN_WIDTH:, :])
    x1 = _token_tile(x_refs, n_ctx_tiles) + gate1 * mix
    x1_ref[...] = x1
    h2_ref[...] = (_rms(x1) * n2_ref[...] * (1.0 + scale2) + shift2).astype(BF16)


def _mix_out(xs, mods_l, ys, o_ctx, o_lat, z, cu, cv, p, n_ctx_tok, lat_seq):
    n_tok = sum(x.shape[0] for x in xs)
    tm = MIX_TILE
    n_ctx_tiles = n_ctx_tok // tm
    tps = lat_seq // tm
    row = lambda i: (i, 0)
    const = lambda i: (0, 0)
    ctx_row = lambda i: (jnp.minimum(i, n_ctx_tiles - 1), 0)
    lat_row = lambda i: (jnp.maximum(i - n_ctx_tiles, 0), 0)
    ws_cat = jnp.concatenate([p['cm_w_s'][h] for h in range(CM_HEADS)], axis=1).astype(BF16)
    bs_exp = jnp.repeat(p['cm_b_s'].T, CM_HEAD_DIM, axis=1)
    return pl.pallas_call(
        functools.partial(_mix_kernel, n_x=len(xs), n_ctx_tiles=n_ctx_tiles),
        out_shape=[jax.ShapeDtypeStruct((n_tok, D_MODEL), F32),
                   jax.ShapeDtypeStruct((n_tok, D_MODEL), BF16)],
        grid=(n_tok // tm,),
        in_specs=_token_specs(xs, tm, n_ctx_tiles, D_MODEL)
        + [pl.BlockSpec((1, 1, N_MOD * D_MODEL),
                               lambda i: (_mod_row_index(i, n_ctx_tiles, tps), 0, 0)),
                  pl.BlockSpec((2, tm // S5_T, S5_T * 128), lambda i: (0, i, 0)),
                  pl.BlockSpec((tm, DN_WIDTH), ctx_row),
                  pl.BlockSpec((tm, DN_WIDTH), lat_row),
                  pl.BlockSpec((tm, DN_WIDTH), row),
                  pl.BlockSpec((tm, CM_WIDTH), row),
                  pl.BlockSpec((tm, CM_WIDTH), row),
                  pl.BlockSpec((SSM_CH, SSM_CH), const),
                  pl.BlockSpec((1, SSM_CH), const),
                  pl.BlockSpec((1, DN_HEAD_DIM), const),
                  pl.BlockSpec((1, CM_WIDTH), const),
                  pl.BlockSpec((CM_CHUNK, CM_HEADS * CM_CHUNK), const),
                  pl.BlockSpec((CM_CHUNK, CM_WIDTH), const),
                  pl.BlockSpec((D_MODEL, D_MODEL), const),
                  pl.BlockSpec((1, D_MODEL), const)],
        out_specs=[pl.BlockSpec((tm, D_MODEL), row), pl.BlockSpec((tm, D_MODEL), row)],
        scratch_shapes=[pltpu.VMEM((2, tm, 128), F32)],
        compiler_params=_cparams(("parallel",), VMEM_LIMIT),
        name="mix_out",
    )(*xs, mods_l, ys, o_ctx, o_lat, z, cu, cv,
      p['ssm_w_glu'].astype(BF16), p['ssm_b_glu'].reshape(1, SSM_CH),
      p['dn_norm'].reshape(1, DN_HEAD_DIM), p['cm_norm'].reshape(1, CM_WIDTH),
      ws_cat, bs_exp, p['w_out'].astype(BF16), p['norm2'].reshape(1, D_MODEL))


ROUTE_TILE = 512


def _first_max_mask(x, idx, axis):
    m = jnp.max(x, axis=axis, keepdims=True)
    first = jnp.min(jnp.where(x == m, idx, jnp.int32(1 << 20)), axis=axis, keepdims=True)
    return idx == first


def _router_kernel(h_ref, wr_ref, bias_ref, g_ref):
    n = h_ref.shape[0]
    per = N_EXPERTS // N_ROUTE_GROUPS
    logits = _dot_nt(wr_ref[...], h_ref[...])
    scores = _sigmoid(logits)
    choice = scores + bias_ref[...]
    ninf = jnp.float32(-jnp.inf)
    c3 = choice.reshape(N_ROUTE_GROUPS, per, n)
    i3 = lax.broadcasted_iota(jnp.int32, c3.shape, 1)
    top1 = _first_max_mask(c3, i3, 1)
    m1 = jnp.max(c3, axis=1)
    m2 = jnp.max(jnp.where(top1, ninf, c3), axis=1)
    grp = m1 + m2
    gi = lax.broadcasted_iota(jnp.int32, grp.shape, 0)
    gsel = jnp.zeros(grp.shape, jnp.bool_)
    work = grp
    for _ in range(TOPK_ROUTE_GROUPS):
        pick = _first_max_mask(work, gi, 0)
        gsel = gsel | pick
        work = jnp.where(pick, ninf, work)
    gmask = jnp.broadcast_to(gsel[:, None, :], c3.shape).reshape(N_EXPERTS, n)
    work = jnp.where(gmask, choice, ninf)
    ei = lax.broadcasted_iota(jnp.int32, work.shape, 0)
    esel = jnp.zeros(work.shape, jnp.bool_)
    for _ in range(TOP_K):
        pick = _first_max_mask(work, ei, 0)
        esel = esel | pick
        work = jnp.where(pick, ninf, work)
    wts = jnp.where(esel, scores, 0.0)
    gates_t = wts / (jnp.sum(wts, axis=0, keepdims=True) + 1e-20) * ROUTED_SCALE
    gates = gates_t.T
    for j in range(N_EXPERTS // MOE_EG):
        g_ref[j] = gates[:, j * MOE_EG:(j + 1) * MOE_EG]


def _router(h2, router_w, router_bias):
    n_tok = h2.shape[0]
    tm = ROUTE_TILE
    neg = N_EXPERTS // MOE_EG
    return pl.pallas_call(
        _router_kernel,
        out_shape=jax.ShapeDtypeStruct((neg, n_tok, MOE_EG), F32),
        grid=(n_tok // tm,),
        in_specs=[pl.BlockSpec((tm, D_MODEL), lambda i: (i, 0)),
                  pl.BlockSpec((N_EXPERTS, D_MODEL), lambda i: (0, 0)),
                  pl.BlockSpec((N_EXPERTS, 1), lambda i: (0, 0))],
        out_specs=pl.BlockSpec((neg, tm, MOE_EG), lambda i: (0, i, 0)),
        compiler_params=_cparams(("parallel",)),
        name="router",
    )(h2, router_w.T.astype(BF16), router_bias.reshape(N_EXPERTS, 1))


MOE_TILE = 1024
MOE_EG = 8


def _moe_kernel(h_ref, x_ref, mod_ref, g_ref, wg_ref, wu_ref, wd_ref, sg_ref, su_ref, sd_ref, nf_ref,
                *out_refs, final, n_ctx_tiles):
    j = pl.program_id(1)
    nj = pl.num_programs(1)
    acc_ref = out_refs[-1]
    h = h_ref[...]

    @pl.when(j == 0)
    def _():
        sh = _silu(_dot(h, sg_ref[...])) * _dot(h, su_ref[...])
        acc_ref[...] = _dot(sh.astype(BF16), sd_ref[...])

    hg = _dot(h, wg_ref[...])
    hu = _dot(h, wu_ref[...])
    gates = g_ref[0]
    cols = []
    for e in range(MOE_EG):
        sl = slice(e * EXPERT_FF, (e + 1) * EXPERT_FF)
        cols.append((_silu(hg[:, sl]) * hu[:, sl] * gates[:, e:e + 1]).astype(BF16))
    act = jnp.concatenate(cols, axis=1)
    acc_ref[...] += _dot(act, wd_ref[...])

    @pl.when(j == nj - 1)
    def _():
        gate2 = mod_ref[0][:, 5 * D_MODEL:6 * D_MODEL]
        x2 = x_ref[...] + gate2 * acc_ref[...]
        if final:
            y = _rms(x2) * nf_ref[...]
            is_ctx = pl.program_id(0) < n_ctx_tiles

            @pl.when(is_ctx)
            def _():
                out_refs[0][...] = y

            @pl.when(jnp.logical_not(is_ctx))
            def _():
                out_refs[1][...] = y
        else:
            out_refs[0][...] = x2


def _expert_cols_kernel(w_ref, o_ref):
    for e in range(w_ref.shape[0]):
        o_ref[:, e * EXPERT_FF:(e + 1) * EXPERT_FF] = w_ref[e].astype(BF16)


def _expert_cols(w, layer):
    _, n_e, d, f = w.shape
    return pl.pallas_call(
        _expert_cols_kernel,
        out_shape=jax.ShapeDtypeStruct((d, n_e * f), BF16),
        grid=(n_e // MOE_EG,),
        in_specs=[pl.BlockSpec((None, MOE_EG, d, f), lambda j: (layer, j, 0, 0))],
        out_specs=pl.BlockSpec((d, MOE_EG * f), lambda j: (0, j)),
        compiler_params=_cparams(("parallel",)),
        name="expert_cols",
    )(w)


def _cast_rows_kernel(w_ref, o_ref):
    o_ref[...] = w_ref[...].astype(BF16)


def _cast_rows(w, layer, rows):
    _, r, c = w.shape
    return pl.pallas_call(
        _cast_rows_kernel,
        out_shape=jax.ShapeDtypeStruct((r, c), BF16),
        grid=(r // rows,),
        in_specs=[pl.BlockSpec((None, rows, c), lambda j: (layer, j, 0))],
        out_specs=pl.BlockSpec((rows, c), lambda j: (j, 0)),
        compiler_params=_cparams(("parallel",)),
        name="cast_rows",
    )(w)


def _moe(h2, x1, mods_l, gates_t, w_gate, w_up, w_down, layer, p, norm_f, final, n_ctx_tok, lat_seq):
    n_tok = h2.shape[0]
    tm = MOE_TILE
    n_ctx_tiles = n_ctx_tok // tm
    tps = lat_seq // tm
    neg = N_EXPERTS // MOE_EG
    ef = MOE_EG * EXPERT_FF
    wg = _expert_cols(w_gate, layer)
    wu = _expert_cols(w_up, layer)
    wd = _cast_rows(w_down.reshape(w_down.shape[0], N_EXPERTS * EXPERT_FF, D_MODEL), layer, 1024)
    row = lambda i, j: (i, 0)
    const = lambda i, j: (0, 0)
    if final:
        out_shape = [jax.ShapeDtypeStruct((n_ctx_tok, D_MODEL), F32),
                     jax.ShapeDtypeStruct((n_tok - n_ctx_tok, D_MODEL), F32)]
        out_specs = [pl.BlockSpec((tm, D_MODEL), lambda i, j: (jnp.minimum(i, n_ctx_tiles - 1), 0)),
                     pl.BlockSpec((tm, D_MODEL), lambda i, j: (jnp.maximum(i - n_ctx_tiles, 0), 0))]
    else:
        out_shape = jax.ShapeDtypeStruct((n_tok, D_MODEL), F32)
        out_specs = pl.BlockSpec((tm, D_MODEL), row)
    out = pl.pallas_call(
        functools.partial(_moe_kernel, final=final, n_ctx_tiles=n_ctx_tiles),
        out_shape=out_shape,
        grid=(n_tok // tm, neg),
        in_specs=[pl.BlockSpec((tm, D_MODEL), row),
                  pl.BlockSpec((tm, D_MODEL), row),
                  pl.BlockSpec((1, 1, N_MOD * D_MODEL),
                               lambda i, j: (_mod_row_index(i, n_ctx_tiles, tps), 0, 0)),
                  pl.BlockSpec((1, tm, MOE_EG), lambda i, j: (j, i, 0)),
                  pl.BlockSpec((D_MODEL, ef), lambda i, j: (0, j)),
                  pl.BlockSpec((D_MODEL, ef), lambda i, j: (0, j)),
                  pl.BlockSpec((ef, D_MODEL), lambda i, j: (j, 0)),
                  pl.BlockSpec((D_MODEL, SHARED_FF), const),
                  pl.BlockSpec((D_MODEL, SHARED_FF), const),
                  pl.BlockSpec((SHARED_FF, D_MODEL), const),
                  pl.BlockSpec((1, D_MODEL), const)],
        out_specs=out_specs,
        scratch_shapes=[pltpu.VMEM((tm, D_MODEL), F32)],
        compiler_params=_cparams(("arbitrary", "arbitrary"), VMEM_LIMIT_MAX if final else VMEM_LIMIT),
        name="moe",
    )(h2, x1, mods_l, gates_t, wg, wu, wd,
      p['moe_ws_gate'].astype(BF16), p['moe_ws_up'].astype(BF16), p['moe_ws_down'].astype(BF16),
      norm_f.reshape(1, D_MODEL))
    return out


def kernel(x_prompt, x_sample, state_ssm, state_delta, c, c_ctx, w_ada, b_ada, norm1, norm2, w_in, w_out, ssm_lam_re, ssm_lam_im, ssm_log_dt, ssm_b_re, ssm_b_im, ssm_c_re, ssm_c_im, ssm_d, ssm_w_glu, ssm_b_glu, dn_conv, dn_a_log, dn_dt_bias, dn_norm, cm_norm, cm_w_s, cm_b_s, moe_router, moe_router_bias, moe_w_gate, moe_w_up, moe_w_down, moe_ws_gate, moe_ws_up, moe_ws_down, norm_f):
    n_ctx_seq, ctx_seq, _ = x_prompt.shape
    n_lat_seq, lat_seq, _ = x_sample.shape
    depth = w_ada.shape[0]
    n_ctx_tok = n_ctx_seq * ctx_seq
    n_seq = n_ctx_seq + n_lat_seq
    g = SSM_GROUPS

    xs = [x_prompt.reshape(n_ctx_tok, D_MODEL), x_sample.reshape(n_lat_seq * lat_seq, D_MODEL)]
    cvec = jnp.concatenate([c_ctx[None, :], c, jnp.zeros((8 - 1 - n_lat_seq, D_MODEL), F32)], axis=0)
    mods = _ada_mods(cvec, w_ada, b_ada)

    ssm_states, dn_states = [], []
    y = None
    for l in range(depth):
        p = {'w_out': w_out[l], 'norm2': norm2[l], 'ssm_w_glu': ssm_w_glu[l], 'ssm_b_glu': ssm_b_glu[l],
             'dn_conv': dn_conv[l], 'dn_a_log': dn_a_log[l], 'dn_dt_bias': dn_dt_bias[l],
             'dn_norm': dn_norm[l], 'cm_norm': cm_norm[l], 'cm_w_s': cm_w_s[l], 'cm_b_s': cm_b_s[l],
             'moe_ws_gate': moe_ws_gate[l], 'moe_ws_up': moe_ws_up[l], 'moe_ws_down': moe_ws_down[l]}
        mods_l = mods[l].reshape(8, 1, N_MOD * D_MODEL)
        s_u, z, cu, cv, qkvn, gcb, gcb_t = _in_projection(
            xs, mods_l, norm1[l], _cast_rows(w_in, l, 256), dn_conv[l], dn_a_log[l], dn_dt_bias[l],
            n_ctx_tok, ctx_seq, lat_seq)

        mats = _s5_prep(ssm_lam_re[l], ssm_lam_im[l], ssm_log_dt[l], ssm_b_re[l], ssm_b_im[l],
                        ssm_c_re[l], ssm_c_im[l])
        h0_lat = state_ssm[:, l].transpose(0, 3, 2, 1, 4).reshape(n_lat_seq, g, 256)
        h0 = jnp.concatenate([jnp.zeros((n_ctx_seq, g, 256), F32), h0_lat], axis=0)
        ys, ssm_fin = _s5_mixer(s_u, mats, ssm_d[l], h0, n_ctx_seq, ctx_seq, lat_seq)
        ssm_states.append(ssm_fin[:n_ctx_seq].reshape(n_ctx_seq, g, 2, 2, SSM_STATE).transpose(0, 3, 2, 1, 4))

        o_ctx, o_lat, dn_fin = _deltanet(qkvn, gcb, gcb_t, state_delta[:, l], n_ctx_seq, ctx_seq, n_lat_seq,
                                         lat_seq)
        dn_states.append(dn_fin)

        x1, h2 = _mix_out(xs, mods_l, ys, o_ctx, o_lat, z, cu, cv, p, n_ctx_tok, lat_seq)
        gates_t = _router(h2, moe_router[l], moe_router_bias[l])
        x = _moe(h2, x1, mods_l, gates_t, moe_w_gate, moe_w_up, moe_w_down, l, p, norm_f, l == depth - 1,
                 n_ctx_tok, lat_seq)
        xs = list(x) if isinstance(x, (list, tuple)) else [x]

    y_prompt = xs[0].reshape(x_prompt.shape)
    y_sample = xs[1].reshape(x_sample.shape)
    new_state_ssm = jnp.stack(ssm_states, axis=1)
    new_state_delta = jnp.stack(dn_states, axis=1)
    return (y_prompt, y_sample, new_state_ssm, new_state_delta)
```

```python
import functools
import math

import jax
import jax.numpy as jnp
from jax import lax
from jax.experimental import pallas as pl
from jax.experimental.pallas import tpu as pltpu

F32 = jnp.float32
BF16 = jnp.bfloat16
HIGHEST = lax.Precision.HIGHEST

D_MODEL = 1024
N_MOD = 6
EPS = 1e-6
SSM_CH = 256
SSM_GROUPS = 16
SSM_GROUP_CH = 16
SSM_STATE = 64
S5_T = 16
S5_W = S5_T * SSM_GROUP_CH
DN_HEADS = 4
DN_HEAD_DIM = 128
DN_WIDTH = 512
DN_CHUNK = 64
CM_HEADS = 4
CM_WIDTH = 256
CM_HEAD_DIM = 64
CM_CHUNK = 128
N_EXPERTS = 64
TOP_K = 6
N_ROUTE_GROUPS = 8
TOPK_ROUTE_GROUPS = 4
EXPERT_FF = 128
SHARED_FF = 256
ROUTED_SCALE = 2.5
MAIN_COLS = SSM_CH + 3 * DN_WIDTH + DN_WIDTH + 2 * CM_WIDTH
VMEM_LIMIT = 56 * 1024 * 1024
VMEM_LIMIT_MAX = 60 * 1024 * 1024


def _cparams(sem, vmem=None):
    return pltpu.CompilerParams(dimension_semantics=sem, vmem_limit_bytes=vmem)


def _dot(a, b):
    return jnp.dot(a, b, preferred_element_type=F32)


def _dot_nt(a, b, precision=None):
    return lax.dot_general(a, b, (((1,), (1,)), ((), ())), preferred_element_type=F32,
                           precision=precision)


def _dot_tn(a, b):
    return lax.dot_general(a, b, (((0,), (0,)), ((), ())), preferred_element_type=F32)


def _sigmoid(x):
    return 1.0 / (1.0 + jnp.exp(-x))


def _silu(x):
    return x * _sigmoid(x)


def _gelu(x):
    c = math.sqrt(2.0 / math.pi)
    return 0.5 * x * (1.0 + jnp.tanh(c * (x + 0.044715 * (x * x * x))))


def _softplus(x):
    return jnp.maximum(x, 0.0) + jnp.log(1.0 + jnp.exp(-jnp.abs(x)))


def _rms(x):
    return x * lax.rsqrt(jnp.mean(x * x, axis=-1, keepdims=True) + EPS)


def _mod_row_index(tile, n_ctx_tiles, tiles_per_lat_seq):
    return jnp.where(tile < n_ctx_tiles, 0, 1 + (tile - n_ctx_tiles) // tiles_per_lat_seq)


def _ada_kernel(c_ref, w_ref, b_ref, o_ref):
    h = _silu(c_ref[...])
    o_ref[0] = jnp.dot(h, w_ref[0], preferred_element_type=F32, precision=HIGHEST) + b_ref[0]


def _ada_mods(cvec, w_ada, b_ada):
    depth, d, n = w_ada.shape
    bn = 1536
    return pl.pallas_call(
        _ada_kernel,
        out_shape=jax.ShapeDtypeStruct((depth, 8, n), F32),
        grid=(depth, n // bn),
        in_specs=[pl.BlockSpec((8, d), lambda l, j: (0, 0)),
                  pl.BlockSpec((1, d, bn), lambda l, j: (l, 0, j)),
                  pl.BlockSpec((1, 1, bn), lambda l, j: (l, 0, j))],
        out_specs=pl.BlockSpec((1, 8, bn), lambda l, j: (l, 0, j)),
        compiler_params=_cparams(("parallel", "parallel")),
        name="ada_mods",
    )(cvec, w_ada, b_ada.reshape(depth, 1, n))


def _token_specs(xs, tm, n_ctx_tiles, width):
    if len(xs) == 1:
        return [pl.BlockSpec((tm, width), lambda i, *_: (i, 0))]
    return [pl.BlockSpec((tm, width), lambda i, *_: (jnp.minimum(i, n_ctx_tiles - 1), 0)),
            pl.BlockSpec((tm, width), lambda i, *_: (jnp.maximum(i - n_ctx_tiles, 0), 0))]


def _token_tile(refs, n_ctx_tiles, tile=None):
    if len(refs) == 1:
        return refs[0][...]
    tile = pl.program_id(0) if tile is None else tile
    return jnp.where(tile < n_ctx_tiles, refs[0][...], refs[1][...])


IN_TILE = 1024
IN_PARTS = 4


def _inproj_kernel(*refs, n_x, n_ctx_tiles, tiles_per_lat_seq, ctx_seq):
    x_refs = refs[:n_x]
    (prev_ref, next_ref, mod_ref, g_ref, w_ref, cw_ref, alog_ref, dtb_ref,
     su_ref, z_ref, cu_ref, cv_ref, qkvn_ref, gcb_ref, gt_ref, scr) = refs[n_x:]
    mod = mod_ref[0]
    shift = mod[:, 0:D_MODEL]
    scale = mod[:, D_MODEL:2 * D_MODEL]
    gain = g_ref[...] * (1.0 + scale)

    c0 = SSM_CH
    c1 = c0 + 3 * DN_WIDTH
    c2 = c1 + DN_WIDTH
    n_ab = 4 * DN_HEADS
    def normed_f32(v):
        return _rms(v) * gain + shift

    hf = normed_f32(_token_tile(x_refs, n_ctx_tiles))
    n = hf.shape[0]
    rq = n // IN_PARTS
    w_qkv = w_ref[:, c0:c1]
    edges = [normed_f32(prev_ref[...]), normed_f32(next_ref[...])]
    for b in range(1, IN_PARTS):
        edges += [hf[b * rq - 8:b * rq], hf[b * rq:b * rq + 8]]
    x_edge = _dot(jnp.concatenate(edges, axis=0).astype(BF16), w_qkv)
    xs_, tails = [], []
    for r in range(IN_PARTS):
        rows = slice(r * rq, (r + 1) * rq)
        hb = hf[rows].astype(BF16)
        xs_.append(_dot(hb, w_qkv))
        tail = _dot(hb, w_ref[:, c2:])
        tails.append(tail)
        su = _dot(hb, w_ref[:, 0:c0])
        z_ref[rows, :] = _dot(hb, w_ref[:, c1:c2])
        cu_ref[rows, :] = tail[:, n_ab:n_ab + CM_WIDTH]
        cv_ref[rows, :] = tail[:, n_ab + CM_WIDTH:n_ab + 2 * CM_WIDTH]
        scr[0, rows, :] = su[:, 0:128]
        scr[1, rows, :] = su[:, 128:c0]
    n_rows = su_ref.shape[1]
    for half in range(2):
        for t in range(S5_T):
            su_ref[half, :, t * 128:(t + 1) * 128] = scr[half, pl.ds(t, n_rows, stride=S5_T), :]

    i = pl.program_id(0)
    is_ctx = i < n_ctx_tiles
    lat_pos = (i - n_ctx_tiles) % tiles_per_lat_seq
    has_prev = jnp.logical_not(is_ctx) & (lat_pos > 0)
    has_next = jnp.logical_not(is_ctx) & (lat_pos < tiles_per_lat_seq - 1)
    cw = cw_ref[...]
    lane = lax.broadcasted_iota(jnp.int32, (1, 128), 1)
    ii = lax.broadcasted_iota(jnp.int32, (DN_CHUNK, DN_CHUNK), 0)
    jj = lax.broadcasted_iota(jnp.int32, (DN_CHUNK, DN_CHUNK), 1)
    pre = jnp.where(jj <= ii, 1.0, 0.0).astype(F32)
    suf = jnp.where(jj >= ii, 1.0, 0.0).astype(F32)
    n_ch = rq // DN_CHUNK
    for r in range(IN_PARTS):
        rows = slice(r * rq, (r + 1) * rq)
        x = xs_[r]
        if r == 0:
            prev_row = jnp.where(has_prev, x_edge[7:8, :], 0.0)
        else:
            prev_row = x_edge[16 * r + 7:16 * r + 8, :]
        if r == IN_PARTS - 1:
            next_row = jnp.where(has_next, x_edge[8:9, :], 0.0)
        else:
            next_row = x_edge[16 * (r + 1) + 8:16 * (r + 1) + 9, :]
        local = lax.broadcasted_iota(jnp.int32, (rq, 1), 0)
        row = r * rq + local
        seq_first = is_ctx & (row % ctx_seq == 0)
        seq_last = is_ctx & (row % ctx_seq == ctx_seq - 1)
        xp = jnp.where(local == 0, prev_row, pltpu.roll(x, 1, 0))
        xn = jnp.where(local == rq - 1, next_row, pltpu.roll(x, rq - 1, 0))
        xp = jnp.where(seq_first, 0.0, xp)
        xn = jnp.where(seq_last, 0.0, xn)
        y = _silu(cw[0:1] * xp + cw[1:2] * x + cw[2:3] * xn)
        for hcol in range(2 * DN_HEADS):
            seg = y[:, hcol * DN_HEAD_DIM:(hcol + 1) * DN_HEAD_DIM]
            nrm = seg * lax.rsqrt(jnp.sum(seg * seg, axis=-1, keepdims=True) + EPS)
            if hcol < DN_HEADS:
                nrm = nrm * (DN_HEAD_DIM ** -0.5)
            qkvn_ref[rows, hcol * DN_HEAD_DIM:(hcol + 1) * DN_HEAD_DIM] = nrm.astype(BF16)
        qkvn_ref[rows, 2 * DN_WIDTH:] = y[:, 2 * DN_WIDTH:].astype(BF16)
        ab = tails[r][:, 0:128]
        glog = jnp.where(lane < 2 * DN_HEADS, -jnp.exp(alog_ref[...]) * _softplus(ab + dtb_ref[...]), 0.0)
        beta = _sigmoid(ab)
        wide = jnp.concatenate([glog[c * DN_CHUNK:(c + 1) * DN_CHUNK] for c in range(n_ch)], axis=1)
        gpre_w = jnp.dot(pre, wide, preferred_element_type=F32, precision=HIGHEST)
        gsuf_w = jnp.dot(suf, wide, preferred_element_type=F32, precision=HIGHEST)
        unwide = lambda a: jnp.concatenate([a[:, c * 128:(c + 1) * 128] for c in range(n_ch)], axis=0)
        gcb = jnp.where(lane < DN_HEADS, unwide(gpre_w),
                        jnp.where(lane < 2 * DN_HEADS, unwide(gsuf_w), beta))
        gcb_ref[rows, :] = gcb
        gcb_t = gcb.T
        for c in range(n_ch):
            gt_ref[r * n_ch + c] = gcb_t[0:16, c * DN_CHUNK:(c + 1) * DN_CHUNK]


def _in_projection(xs, mods_l, norm1, w_in_bf, conv_w, a_log, dt_bias, n_ctx_tok, ctx_seq, lat_seq):
    n_tok = sum(x.shape[0] for x in xs)
    tm = IN_TILE
    assert tm % ctx_seq == 0 and lat_seq % tm == 0
    n_ctx_tiles = n_ctx_tok // tm
    tps = lat_seq // tm
    row = lambda i: (i, 0)
    const = lambda i: (0, 0)
    in_cols = w_in_bf.shape[1]
    r8 = tm // 8
    halo_src = xs[-1]
    first = n_ctx_tiles if len(xs) == 2 else 0
    last8 = halo_src.shape[0] // 8 - 1
    prev_spec = pl.BlockSpec((8, D_MODEL), lambda i: (jnp.clip((i - first) * r8 - 1, 0, last8), 0))
    next_spec = pl.BlockSpec((8, D_MODEL), lambda i: (jnp.clip((i - first + 1) * r8, 0, last8), 0))
    w8 = jnp.concatenate([conv_w, jnp.zeros((5, 3 * DN_WIDTH), F32)], axis=0)
    pad = lambda v: jnp.concatenate([v.reshape(1, 2 * DN_HEADS), jnp.zeros((1, 120), F32)], axis=1)
    outs = [(DN_WIDTH, F32), (CM_WIDTH, F32), (CM_WIDTH, F32), (3 * DN_WIDTH, BF16), (128, F32)]
    return pl.pallas_call(
        functools.partial(_inproj_kernel, n_x=len(xs), n_ctx_tiles=n_ctx_tiles, tiles_per_lat_seq=tps,
                          ctx_seq=ctx_seq),
        out_shape=[jax.ShapeDtypeStruct((2, n_tok // S5_T, S5_T * 128), F32)]
        + [jax.ShapeDtypeStruct((n_tok, w), dt) for w, dt in outs]
        + [jax.ShapeDtypeStruct((n_tok // DN_CHUNK, 16, DN_CHUNK), F32)],
        grid=(n_tok // tm,),
        in_specs=_token_specs(xs, tm, n_ctx_tiles, D_MODEL)
        + [prev_spec, next_spec,
           pl.BlockSpec((1, 1, N_MOD * D_MODEL),
                        lambda i: (_mod_row_index(i, n_ctx_tiles, tps), 0, 0)),
           pl.BlockSpec((1, D_MODEL), const),
           pl.BlockSpec((D_MODEL, in_cols), const),
           pl.BlockSpec((8, 3 * DN_WIDTH), const),
           pl.BlockSpec((1, 128), const),
           pl.BlockSpec((1, 128), const)],
        out_specs=[pl.BlockSpec((2, tm // S5_T, S5_T * 128), lambda i: (0, i, 0))]
        + [pl.BlockSpec((tm, w), row) for w, _ in outs]
        + [pl.BlockSpec((tm // DN_CHUNK, 16, DN_CHUNK), lambda i: (i, 0, 0))],
        scratch_shapes=[pltpu.VMEM((2, tm, 128), F32)],
        compiler_params=_cparams(("parallel",), VMEM_LIMIT),
        name="in_projection",
    )(*xs, halo_src, halo_src, mods_l, norm1.reshape(1, D_MODEL), w_in_bf, w8, pad(a_log), pad(dt_bias))


def _s5_prep_kernel(lre_ref, lim_ref, ldt_ref, btre_ref, btim_ref, cre_ref, cim_ref,
                    mt_ref, pt_ref, qt_ref, at_ref):
    t_n, hg, w = S5_T, SSM_GROUP_CH, S5_W
    lre, lim = lre_ref[0], lim_ref[0]
    dt = jnp.exp(ldt_ref[0])
    xr, xi = lre * dt, lim * dt
    tau = lax.broadcasted_iota(jnp.int32, (t_n, 128), 0).astype(F32)
    mag = jnp.exp(tau * xr)
    a_re, a_im = mag * jnp.cos(tau * xi), mag * jnp.sin(tau * xi)
    ar, ai = jnp.exp(xr) * jnp.cos(xi), jnp.exp(xr) * jnp.sin(xi)
    inv = 1.0 / (lre * lre + lim * lim)
    br_, bi_ = lre * inv, -lim * inv
    fr = (ar - 1.0) * br_ - ai * bi_
    fi = (ar - 1.0) * bi_ + ai * br_
    btre, btim = btre_ref[0], btim_ref[0]
    bbr = fr * btre - fi * btim
    bbi = fr * btim + fi * btre
    cre, cim = cre_ref[0], cim_ref[0]

    lane = lax.broadcasted_iota(jnp.int32, (1, 128), 1)
    is_f = lane < SSM_STATE

    def rep_rows(a):
        return jnp.concatenate([jnp.broadcast_to(a[t:t + 1], (hg, 128)) for t in range(t_n)], axis=0)

    def tile_rows(c):
        return jnp.concatenate([c] * t_n, axis=0)

    def rev_rows(a):
        return jnp.concatenate([a[t_n - 1 - t:t_n - t] for t in range(t_n)], axis=0)

    cr_t, ci_t = tile_rows(cre), tile_rows(cim)
    a_re_rep, a_im_rep = rep_rows(a_re), rep_rows(a_im)
    w_re = cr_t * a_re_rep - ci_t * a_im_rep
    w_im = cr_t * a_im_rep + ci_t * a_re_rep
    zero = jnp.zeros_like(w_re)
    e_f = jnp.concatenate([jnp.where(is_f, w_re, zero), jnp.where(is_f, -w_im, zero)], axis=1)
    e_b = jnp.concatenate([jnp.where(is_f, zero, w_re), jnp.where(is_f, zero, -w_im)], axis=1)
    e_b_rev = jnp.concatenate([e_b[(t_n - 1 - t) * hg:(t_n - t) * hg] for t in range(t_n)], axis=0)
    bb_t = tile_rows(jnp.concatenate([bbr, bbi], axis=1))
    row_blk = lax.broadcasted_iota(jnp.int32, (w, w), 0) // hg

    m = jnp.zeros((w, w), F32)
    zrows = lambda n: jnp.zeros((n * hg, w), F32)
    for s in range(t_n):
        down = e_f if s == 0 else jnp.concatenate([zrows(s), e_f[:(t_n - s) * hg]], axis=0)
        k = t_n - 1 - s
        up = e_b_rev if k == 0 else jnp.concatenate([e_b_rev[k * hg:], zrows(k)], axis=0)
        rhs = jnp.where(row_blk == s, bb_t, 0.0)
        m = m + _dot_nt((down + up).astype(BF16), rhs.astype(BF16))
    mt_ref[...] = m.T.reshape(t_n, hg, w).astype(BF16)

    a_re_rev, a_im_rev = rev_rows(a_re), rev_rows(a_im)
    ps_re = rep_rows(jnp.where(is_f, a_re_rev, a_re))
    ps_im = rep_rows(jnp.where(is_f, a_im_rev, a_im))
    bbr_t, bbi_t = tile_rows(bbr), tile_rows(bbi)
    p_re = ps_re * bbr_t - ps_im * bbi_t
    p_im = ps_re * bbi_t + ps_im * bbr_t
    pt_ref[...] = jnp.concatenate([p_re, p_im], axis=1).reshape(t_n, hg, w).astype(BF16)

    a1_re = a_re * ar - a_im * ai
    a1_im = a_re * ai + a_im * ar
    aq_re = rep_rows(jnp.where(is_f, a1_re, rev_rows(a1_re)))
    aq_im = rep_rows(jnp.where(is_f, a1_im, rev_rows(a1_im)))
    q_re = cr_t * aq_re - ci_t * aq_im
    q_im = cr_t * aq_im + ci_t * aq_re
    qt_ref[0] = jnp.concatenate([q_re, -q_im], axis=1).T.astype(BF16)
    at_ref[0] = jnp.concatenate([a1_re[t_n - 1:t_n], a1_im[t_n - 1:t_n]], axis=1)


def _s5_prep(lam_re, lam_im, log_dt, b_re, b_im, c_re, c_im):
    g = SSM_GROUPS
    cat = lambda a: jnp.concatenate([a[0], a[1]], axis=-1)
    lre = cat(lam_re)[:, None, :]
    lim = cat(lam_im)[:, None, :]
    ldt = cat(jnp.broadcast_to(log_dt[..., None], (2, g, SSM_STATE)))[:, None, :]
    btre = cat(jnp.swapaxes(b_re, -1, -2))
    btim = cat(jnp.swapaxes(b_im, -1, -2))
    cre, cim = cat(c_re), cat(c_im)
    vec = pl.BlockSpec((1, 1, 128), lambda i: (i, 0, 0))
    mat = pl.BlockSpec((1, SSM_GROUP_CH, 128), lambda i: (i, 0, 0))
    big = pl.BlockSpec((1, S5_W, S5_W), lambda i: (i, 0, 0))
    gh = g // 2
    arranged = pl.BlockSpec((None, S5_T, None, SSM_GROUP_CH, S5_W), lambda i: (i // gh, 0, i % gh, 0, 0))
    arranged_shape = jax.ShapeDtypeStruct((2, S5_T, gh, SSM_GROUP_CH, S5_W), BF16)
    return pl.pallas_call(
        _s5_prep_kernel,
        out_shape=[arranged_shape, arranged_shape, jax.ShapeDtypeStruct((g, S5_W, S5_W), BF16),
                   jax.ShapeDtypeStruct((g, 1, 256), F32)],
        grid=(g,),
        in_specs=[vec, vec, vec, mat, mat, mat, mat],
        out_specs=[arranged, arranged, big, pl.BlockSpec((1, 1, 256), lambda i: (i, 0, 0))],
        compiler_params=_cparams(("parallel",)),
        name="s5_prep",
    )(lre, lim, ldt, btre, btim, cre, cim)


S5_SPREAD_ROWS = 512


def _s5_spread_kernel(a_ref, o_ref, *, rows_per_group, gh):
    hg, w = SSM_GROUP_CH, S5_W
    n_rows, n_cols = o_ref.shape
    src = lax.broadcasted_iota(jnp.int32, (w, n_cols), 0)
    dst = lax.broadcasted_iota(jnp.int32, (w, n_cols), 1)
    tile = jnp.where((src // hg == dst // 128) & (src % hg == dst % hg), 1.0, 0.0).astype(BF16)
    row = pl.program_id(1) * n_rows + lax.broadcasted_iota(jnp.int32, (n_rows, n_cols), 0)
    col = lax.broadcasted_iota(jnp.int32, (n_rows, n_cols), 1)
    keep = (row // rows_per_group) % gh == (col // hg) % gh
    o_ref[...] = jnp.where(keep, _dot(a_ref[...], tile), 0.0).astype(BF16)


def _s5_spread_cols(a, rows_per_group, gh):
    _, n_rows, w = a.shape
    rb = S5_SPREAD_ROWS
    return pl.pallas_call(
        functools.partial(_s5_spread_kernel, rows_per_group=rows_per_group, gh=gh),
        out_shape=jax.ShapeDtypeStruct((2, n_rows, S5_T * 128), BF16),
        grid=(2, n_rows // rb),
        in_specs=[pl.BlockSpec((None, rb, w), lambda z, i: (z, i, 0))],
        out_specs=pl.BlockSpec((None, rb, S5_T * 128), lambda z, i: (z, i, 0)),
        compiler_params=_cparams(("parallel", "parallel")),
        name="s5_spread_cols",
    )(a)


def _s5_place_kernel(a_ref, o_ref, *, gh):
    a = a_ref[...]
    n_rows, w = a.shape
    grp = ((pl.program_id(1) * n_rows + lax.broadcasted_iota(jnp.int32, (n_rows, w), 0))
           // SSM_GROUP_CH) % gh
    for k in range(gh):
        o_ref[:, k * w:(k + 1) * w] = jnp.where(grp == k, a, jnp.zeros_like(a))


def _s5_place_cols(a, gh):
    _, n_rows, w = a.shape
    rb = S5_SPREAD_ROWS
    return pl.pallas_call(
        functools.partial(_s5_place_kernel, gh=gh),
        out_shape=jax.ShapeDtypeStruct((2, n_rows, gh * w), BF16),
        grid=(2, n_rows // rb),
        in_specs=[pl.BlockSpec((None, rb, w), lambda z, i: (z, i, 0))],
        out_specs=pl.BlockSpec((None, rb, gh * w), lambda z, i: (z, i, 0)),
        compiler_params=_cparams(("parallel", "parallel")),
        name="s5_place_cols",
    )(a)


def _s5_z_kernel(u_ref, pt_ref, z_ref):
    z_ref[0] = _dot(u_ref[0].astype(BF16), pt_ref[0])


def _s5_rec_kernel(z_ref, a_ref, h0_ref, hf_ref, hb_ref, fin_ref, *, n_ctx_seq, ctx_chunks, n_lat_seq,
                   lat_chunks):
    sw = z_ref.shape[-1]
    ng = sw // 256
    lane = lax.broadcasted_iota(jnp.int32, (1, sw), 1)
    is_f = (lane % 128) < SSM_STATE

    def split(row):
        re = jnp.concatenate([row[:, g * 256:g * 256 + 128] for g in range(ng)], axis=1)
        im = jnp.concatenate([row[:, g * 256 + 128:(g + 1) * 256] for g in range(ng)], axis=1)
        return re, im

    def merge(re, im):
        parts = []
        for g in range(ng):
            parts += [re[:, g * 128:(g + 1) * 128], im[:, g * 128:(g + 1) * 128]]
        return jnp.concatenate(parts, axis=1)

    a_re, a_im = split(a_ref[0])

    def run_seq(base, n, seq):
        def step(k, h):
            h_re, h_im = h
            cf = base + k
            cb = base + n - 1 - k
            row = merge(h_re, h_im)
            hf_ref[0, pl.ds(cf, 1), :] = row
            hb_ref[0, pl.ds(cb, 1), :] = row
            z_re, z_im = split(jnp.where(is_f, z_ref[0, pl.ds(cf, 1), :], z_ref[0, pl.ds(cb, 1), :]))
            return (a_re * h_re - a_im * h_im + z_re, a_re * h_im + a_im * h_re + z_im)
        h_re, h_im = lax.fori_loop(0, n, step, split(h0_ref[0, pl.ds(seq, 1), :]))
        fin_ref[0, pl.ds(seq, 1), :] = merge(h_re, h_im)

    def ctx_body(s, carry):
        run_seq(s * ctx_chunks, ctx_chunks, s)
        return carry
    lax.fori_loop(0, n_ctx_seq, ctx_body, 0)

    def lat_body(s, carry):
        run_seq(n_ctx_seq * ctx_chunks + s * lat_chunks, lat_chunks, n_ctx_seq + s)
        return carry
    lax.fori_loop(0, n_lat_seq, lat_body, 0)


def _s5_y_kernel(u_ref, hf_ref, hb_ref, mt_ref, qt_ref, d_ref, y_ref):
    u = u_ref[0]
    lane = lax.broadcasted_iota(jnp.int32, (1, u.shape[-1]), 1)
    is_f = (lane % 128) < SSM_STATE
    h = jnp.where(is_f, hf_ref[0], hb_ref[0]).astype(BF16)
    y_ref[0] = _dot(u.astype(BF16), mt_ref[0]) + _dot(h, qt_ref[0]) + d_ref[0] * u


def _s5_mixer(s_u, mats, ssm_d, h0, n_ctx_seq, ctx_seq, lat_seq):
    mt, pt, qt, at = mats
    g, t_n, w, hg = SSM_GROUPS, S5_T, S5_W, SSM_GROUP_CH
    gh = g // 2
    nc = s_u.shape[1]
    bw = t_n * 128
    sw = gh * w
    n_seq = h0.shape[0]
    ctx_chunks, lat_chunks = ctx_seq // t_n, lat_seq // t_n
    n_lat_seq = n_seq - n_ctx_seq
    u = s_u
    m_big = _s5_spread_cols(mt.reshape(2, bw, w), hg, gh)
    q_big = _s5_spread_cols(qt.reshape(2, sw, w), w, gh)
    p_big = _s5_place_cols(pt.reshape(2, bw, w), gh)
    d_big = jnp.tile(ssm_d.reshape(2, 1, 128), (1, 1, t_n))

    rb = math.gcd(nc, 256)
    u_spec = pl.BlockSpec((1, rb, bw), lambda z, i: (z, i, 0))
    s_spec = pl.BlockSpec((1, rb, sw), lambda z, i: (z, i, 0))
    z = pl.pallas_call(
        _s5_z_kernel,
        out_shape=jax.ShapeDtypeStruct((2, nc, sw), F32),
        grid=(2, nc // rb),
        in_specs=[u_spec, pl.BlockSpec((1, bw, sw), lambda z, i: (z, 0, 0))],
        out_specs=s_spec,
        compiler_params=_cparams(("parallel", "parallel"), VMEM_LIMIT),
        name="s5_chunk_inputs",
    )(u, p_big)

    half = lambda rows: pl.BlockSpec((1, rows, sw), lambda z: (z, 0, 0))
    hf, hb, fin = pl.pallas_call(
        functools.partial(_s5_rec_kernel, n_ctx_seq=n_ctx_seq, ctx_chunks=ctx_chunks,
                          n_lat_seq=n_lat_seq, lat_chunks=lat_chunks),
        out_shape=[jax.ShapeDtypeStruct((2, nc, sw), F32)] * 2 + [jax.ShapeDtypeStruct((2, n_seq, sw), F32)],
        grid=(2,),
        in_specs=[half(nc), half(1), half(n_seq)],
        out_specs=[half(nc), half(nc), half(n_seq)],
        compiler_params=_cparams(("parallel",), VMEM_LIMIT),
        name="s5_recurrence",
    )(z, at.reshape(2, 1, sw), h0.reshape(n_seq, 2, sw).transpose(1, 0, 2))

    y = pl.pallas_call(
        _s5_y_kernel,
        out_shape=jax.ShapeDtypeStruct((2, nc, bw), F32),
        grid=(2, nc // rb),
        in_specs=[u_spec, s_spec, s_spec,
                  pl.BlockSpec((1, bw, bw), lambda z, i: (z, 0, 0)),
                  pl.BlockSpec((1, sw, bw), lambda z, i: (z, 0, 0)),
                  pl.BlockSpec((1, 1, bw), lambda z, i: (z, 0, 0))],
        out_specs=u_spec,
        compiler_params=_cparams(("parallel", "parallel"), VMEM_LIMIT),
        name="s5_outputs",
    )(u, hf, hb, m_big, q_big, d_big)
    return y, fin.transpose(1, 0, 2).reshape(n_seq, g, w)


DN_CHAINS = 2 * DN_HEADS


def _split(x):
    hi = x.astype(BF16)
    lo = (x - hi.astype(F32)).astype(BF16)
    return hi, lo


def _split_lhs(a):
    hi, lo = _split(a)
    hif = hi.astype(F32)
    return jnp.concatenate([hif, hif, lo.astype(F32)], axis=1).astype(BF16)


def _split_rhs(b):
    hi, lo = _split(b)
    return jnp.concatenate([hi, lo, hi], axis=0)


def _dn_prepare(q_ref, k_ref, v_ref, gc_ref, gt_ref, row0, chunk_idx):
    c = DN_CHUNK
    ii = lax.broadcasted_iota(jnp.int32, (c, c), 0)
    jj = lax.broadcasted_iota(jnp.int32, (c, c), 1)
    items = []
    for d in range(2):
        rows = pl.ds(row0[d], c)
        gcol = gc_ref[rows, :]
        grow = gt_ref[chunk_idx[d]]
        incl = (ii >= jj) if d == 0 else (ii <= jj)
        strict = (ii > jj) if d == 0 else (ii < jj)
        for h in range(DN_HEADS):
            ch = d * DN_HEADS + h
            cols = slice(h * DN_HEAD_DIM, (h + 1) * DN_HEAD_DIM)
            q, k, v = q_ref[rows, cols], k_ref[rows, cols], v_ref[rows, cols]
            gc_col = gcol[:, ch:ch + 1]
            beta = gcol[:, DN_CHAINS + ch:DN_CHAINS + ch + 1]
            gc_row = grow[ch:ch + 1, :]
            kf = k.astype(F32)
            kb = kf * beta
            decay = jnp.exp(jnp.where(incl, gc_col - gc_row, -jnp.inf))
            egc = jnp.exp(gc_col)
            g_last = gc_col[c - 1:c] if d == 0 else gc_col[0:1]
            items.append(dict(
                q=q, k=k, kb=kb.astype(BF16), decay=decay, incl=incl, strict=strict,
                rhs=jnp.concatenate([kb * egc, v.astype(F32) * beta], axis=1),
                qg=(q.astype(F32) * egc).astype(BF16),
                kt=(kf * jnp.exp(g_last - gc_col)).astype(BF16),
                gl=jnp.broadcast_to(jnp.exp(g_last), (8, 128))))
    for it in items:
        it['a'] = jnp.where(it['strict'], _dot_nt(it['kb'], it['k']) * it['decay'], 0.0)
        it['qk'] = jnp.where(it['incl'], _dot_nt(it['q'], it['k']) * it['decay'], 0.0).astype(BF16)
    power = 1
    while power < c:
        for it in items:
            it['lhs'] = _split_lhs(it['a'])
            if 2 * power < c:
                it['a'] = _dot(it['lhs'], _split_rhs(it['a']))
        for it in items:
            upd = _dot(it['lhs'], _split_rhs(it['rhs'] if power == 1 else it['r']))
            it['r'] = it['rhs'] - upd if power == 1 else it['r'] + upd
        power *= 2
    return items


def _dn_scan_kernel(*refs, n_chunks, seq_len, seqs_per_blk, has_s0, write_fin):
    refs = list(refs)
    q_ref, k_ref, v_ref, gc_ref, gt_ref = refs[:5]
    pos = 5
    s0_ref = None
    if has_s0:
        s0_ref = refs[pos]
        pos += 1
    o_ref = refs[pos]
    pos += 1
    fin_ref = None
    if write_fin:
        fin_ref = refs[pos]
        pos += 1
    s_ref, w_s, u_s, qk_s, qg_s, kt_s, gl_s = refs[pos:pos + 7]
    n_steps = seqs_per_blk * n_chunks

    o_ref[...] = jnp.zeros_like(o_ref)
    for r in (s_ref, w_s, u_s, qk_s, qg_s, kt_s, gl_s):
        r[...] = jnp.zeros_like(r)

    def locate(step):
        seq = step // n_chunks
        kk = step % n_chunks
        cf = seq * n_chunks + kk
        cb = seq * n_chunks + (n_chunks - 1 - kk)
        rows = [pl.multiple_of(cf * DN_CHUNK, DN_CHUNK), pl.multiple_of(cb * DN_CHUNK, DN_CHUNK)]
        return seq, kk, rows, [cf, cb]

    def body(t, carry):
        tb = jnp.maximum(t, 0)
        valid = t >= 0
        seq, kk, rows_b, _ = locate(tb)
        first = kk == 0
        vnews = []
        for ch in range(DN_CHAINS):
            d, h = divmod(ch, DN_HEADS)
            s0 = s0_ref[0, d, h] if has_s0 else jnp.zeros((DN_HEAD_DIM, DN_HEAD_DIM), F32)
            s_ref[ch] = jnp.where(first, s0, s_ref[ch])
        for ch in range(DN_CHAINS):
            vnews.append((u_s[ch] - _dot(w_s[ch], s_ref[ch].astype(BF16))).astype(BF16))
        tn = jnp.minimum(t + 1, n_steps - 1)
        _, _, rows_a, chunks_a = locate(tn)
        items = _dn_prepare(q_ref, k_ref, v_ref, gc_ref, gt_ref, rows_a, chunks_a)
        for ch in range(DN_CHAINS):
            d, h = divmod(ch, DN_HEADS)
            state = s_ref[ch]
            o = _dot(qg_s[ch], state.astype(BF16)) + _dot(qk_s[ch], vnews[ch])
            s_new = state * gl_s[ch][0:1, 0:1] + _dot_tn(kt_s[ch], vnews[ch])
            s_ref[ch] = s_new
            if write_fin:
                fin_ref[seq, d, h] = s_new
            rows = pl.ds(rows_b[d], DN_CHUNK)
            cols = slice(h * DN_HEAD_DIM, (h + 1) * DN_HEAD_DIM)
            o_ref[rows, cols] = o_ref[rows, cols] + jnp.where(valid, o, 0.0)
        for ch, it in enumerate(items):
            w_s[ch] = it['r'][:, :DN_HEAD_DIM].astype(BF16)
            u_s[ch] = it['r'][:, DN_HEAD_DIM:]
            qk_s[ch] = it['qk']
            qg_s[ch] = it['qg']
            kt_s[ch] = it['kt']
            gl_s[ch] = it['gl']
        return carry

    lax.fori_loop(-1, n_steps, body, 0)


def _dn_scan(qkvn, gcb, gcb_t, seq_len, n_seq, seqs_per_blk, first_tok, s0, write_fin):
    n_chunks = seq_len // DN_CHUNK
    rows_blk = seqs_per_blk * seq_len
    b0 = first_tok // rows_blk
    hd = DN_HEAD_DIM
    st_spec = pl.BlockSpec((seqs_per_blk, 2, DN_HEADS, hd, hd), lambda s: (s, 0, 0, 0, 0))
    in_specs = [pl.BlockSpec((rows_blk, DN_WIDTH), lambda s: (b0 + s, 0)),
                pl.BlockSpec((rows_blk, DN_WIDTH), lambda s: (b0 + s, 1)),
                pl.BlockSpec((rows_blk, DN_WIDTH), lambda s: (b0 + s, 2)),
                pl.BlockSpec((rows_blk, 128), lambda s: (b0 + s, 0)),
                pl.BlockSpec((rows_blk // DN_CHUNK, 16, DN_CHUNK), lambda s: (b0 + s, 0, 0))]
    args = [qkvn, qkvn, qkvn, gcb, gcb_t]
    if s0 is not None:
        assert seqs_per_blk == 1
        in_specs.append(st_spec)
        args.append(s0)
    out_shape = [jax.ShapeDtypeStruct((n_seq * seq_len, DN_WIDTH), F32)]
    out_specs = [pl.BlockSpec((rows_blk, DN_WIDTH), lambda s: (s, 0))]
    if write_fin:
        out_shape.append(jax.ShapeDtypeStruct((n_seq, 2, DN_HEADS, hd, hd), F32))
        out_specs.append(st_spec)
    c = DN_CHUNK
    scratch = [pltpu.VMEM((DN_CHAINS, hd, hd), F32),
               pltpu.VMEM((DN_CHAINS, c, hd), BF16),
               pltpu.VMEM((DN_CHAINS, c, hd), F32),
               pltpu.VMEM((DN_CHAINS, c, c), BF16),
               pltpu.VMEM((DN_CHAINS, c, hd), BF16),
               pltpu.VMEM((DN_CHAINS, c, hd), BF16),
               pltpu.VMEM((DN_CHAINS, 8, 128), F32)]
    return pl.pallas_call(
        functools.partial(_dn_scan_kernel, n_chunks=n_chunks, seq_len=seq_len, seqs_per_blk=seqs_per_blk,
                          has_s0=s0 is not None, write_fin=write_fin),
        out_shape=out_shape,
        grid=(n_seq // seqs_per_blk,),
        in_specs=in_specs,
        out_specs=out_specs,
        scratch_shapes=scratch,
        compiler_params=_cparams(("parallel",), VMEM_LIMIT),
        name="dn_scan_%d" % seq_len,
    )(*args)


def _deltanet(qkvn, gcb, gcb_t, s0_lat, n_ctx_seq, ctx_seq, n_lat_seq, lat_seq):
    n_ctx_tok = n_ctx_seq * ctx_seq
    ctx_blk = math.gcd(n_ctx_seq, 8)
    o_ctx, fin = _dn_scan(qkvn, gcb, gcb_t, ctx_seq, n_ctx_seq, ctx_blk, 0, None, True)
    (o_lat,) = _dn_scan(qkvn, gcb, gcb_t, lat_seq, n_lat_seq, 1, n_ctx_tok, s0_lat, False)
    return o_ctx, o_lat, fin


MIX_TILE = 512


def _mix_kernel(*refs, n_x, n_ctx_tiles):
    x_refs = refs[:n_x]
    (mod_ref, ys_ref, octx_ref, olat_ref, z_ref, cu_ref, cv_ref,
     wglu_ref, bglu_ref, dnn_ref, cmn_ref, ws_ref, bs_ref, wo_ref, n2_ref,
     x1_ref, h2_ref, scr) = refs[n_x:]
    mod = mod_ref[0]
    gate1 = mod[:, 2 * D_MODEL:3 * D_MODEL]
    shift2 = mod[:, 3 * D_MODEL:4 * D_MODEL]
    scale2 = mod[:, 4 * D_MODEL:5 * D_MODEL]
    n_rows = ys_ref.shape[1]
    for half in range(2):
        for t in range(S5_T):
            scr[half, pl.ds(t, n_rows, stride=S5_T), :] = ys_ref[half, :, t * 128:(t + 1) * 128]
    ya = _gelu(jnp.concatenate([scr[0], scr[1]], axis=1))
    ya = ya * _sigmoid(_dot(ya.astype(BF16), wglu_ref[...]) + bglu_ref[...])
    mix = _dot(ya.astype(BF16), wo_ref[0:SSM_CH, :])
    o = jnp.where(pl.program_id(0) < n_ctx_tiles, octx_ref[...], olat_ref[...])
    z = z_ref[...]
    parts = []
    for h in range(DN_HEADS):
        sl = slice(h * DN_HEAD_DIM, (h + 1) * DN_HEAD_DIM)
        parts.append(_rms(o[:, sl]) * dnn_ref[...] * _silu(z[:, sl]))
    yb = jnp.concatenate(parts, axis=1)
    mix = mix + _dot(yb.astype(BF16), wo_ref[SSM_CH:SSM_CH + DN_WIDTH, :])
    u = _gelu(cu_ref[...])
    v = _gelu(cv_ref[...])
    seg_i = lax.broadcasted_iota(jnp.int32, (CM_WIDTH, CM_WIDTH), 0) // CM_HEAD_DIM
    seg_j = lax.broadcasted_iota(jnp.int32, (CM_WIDTH, CM_WIDTH), 1) // CM_HEAD_DIM
    seg_ones = jnp.where(seg_i == seg_j, 1.0, 0.0).astype(BF16)
    vv = v * v
    vv_hi = vv.astype(BF16)
    vv_lo = (vv - vv_hi.astype(F32)).astype(BF16)
    ms = (_dot(vv_hi, seg_ones) + _dot(vv_lo, seg_ones)) * (1.0 / CM_HEAD_DIM)
    vn = (v * lax.rsqrt(ms + EPS) * cmn_ref[...]).astype(BF16)
    head_of_lane = lax.broadcasted_iota(jnp.int32, (1, CM_WIDTH), 1) // CM_HEAD_DIM
    ycs = []
    for c in range(MIX_TILE // CM_CHUNK):
        vc = vn[c * CM_CHUNK:(c + 1) * CM_CHUNK]
        stack = jnp.concatenate([jnp.where(head_of_lane == h, vc, jnp.zeros_like(vc))
                                 for h in range(CM_HEADS)], axis=0)
        vs = _dot(ws_ref[...], stack) + bs_ref[...]
        ycs.append(u[c * CM_CHUNK:(c + 1) * CM_CHUNK] * vs)
    yc = jnp.concatenate(ycs, axis=0)
    mix = mix + _dot(yc.astype(BF16), wo_ref[SSM_CH + DN_WIDTH:, :])
    x1 = _token_tile(x_refs, n_ctx_tiles) + gate1 * mix
    x1_ref[...] = x1
    h2_ref[...] = (_rms(x1) * n2_ref[...] * (1.0 + scale2) + shift2).astype(BF16)


def _mix_out(xs, mods_l, ys, o_ctx, o_lat, z, cu, cv, p, n_ctx_tok, lat_seq):
    n_tok = sum(x.shape[0] for x in xs)
    tm = MIX_TILE
    n_ctx_tiles = n_ctx_tok // tm
    tps = lat_seq // tm
    row = lambda i: (i, 0)
    const = lambda i: (0, 0)
    ctx_row = lambda i: (jnp.minimum(i, n_ctx_tiles - 1), 0)
    lat_row = lambda i: (jnp.maximum(i - n_ctx_tiles, 0), 0)
    ws_cat = jnp.concatenate([p['cm_w_s'][h] for h in range(CM_HEADS)], axis=1).astype(BF16)
    bs_exp = jnp.repeat(p['cm_b_s'].T, CM_HEAD_DIM, axis=1)
    return pl.pallas_call(
        functools.partial(_mix_kernel, n_x=len(xs), n_ctx_tiles=n_ctx_tiles),
        out_shape=[jax.ShapeDtypeStruct((n_tok, D_MODEL), F32),
                   jax.ShapeDtypeStruct((n_tok, D_MODEL), BF16)],
        grid=(n_tok // tm,),
        in_specs=_token_specs(xs, tm, n_ctx_tiles, D_MODEL)
        + [pl.BlockSpec((1, 1, N_MOD * D_MODEL),
                               lambda i: (_mod_row_index(i, n_ctx_tiles, tps), 0, 0)),
                  pl.BlockSpec((2, tm // S5_T, S5_T * 128), lambda i: (0, i, 0)),
                  pl.BlockSpec((tm, DN_WIDTH), ctx_row),
                  pl.BlockSpec((tm, DN_WIDTH), lat_row),
                  pl.BlockSpec((tm, DN_WIDTH), row),
                  pl.BlockSpec((tm, CM_WIDTH), row),
                  pl.BlockSpec((tm, CM_WIDTH), row),
                  pl.BlockSpec((SSM_CH, SSM_CH), const),
                  pl.BlockSpec((1, SSM_CH), const),
                  pl.BlockSpec((1, DN_HEAD_DIM), const),
                  pl.BlockSpec((1, CM_WIDTH), const),
                  pl.BlockSpec((CM_CHUNK, CM_HEADS * CM_CHUNK), const),
                  pl.BlockSpec((CM_CHUNK, CM_WIDTH), const),
                  pl.BlockSpec((D_MODEL, D_MODEL), const),
                  pl.BlockSpec((1, D_MODEL), const)],
        out_specs=[pl.BlockSpec((tm, D_MODEL), row), pl.BlockSpec((tm, D_MODEL), row)],
        scratch_shapes=[pltpu.VMEM((2, tm, 128), F32)],
        compiler_params=_cparams(("parallel",), VMEM_LIMIT),
        name="mix_out",
    )(*xs, mods_l, ys, o_ctx, o_lat, z, cu, cv,
      p['ssm_w_glu'].astype(BF16), p['ssm_b_glu'].reshape(1, SSM_CH),
      p['dn_norm'].reshape(1, DN_HEAD_DIM), p['cm_norm'].reshape(1, CM_WIDTH),
      ws_cat, bs_exp, p['w_out'].astype(BF16), p['norm2'].reshape(1, D_MODEL))


ROUTE_TILE = 512


def _first_max_mask(x, idx, axis):
    m = jnp.max(x, axis=axis, keepdims=True)
    first = jnp.min(jnp.where(x == m, idx, jnp.int32(1 << 20)), axis=axis, keepdims=True)
    return idx == first


def _router_kernel(h_ref, wr_ref, bias_ref, g_ref):
    n = h_ref.shape[0]
    per = N_EXPERTS // N_ROUTE_GROUPS
    logits = _dot_nt(wr_ref[...], h_ref[...])
    scores = _sigmoid(logits)
    choice = scores + bias_ref[...]
    ninf = jnp.float32(-jnp.inf)
    c3 = choice.reshape(N_ROUTE_GROUPS, per, n)
    i3 = lax.broadcasted_iota(jnp.int32, c3.shape, 1)
    top1 = _first_max_mask(c3, i3, 1)
    m1 = jnp.max(c3, axis=1)
    m2 = jnp.max(jnp.where(top1, ninf, c3), axis=1)
    grp = m1 + m2
    gi = lax.broadcasted_iota(jnp.int32, grp.shape, 0)
    gsel = jnp.zeros(grp.shape, jnp.bool_)
    work = grp
    for _ in range(TOPK_ROUTE_GROUPS):
        pick = _first_max_mask(work, gi, 0)
        gsel = gsel | pick
        work = jnp.where(pick, ninf, work)
    gmask = jnp.broadcast_to(gsel[:, None, :], c3.shape).reshape(N_EXPERTS, n)
    work = jnp.where(gmask, choice, ninf)
    ei = lax.broadcasted_iota(jnp.int32, work.shape, 0)
    esel = jnp.zeros(work.shape, jnp.bool_)
    for _ in range(TOP_K):
        pick = _first_max_mask(work, ei, 0)
        esel = esel | pick
        work = jnp.where(pick, ninf, work)
    wts = jnp.where(esel, scores, 0.0)
    gates_t = wts / (jnp.sum(wts, axis=0, keepdims=True) + 1e-20) * ROUTED_SCALE
    gates = gates_t.T
    for j in range(N_EXPERTS // MOE_EG):
        g_ref[j] = gates[:, j * MOE_EG:(j + 1) * MOE_EG]


def _router(h2, router_w, router_bias):
    n_tok = h2.shape[0]
    tm = ROUTE_TILE
    neg = N_EXPERTS // MOE_EG
    return pl.pallas_call(
        _router_kernel,
        out_shape=jax.ShapeDtypeStruct((neg, n_tok, MOE_EG), F32),
        grid=(n_tok // tm,),
        in_specs=[pl.BlockSpec((tm, D_MODEL), lambda i: (i, 0)),
                  pl.BlockSpec((N_EXPERTS, D_MODEL), lambda i: (0, 0)),
                  pl.BlockSpec((N_EXPERTS, 1), lambda i: (0, 0))],
        out_specs=pl.BlockSpec((neg, tm, MOE_EG), lambda i: (0, i, 0)),
        compiler_params=_cparams(("parallel",)),
        name="router",
    )(h2, router_w.T.astype(BF16), router_bias.reshape(N_EXPERTS, 1))


MOE_TILE = 1024
MOE_EG = 8


def _moe_kernel(h_ref, x_ref, mod_ref, g_ref, wg_ref, wu_ref, wd_ref, sg_ref, su_ref, sd_ref, nf_ref,
                *out_refs, final, n_ctx_tiles):
    j = pl.program_id(1)
    nj = pl.num_programs(1)
    acc_ref = out_refs[-1]
    h = h_ref[...]

    @pl.when(j == 0)
    def _():
        sh = _silu(_dot(h, sg_ref[...])) * _dot(h, su_ref[...])
        acc_ref[...] = _dot(sh.astype(BF16), sd_ref[...])

    hg = _dot(h, wg_ref[...])
    hu = _dot(h, wu_ref[...])
    gates = g_ref[0]
    cols = []
    for e in range(MOE_EG):
        sl = slice(e * EXPERT_FF, (e + 1) * EXPERT_FF)
        cols.append((_silu(hg[:, sl]) * hu[:, sl] * gates[:, e:e + 1]).astype(BF16))
    act = jnp.concatenate(cols, axis=1)
    acc_ref[...] += _dot(act, wd_ref[...])

    @pl.when(j == nj - 1)
    def _():
        gate2 = mod_ref[0][:, 5 * D_MODEL:6 * D_MODEL]
        x2 = x_ref[...] + gate2 * acc_ref[...]
        if final:
            y = _rms(x2) * nf_ref[...]
            is_ctx = pl.program_id(0) < n_ctx_tiles

            @pl.when(is_ctx)
            def _():
                out_refs[0][...] = y

            @pl.when(jnp.logical_not(is_ctx))
            def _():
                out_refs[1][...] = y
        else:
            out_refs[0][...] = x2


def _expert_cols_kernel(w_ref, o_ref):
    for e in range(w_ref.shape[0]):
        o_ref[:, e * EXPERT_FF:(e + 1) * EXPERT_FF] = w_ref[e].astype(BF16)


def _expert_cols(w, layer):
    _, n_e, d, f = w.shape
    return pl.pallas_call(
        _expert_cols_kernel,
        out_shape=jax.ShapeDtypeStruct((d, n_e * f), BF16),
        grid=(n_e // MOE_EG,),
        in_specs=[pl.BlockSpec((None, MOE_EG, d, f), lambda j: (layer, j, 0, 0))],
        out_specs=pl.BlockSpec((d, MOE_EG * f), lambda j: (0, j)),
        compiler_params=_cparams(("parallel",)),
        name="expert_cols",
    )(w)


def _cast_rows_kernel(w_ref, o_ref):
    o_ref[...] = w_ref[...].astype(BF16)


def _cast_rows(w, layer, rows):
    _, r, c = w.shape
    return pl.pallas_call(
        _cast_rows_kernel,
        out_shape=jax.ShapeDtypeStruct((r, c), BF16),
        grid=(r // rows,),
        in_specs=[pl.BlockSpec((None, rows, c), lambda j: (layer, j, 0))],
        out_specs=pl.BlockSpec((rows, c), lambda j: (j, 0)),
        compiler_params=_cparams(("parallel",)),
        name="cast_rows",
    )(w)


def _moe(h2, x1, mods_l, gates_t, w_gate, w_up, w_down, layer, p, norm_f, final, n_ctx_tok, lat_seq):
    n_tok = h2.shape[0]
    tm = MOE_TILE
    n_ctx_tiles = n_ctx_tok // tm
    tps = lat_seq // tm
    neg = N_EXPERTS // MOE_EG
    ef = MOE_EG * EXPERT_FF
    wg = _expert_cols(w_gate, layer)
    wu = _expert_cols(w_up, layer)
    wd = _cast_rows(w_down.reshape(w_down.shape[0], N_EXPERTS * EXPERT_FF, D_MODEL), layer, 1024)
    row = lambda i, j: (i, 0)
    const = lambda i, j: (0, 0)
    if final:
        out_shape = [jax.ShapeDtypeStruct((n_ctx_tok, D_MODEL), F32),
                     jax.ShapeDtypeStruct((n_tok - n_ctx_tok, D_MODEL), F32)]
        out_specs = [pl.BlockSpec((tm, D_MODEL), lambda i, j: (jnp.minimum(i, n_ctx_tiles - 1), 0)),
                     pl.BlockSpec((tm, D_MODEL), lambda i, j: (jnp.maximum(i - n_ctx_tiles, 0), 0))]
    else:
        out_shape = jax.ShapeDtypeStruct((n_tok, D_MODEL), F32)
        out_specs = pl.BlockSpec((tm, D_MODEL), row)
    out = pl.pallas_call(
        functools.partial(_moe_kernel, final=final, n_ctx_tiles=n_ctx_tiles),
        out_shape=out_shape,
        grid=(n_tok // tm, neg),
        in_specs=[pl.BlockSpec((tm, D_MODEL), row),
                  pl.BlockSpec((tm, D_MODEL), row),
                  pl.BlockSpec((1, 1, N_MOD * D_MODEL),
                               lambda i, j: (_mod_row_index(i, n_ctx_tiles, tps), 0, 0)),
                  pl.BlockSpec((1, tm, MOE_EG), lambda i, j: (j, i, 0)),
                  pl.BlockSpec((D_MODEL, ef), lambda i, j: (0, j)),
                  pl.BlockSpec((D_MODEL, ef), lambda i, j: (0, j)),
                  pl.BlockSpec((ef, D_MODEL), lambda i, j: (j, 0)),
                  pl.BlockSpec((D_MODEL, SHARED_FF), const),
                  pl.BlockSpec((D_MODEL, SHARED_FF), const),
                  pl.BlockSpec((SHARED_FF, D_MODEL), const),
                  pl.BlockSpec((1, D_MODEL), const)],
        out_specs=out_specs,
        scratch_shapes=[pltpu.VMEM((tm, D_MODEL), F32)],
        compiler_params=_cparams(("arbitrary", "arbitrary"), VMEM_LIMIT_MAX if final else VMEM_LIMIT),
        name="moe",
    )(h2, x1, mods_l, gates_t, wg, wu, wd,
      p['moe_ws_gate'].astype(BF16), p['moe_ws_up'].astype(BF16), p['moe_ws_down'].astype(BF16),
      norm_f.reshape(1, D_MODEL))
    return out


def kernel(x_prompt, x_sample, state_ssm, state_delta, c, c_ctx, w_ada, b_ada, norm1, norm2, w_in, w_out, ssm_lam_re, ssm_lam_im, ssm_log_dt, ssm_b_re, ssm_b_im, ssm_c_re, ssm_c_im, ssm_d, ssm_w_glu, ssm_b_glu, dn_conv, dn_a_log, dn_dt_bias, dn_norm, cm_norm, cm_w_s, cm_b_s, moe_router, moe_router_bias, moe_w_gate, moe_w_up, moe_w_down, moe_ws_gate, moe_ws_up, moe_ws_down, norm_f):
    n_ctx_seq, ctx_seq, _ = x_prompt.shape
    n_lat_seq, lat_seq, _ = x_sample.shape
    depth = w_ada.shape[0]
    n_ctx_tok = n_ctx_seq * ctx_seq
    n_seq = n_ctx_seq + n_lat_seq
    g = SSM_GROUPS

    xs = [x_prompt.reshape(n_ctx_tok, D_MODEL), x_sample.reshape(n_lat_seq * lat_seq, D_MODEL)]
    cvec = jnp.concatenate([c_ctx[None, :], c, jnp.zeros((8 - 1 - n_lat_seq, D_MODEL), F32)], axis=0)
    mods = _ada_mods(cvec, w_ada, b_ada)

    ssm_states, dn_states = [], []
    y = None
    for l in range(depth):
        p = {'w_out': w_out[l], 'norm2': norm2[l], 'ssm_w_glu': ssm_w_glu[l], 'ssm_b_glu': ssm_b_glu[l],
             'dn_conv': dn_conv[l], 'dn_a_log': dn_a_log[l], 'dn_dt_bias': dn_dt_bias[l],
             'dn_norm': dn_norm[l], 'cm_norm': cm_norm[l], 'cm_w_s': cm_w_s[l], 'cm_b_s': cm_b_s[l],
             'moe_ws_gate': moe_ws_gate[l], 'moe_ws_up': moe_ws_up[l], 'moe_ws_down': moe_ws_down[l]}
        mods_l = mods[l].reshape(8, 1, N_MOD * D_MODEL)
        s_u, z, cu, cv, qkvn, gcb, gcb_t = _in_projection(
            xs, mods_l, norm1[l], _cast_rows(w_in, l, 256), dn_conv[l], dn_a_log[l], dn_dt_bias[l],
            n_ctx_tok, ctx_seq, lat_seq)

        mats = _s5_prep(ssm_lam_re[l], ssm_lam_im[l], ssm_log_dt[l], ssm_b_re[l], ssm_b_im[l],
                        ssm_c_re[l], ssm_c_im[l])
        h0_lat = state_ssm[:, l].transpose(0, 3, 2, 1, 4).reshape(n_lat_seq, g, 256)
        h0 = jnp.concatenate([jnp.zeros((n_ctx_seq, g, 256), F32), h0_lat], axis=0)
        ys, ssm_fin = _s5_mixer(s_u, mats, ssm_d[l], h0, n_ctx_seq, ctx_seq, lat_seq)
        ssm_states.append(ssm_fin[:n_ctx_seq].reshape(n_ctx_seq, g, 2, 2, SSM_STATE).transpose(0, 3, 2, 1, 4))

        o_ctx, o_lat, dn_fin = _deltanet(qkvn, gcb, gcb_t, state_delta[:, l], n_ctx_seq, ctx_seq, n_lat_seq,
                                         lat_seq)
        dn_states.append(dn_fin)

        x1, h2 = _mix_out(xs, mods_l, ys, o_ctx, o_lat, z, cu, cv, p, n_ctx_tok, lat_seq)
        gates_t = _router(h2, moe_router[l], moe_router_bias[l])
        x = _moe(h2, x1, mods_l, gates_t, moe_w_gate, moe_w_up, moe_w_down, l, p, norm_f, l == depth - 1,
                 n_ctx_tok, lat_seq)
        xs = list(x) if isinstance(x, (list, tuple)) else [x]

    y_prompt = xs[0].reshape(x_prompt.shape)
    y_sample = xs[1].reshape(x_sample.shape)
    new_state_ssm = jnp.stack(ssm_states, axis=1)
    new_state_delta = jnp.stack(dn_states, axis=1)
    return (y_prompt, y_sample, new_state_ssm, new_state_delta)
```
